```python
import math
import jax, jax.numpy as jnp
from jax import lax
import numpy as np

D_MODEL = 1024
BATCH = 32
SEQ = 256
DEPTH = 4
DEC_BATCH = 2
DEC_SEQ = 2048
PAST_LEN = 512

GRID_W = 64
Q_BLOCK = 128
D_FF = 2816
N_MOD = 9
EPS = 1e-6
ROPE_BASE = 10000.0
A_HEADS = 6
A_KV_HEADS = 2
A_HEAD_DIM = 64
B_HEADS = 4
B_KEY_DIM = 32
B_VAL_DIM = 64
B_GATE_RANK = 16
B_GATE_TAU = 16.0
B_CHUNK = 64
C_HEADS = 4
C_QK_DIM = 48
C_VAL_DIM = 2 * C_QK_DIM

A_WIDTH = A_HEADS * A_HEAD_DIM
B_WIDTH = B_HEADS * B_VAL_DIM
C_WIDTH = C_HEADS * C_VAL_DIM
MIX_WIDTH = A_WIDTH + B_WIDTH + C_WIDTH
PROJ_SPLITS = (A_WIDTH, A_KV_HEADS * A_HEAD_DIM, A_KV_HEADS * A_HEAD_DIM,
               B_HEADS * B_KEY_DIM, B_HEADS * B_KEY_DIM, B_WIDTH, 2 * B_GATE_RANK, B_WIDTH,
               2 * C_HEADS * C_QK_DIM, 2 * C_HEADS * C_QK_DIM, C_WIDTH)
PROJ_WIDTH = sum(PROJ_SPLITS)

kernel_name = 'hybrid_dit_prefix_ctx_step'


def rms_norm(x, g):
    xf = x.astype(jnp.float32)
    y = xf * lax.rsqrt(jnp.mean(xf * xf, axis=-1, keepdims=True) + EPS)
    return (y * g.astype(jnp.float32)).astype(x.dtype)


def modulate(x, shift, scale):
    return x * (1 + scale[:, None, :]) + shift[:, None, :]


def swiglu(h, wg, wu, wd):
    return (jax.nn.silu(h @ wg) * (h @ wu)) @ wd


def rope_1d(x, pos):
    m = x.shape[-1] // 2
    freqs = ROPE_BASE ** (-jnp.arange(m, dtype=jnp.float32) / m)
    ang = pos[:, None] * freqs[None, :]
    cs = jnp.cos(ang)[:, None, :].astype(x.dtype)
    sn = jnp.sin(ang)[:, None, :].astype(x.dtype)
    a, b = x[..., :m], x[..., m:]
    return jnp.concatenate([a * cs - b * sn, a * sn + b * cs], axis=-1)


def rope_2d(x, row, col):
    h = x.shape[-1] // 2
    return jnp.concatenate([rope_1d(x[..., :h], row), rope_1d(x[..., h:], col)], axis=-1)


def map_query_blocks(fn, q):
    B, T = q.shape[:2]
    nb = T // Q_BLOCK
    qb = jnp.moveaxis(q.reshape((B, nb, Q_BLOCK) + q.shape[2:]), 1, 0)
    out = jnp.moveaxis(lax.map(fn, qb), 0, 1)
    return out.reshape((B, T) + out.shape[3:])


def gqa_attention(q, k, v):
    scale = A_HEAD_DIM ** -0.5
    def block(qb):
        s = jnp.einsum('bqhgd,bkhd->bhgqk', qb, k).astype(jnp.float32) * scale
        p = jax.nn.softmax(s, axis=-1).astype(v.dtype)
        return jnp.einsum('bhgqk,bkhd->bqhgd', p, v)
    return map_query_blocks(block, q)


def diff_attention(q, k, v, lam):
    scale = C_QK_DIM ** -0.5
    def block(qb):
        s = jnp.einsum('bqhmd,bkhmd->bhmqk', qb, k).astype(jnp.float32) * scale
        p = jax.nn.softmax(s, axis=-1)
        w = (p[:, :, 0] - lam * p[:, :, 1]).astype(v.dtype)
        return jnp.einsum('bhqk,bkhd->bqhd', w, v)
    return map_query_blocks(block, q)


def gla_scan(q, k, v, log_a, s0):
    B, T, H, dk = q.shape
    dv = v.shape[-1]
    n = T // B_CHUNK
    def chunks(x):
        x = x.astype(jnp.float32).reshape(B, n, B_CHUNK, H, x.shape[-1])
        return jnp.transpose(x, (1, 0, 3, 2, 4))
    qc = chunks(q) * (dk ** -0.5)
    kc, vc, lc = chunks(k), chunks(v), chunks(log_a)
    b = jnp.cumsum(lc, axis=3)
    b_last = b[:, :, :, -1:, :]
    mask = jnp.tril(jnp.ones((B_CHUNK, B_CHUNK), dtype=bool))[:, :, None]
    diff = b[..., :, None, :] - b[..., None, :, :]
    decay = jnp.where(mask, jnp.exp(jnp.where(mask, diff, 0.0)), 0.0)
    scores = jnp.einsum('nbhid,nbhjd,nbhijd->nbhij', qc, kc, decay)
    o_intra = jnp.einsum('nbhij,nbhjv->nbhiv', scores, vc)
    q_dec = qc * jnp.exp(b)
    k_dec = kc * jnp.exp(b_last - b)
    a_last = jnp.exp(b_last[:, :, :, 0, :])
    def step(S, xs):
        qd, kd, vv, al = xs
        o = jnp.einsum('bhid,bhdv->bhiv', qd, S)
        S = al[..., None] * S + jnp.einsum('bhjd,bhjv->bhdv', kd, vv)
        return S, o
    s_fin, o_inter = lax.scan(step, s0.astype(jnp.float32), (q_dec, k_dec, vc, a_last))
    o = jnp.transpose(o_intra + o_inter, (1, 0, 3, 2, 4)).reshape(B, T, H, dv)
    return o.astype(v.dtype), s_fin.astype(v.dtype)


def gla_bidir(q, k, v, log_a_f, log_a_b, s0_f, s0_b):
    o_f, s_f = gla_scan(q, k, v, log_a_f, s0_f)
    flip = lambda x: jnp.flip(x, axis=1)
    o_b, s_b = gla_scan(flip(q), flip(k), flip(v), flip(log_a_b), s0_b)
    return o_f + flip(o_b), s_f, s_b


def token_mix(h, lp, lam_init, ctx, pos):
    B, T, _ = h.shape
    offs = [int(o) for o in np.cumsum(PROJ_SPLITS)[:-1]]
    a_q, a_k, a_v, b_q, b_k, b_v, b_g, b_r, c_q, c_k, c_v = jnp.split(h @ lp['w_in'], offs, axis=-1)
    a_q = rms_norm(a_q.reshape(B, T, A_HEADS, A_HEAD_DIM), lp['g_a_q'])
    a_k = rms_norm(a_k.reshape(B, T, A_KV_HEADS, A_HEAD_DIM), lp['g_a_k'])
    a_v = a_v.reshape(B, T, A_KV_HEADS, A_HEAD_DIM)
    c_q = rms_norm(c_q.reshape(B, T, C_HEADS, 2, C_QK_DIM), lp['g_c_q'])
    c_k = rms_norm(c_k.reshape(B, T, C_HEADS, 2, C_QK_DIM), lp['g_c_k'])
    c_v = c_v.reshape(B, T, C_HEADS, C_VAL_DIM)
    b_q = b_q.reshape(B, T, B_HEADS, B_KEY_DIM)
    b_k = b_k.reshape(B, T, B_HEADS, B_KEY_DIM)
    b_v = b_v.reshape(B, T, B_HEADS, B_VAL_DIM)
    z = jnp.einsum('btsr,srk->btsk', b_g.reshape(B, T, 2, B_GATE_RANK), lp['w_gla_up']) + lp['b_gla']
    log_a = (jax.nn.log_sigmoid(z.astype(jnp.float32)) / B_GATE_TAU).reshape(B, T, 2, B_HEADS, B_KEY_DIM)
    if ctx is None:
        keys_a, vals_a, keys_c, vals_c = a_k, a_v, c_k, c_v
        s0 = jnp.zeros((B, 2, B_HEADS, B_KEY_DIM, B_VAL_DIM), jnp.float32)
    else:
        row, col = pos
        ctx_a_k, ctx_a_v, ctx_c_k, ctx_c_v, s0 = ctx
        L = ctx_a_k.shape[1]
        a_q = rope_2d(a_q, row, col)
        keys_a = jnp.concatenate([rope_2d(a_k, row, col), ctx_a_k], axis=1)
        vals_a = jnp.concatenate([a_v, ctx_a_v], axis=1)
        c_q = rope_2d(c_q.reshape(B, T, 2 * C_HEADS, C_QK_DIM), row, col).reshape(B, T, C_HEADS, 2, C_QK_DIM)
        c_k_lat = rope_2d(c_k.reshape(B, T, 2 * C_HEADS, C_QK_DIM), row, col).reshape(B, T, C_HEADS, 2, C_QK_DIM)
        keys_c = jnp.concatenate([c_k_lat, ctx_c_k.reshape(B, L, C_HEADS, 2, C_QK_DIM)], axis=1)
        vals_c = jnp.concatenate([c_v, ctx_c_v], axis=1)
    out_a = gqa_attention(a_q.reshape(B, T, A_KV_HEADS, A_HEADS // A_KV_HEADS, A_HEAD_DIM),
                          keys_a, vals_a).reshape(B, T, A_WIDTH)
    o_b, s_f, s_b = gla_bidir(b_q, b_k, b_v, log_a[:, :, 0], log_a[:, :, 1], s0[:, 0], s0[:, 1])
    out_b = (rms_norm(o_b, lp['g_gla']) * jax.nn.silu(b_r).reshape(B, T, B_HEADS, B_VAL_DIM)).reshape(B, T, B_WIDTH)
    lq1, lk1, lq2, lk2 = lp['lam_c'].astype(jnp.float32)
    lam = jnp.exp(jnp.sum(lq1 * lk1)) - jnp.exp(jnp.sum(lq2 * lk2)) + lam_init
    out_c = diff_attention(c_q, keys_c, vals_c, lam)
    out_c = (rms_norm(out_c, lp['g_c_out']) * (1.0 - lam_init)).reshape(B, T, C_WIDTH)
    out = jnp.concatenate([out_a, out_b, out_c], axis=-1) @ lp['w_out']
    if ctx is None:
        new_ctx = (a_k, a_v, c_k.reshape(B, T, C_HEADS, 2 * C_QK_DIM), c_v, jnp.stack([s_f, s_b], axis=1))
        return out, new_ctx
    return out, None


def adaln(cond, lp):
    return (jax.nn.silu(cond) @ lp['w_ada'] + lp['b_ada']).reshape(cond.shape[0], N_MOD, D_MODEL)


def trunk_layer(x, mod, lp, lam_init, ctx=None, pos=None):
    sh1, sc1, g1, sh2, sc2, g2, sh3, sc3, g3 = (mod[:, i] for i in range(N_MOD))
    h = modulate(rms_norm(x, lp['g_norm'][0]), sh1, sc1)
    x = x + 0.5 * g1[:, None, :] * swiglu(h, lp['w_ffn_gate'][0], lp['w_ffn_up'][0], lp['w_ffn_down'][0])
    h = modulate(rms_norm(x, lp['g_norm'][1]), sh2, sc2)
    mix, new_ctx = token_mix(h, lp, lam_init, ctx, pos)
    x = x + g2[:, None, :] * mix
    h = modulate(rms_norm(x, lp['g_norm'][2]), sh3, sc3)
    x = x + 0.5 * g3[:, None, :] * swiglu(h, lp['w_ffn_gate'][1], lp['w_ffn_up'][1], lp['w_ffn_down'][1])
    return x, new_ctx


def setup_inputs(seed: int = 0) -> dict:
    key = jax.random.key(seed)
    ks = jax.random.split(key, 32)
    nrm = lambda k, shape, s: jax.random.normal(k, shape, jnp.float32) * s
    gain = lambda k, shape: 1.0 + 0.02 * jax.random.normal(k, shape, jnp.float32)
    return {
        'x_prompt': nrm(ks[0], (BATCH, SEQ, D_MODEL), 1.0),
        'x_sample': nrm(ks[1], (DEC_BATCH, DEC_SEQ, D_MODEL), 1.0),
        'c': nrm(ks[2], (DEC_BATCH, D_MODEL), 1.0),
        'cache_a_k': nrm(ks[3], (DEC_BATCH, DEPTH, PAST_LEN, A_KV_HEADS, A_HEAD_DIM), 1.0),
        'cache_a_v': nrm(ks[4], (DEC_BATCH, DEPTH, PAST_LEN, A_KV_HEADS, A_HEAD_DIM), 1.0),
        'cache_c_k': nrm(ks[5], (DEC_BATCH, DEPTH, PAST_LEN, C_HEADS, 2 * C_QK_DIM), 1.0),
        'cache_c_v': nrm(ks[6], (DEC_BATCH, DEPTH, PAST_LEN, C_HEADS, C_VAL_DIM), 1.0),
        'state_gla': nrm(ks[7], (DEC_BATCH, DEPTH, 2, B_HEADS, B_KEY_DIM, B_VAL_DIM), 2.0),
        'c_ctx': nrm(ks[8], (D_MODEL,), 1.0),
        'w_ada': nrm(ks[9], (DEPTH, D_MODEL, N_MOD * D_MODEL), 0.5 * D_MODEL ** -0.5),
        'b_ada': nrm(ks[10], (DEPTH, N_MOD * D_MODEL), 0.02),
        'g_norm': gain(ks[11], (DEPTH, 3, D_MODEL)),
        'w_ffn_gate': nrm(ks[12], (DEPTH, 2, D_MODEL, D_FF), D_MODEL ** -0.5),
        'w_ffn_up': nrm(ks[13], (DEPTH, 2, D_MODEL, D_FF), D_MODEL ** -0.5),
        'w_ffn_down': nrm(ks[14], (DEPTH, 2, D_FF, D_MODEL), D_FF ** -0.5),
        'w_in': nrm(ks[15], (DEPTH, D_MODEL, PROJ_WIDTH), D_MODEL ** -0.5),
        'g_a_q': gain(ks[16], (DEPTH, A_HEAD_DIM)),
        'g_a_k': gain(ks[17], (DEPTH, A_HEAD_DIM)),
        'w_gla_up': nrm(ks[18], (DEPTH, 2, B_GATE_RANK, B_HEADS * B_KEY_DIM), B_GATE_RANK ** -0.5),
        'b_gla': nrm(ks[19], (DEPTH, 2, B_HEADS * B_KEY_DIM), 0.1),
        'g_gla': gain(ks[20], (DEPTH, B_VAL_DIM)),
        'g_c_q': gain(ks[21], (DEPTH, 2, C_QK_DIM)),
        'g_c_k': gain(ks[22], (DEPTH, 2, C_QK_DIM)),
        'lam_c': nrm(ks[23], (DEPTH, 4, C_QK_DIM), 0.1),
        'g_c_out': gain(ks[24], (DEPTH, C_VAL_DIM)),
        'w_out': nrm(ks[25], (DEPTH, MIX_WIDTH, D_MODEL), MIX_WIDTH ** -0.5),
    }


def reference(x_prompt, x_sample, c, cache_a_k, cache_a_v, cache_c_k, cache_c_v, state_gla, c_ctx,
              w_ada, b_ada, g_norm, w_ffn_gate, w_ffn_up, w_ffn_down, w_in, g_a_q, g_a_k,
              w_gla_up, b_gla, g_gla, g_c_q, g_c_k, lam_c, g_c_out, w_out):
    layers = [dict(w_ada=w_ada[l], b_ada=b_ada[l], g_norm=g_norm[l], w_ffn_gate=w_ffn_gate[l],
                   w_ffn_up=w_ffn_up[l], w_ffn_down=w_ffn_down[l], w_in=w_in[l], g_a_q=g_a_q[l],
                   g_a_k=g_a_k[l], w_gla_up=w_gla_up[l], b_gla=b_gla[l], g_gla=g_gla[l],
                   g_c_q=g_c_q[l], g_c_k=g_c_k[l], lam_c=lam_c[l], g_c_out=g_c_out[l], w_out=w_out[l])
              for l in range(DEPTH)]
    lam_inits = [0.8 - 0.6 * math.exp(-0.3 * l) for l in range(DEPTH)]

    y = x_prompt
    ak, av, ck, cv, sg = [], [], [], [], []
    for l in range(DEPTH):
        mod = adaln(c_ctx[None, :], layers[l])
        y, (k_a, v_a, k_c, v_c, s_g) = trunk_layer(y, mod, layers[l], lam_inits[l])
        ak.append(k_a); av.append(v_a); ck.append(k_c); cv.append(v_c); sg.append(s_g)
    y_prompt = y

    T = x_sample.shape[1]
    ROWS = T // GRID_W
    row = jnp.repeat(jnp.arange(ROWS), GRID_W).astype(jnp.float32)
    col = jnp.tile(jnp.arange(GRID_W), ROWS).astype(jnp.float32)
    z = x_sample
    for l in range(DEPTH):
        mod = adaln(c, layers[l])
        ctx = (cache_a_k[:, l], cache_a_v[:, l], cache_c_k[:, l], cache_c_v[:, l], state_gla[:, l])
        z, _ = trunk_layer(z, mod, layers[l], lam_inits[l], ctx, (row, col))
    y_sample = z

    return (y_prompt, y_sample, jnp.stack(ak, axis=1), jnp.stack(av, axis=1), jnp.stack(ck, axis=1),
            jnp.stack(cv, axis=1), jnp.stack(sg, axis=1))
```

```python
import functools
import math

import numpy as np
import jax
import jax.numpy as jnp
from jax import lax
from jax.experimental import pallas as pl
from jax.experimental.pallas import tpu as pltpu

F32 = jnp.float32
BF16 = jnp.bfloat16

D_MODEL = 1024
D_FF = 2816
DEPTH = 4
N_MOD = 9
EPS = 1e-6
ROPE_BASE = 10000.0
GRID_W = 64

NB_CTX, T_CTX = 32, 256
NB_LAT, T_LAT = 2, 2048
PAST_LEN = 512
N_CTX = NB_CTX * T_CTX
N_LAT = NB_LAT * T_LAT
N_TOK = N_CTX + N_LAT

A_KV, A_G, A_D = 2, 3, 64
A_W = A_KV * A_G * A_D
B_H, B_DK, B_DV = 4, 32, 64
B_KW = B_H * B_DK
B_VW = B_H * B_DV
B_RANK = 16
B_TAU = 16.0
B_CHUNK = 64
C_H, C_DQ, C_DV = 4, 48, 96
C_W = C_H * C_DV

P_AQ, P_AK, P_AV = 0, 384, 512
P_BQ, P_BK, P_BV, P_BR = 640, 768, 896, 1152
P_CQ, P_CK, P_CV = 1408, 1792, 2176
P_BG = 2560
P_W = 2688

TM_FFN = 1024
TF_FFN = 256
TM_PROJ = 256
TQ_ATT = 256
TN_ADA = 1152
VMEM_LIMIT = 56 * 1024 * 1024


def _cparams(sem):
    return pltpu.CompilerParams(dimension_semantics=sem, vmem_limit_bytes=VMEM_LIMIT)


def _dot(a, b):
    return jnp.dot(a, b, preferred_element_type=F32)


def _dot_nt(a, b):
    return lax.dot_general(a, b, (((1,), (1,)), ((), ())), preferred_element_type=F32)


def _dot_tn(a, b):
    return lax.dot_general(a, b, (((0,), (0,)), ((), ())), preferred_element_type=F32)


def _split_dot(x, m):
    hi = x.astype(BF16)
    lo = (x - hi.astype(F32)).astype(BF16)
    return _dot(hi, m) + _dot(lo, m)


def _seg_rms(x, seg, n):
    ms = _split_dot(x * x, seg) * (1.0 / n)
    return x * lax.rsqrt(ms + EPS)


def _silu(x):
    return x * jax.nn.sigmoid(x)


def _adaln_body(cond_ref, w_ref, b_ref, o_ref):
    c = cond_ref[...]
    sc = _silu(c).astype(BF16)
    o_ref[0] = _dot(sc, w_ref[0].astype(BF16)) + b_ref[0]


def _adaln(cond8, w_ada, b_ada):
    nj = (N_MOD * D_MODEL) // TN_ADA
    return pl.pallas_call(
        _adaln_body,
        out_shape=jax.ShapeDtypeStruct((DEPTH, 8, N_MOD * D_MODEL), F32),
        grid=(DEPTH, nj),
        in_specs=[
            pl.BlockSpec((8, D_MODEL), lambda l, j: (0, 0)),
            pl.BlockSpec((1, D_MODEL, TN_ADA), lambda l, j: (l, 0, j)),
            pl.BlockSpec((1, 1, TN_ADA), lambda l, j: (l, 0, j)),
        ],
        out_specs=pl.BlockSpec((1, 8, TN_ADA), lambda l, j: (l, 0, j)),
        compiler_params=_cparams(("parallel", "parallel")),
        name="adaln",
    )(cond8, w_ada, b_ada.reshape(DEPTH, 1, N_MOD * D_MODEL))


def _mod_row_ffn(i):
    tiles_ctx = N_CTX // TM_FFN
    tiles_per_lat = T_LAT // TM_FFN
    lat = jnp.maximum(i - tiles_ctx, 0) // tiles_per_lat
    return jnp.where(i < tiles_ctx, 0, 1 + lat)


def _ffn_body(x_ref, sh_ref, sc_ref, gt_ref, gn_ref, wg_ref, wu_ref, wd_ref, o_ref, h_scr, acc_scr):
    i = pl.program_id(0)
    j = pl.program_id(1)
    r = _mod_row_ffn(i)

    @pl.when(j == 0)
    def _():
        x = x_ref[...]
        ms = jnp.mean(x * x, axis=-1, keepdims=True)
        y = x * lax.rsqrt(ms + EPS) * gn_ref[...]
        h = y * (1.0 + sc_ref[pl.ds(r, 1), :]) + sh_ref[pl.ds(r, 1), :]
        h_scr[...] = h.astype(BF16)
        acc_scr[...] = jnp.zeros_like(acc_scr)

    h = h_scr[...]
    g = _dot(h, wg_ref[...])
    u = _dot(h, wu_ref[...])
    a = (_silu(g) * u).astype(BF16)
    acc_scr[...] += _dot(a, wd_ref[...])

    @pl.when(j == pl.num_programs(1) - 1)
    def _():
        o_ref[...] = x_ref[...] + 0.5 * gt_ref[pl.ds(r, 1), :] * acc_scr[...]


def _ffn(x, mod_l, k0, gn, wg, wu, wd):
    ni = N_TOK // TM_FFN
    nj = D_FF // TF_FFN
    return pl.pallas_call(
        _ffn_body,
        out_shape=jax.ShapeDtypeStruct((N_TOK, D_MODEL), F32),
        grid=(ni, nj),
        in_specs=[
            pl.BlockSpec((TM_FFN, D_MODEL), lambda i, j: (i, 0)),
            pl.BlockSpec((8, D_MODEL), lambda i, j: (0, k0)),
            pl.BlockSpec((8, D_MODEL), lambda i, j: (0, k0 + 1)),
            pl.BlockSpec((8, D_MODEL), lambda i, j: (0, k0 + 2)),
            pl.BlockSpec((1, D_MODEL), lambda i, j: (0, 0)),
            pl.BlockSpec((D_MODEL, TF_FFN), lambda i, j: (0, j)),
            pl.BlockSpec((D_MODEL, TF_FFN), lambda i, j: (0, j)),
            pl.BlockSpec((TF_FFN, D_MODEL), lambda i, j: (j, 0)),
        ],
        out_specs=pl.BlockSpec((TM_FFN, D_MODEL), lambda i, j: (i, 0)),
        scratch_shapes=[
            pltpu.VMEM((TM_FFN, D_MODEL), BF16),
            pltpu.VMEM((TM_FFN, D_MODEL), F32),
        ],
        compiler_params=_cparams(("parallel", "arbitrary")),
        name="ffn",
    )(x, mod_l, mod_l, mod_l, gn, wg, wu, wd)


def _proj_body(*refs, rope, ctx, tiles_per_batch):
    it = iter(refs)
    x_ref, sh_ref, sc_ref, gn_ref, w_ref, wup_ref, bgla_ref = (next(it) for _ in range(7))
    gaq_ref, gak_ref, gcq_ref, gck_ref = (next(it) for _ in range(4))
    s64a_ref, s64k_ref, s48_ref = (next(it) for _ in range(3))
    if rope:
        cosa_ref, sina_ref, cosc_ref, sinc_ref, ra_ref, rak_ref, rc_ref = (next(it) for _ in range(7))
    qa_o, ka_o, va_o, qc_o, kc_o, vc_o, gq_o, gk_o, gv_o, gla_o, gr_o = (next(it) for _ in range(11))
    if ctx:
        ka32_o, va32_o, kc32_o, vc32_o = (next(it) for _ in range(4))

    i = pl.program_id(0)
    r = 0 if ctx else 1 + i // tiles_per_batch

    x = x_ref[...]
    ms = jnp.mean(x * x, axis=-1, keepdims=True)
    y = x * lax.rsqrt(ms + EPS) * gn_ref[...]
    h = (y * (1.0 + sc_ref[pl.ds(r, 1), :]) + sh_ref[pl.ds(r, 1), :]).astype(BF16)
    p = _dot(h, w_ref[...])

    aq = _seg_rms(p[:, P_AQ:P_AQ + A_W], s64a_ref[...], A_D) * gaq_ref[...]
    ak = _seg_rms(p[:, P_AK:P_AK + 128], s64k_ref[...], A_D) * gak_ref[...]
    av = p[:, P_AV:P_AV + 128]
    cq = _seg_rms(p[:, P_CQ:P_CQ + 384], s48_ref[...], C_DQ) * gcq_ref[...]
    ck = _seg_rms(p[:, P_CK:P_CK + 384], s48_ref[...], C_DQ) * gck_ref[...]
    cv = p[:, P_CV:P_CV + 384]
    if ctx:
        ka32_o[...] = ak
        va32_o[...] = av
        kc32_o[...] = ck
        vc32_o[...] = cv
    if rope:
        cosa = cosa_ref[...]
        sina = sina_ref[...]
        cosa3 = jnp.concatenate([cosa, cosa, cosa], axis=1)
        sina3 = jnp.concatenate([sina, sina, sina], axis=1)
        aq = aq * cosa3 + _split_dot(aq, ra_ref[...]) * sina3
        ak = ak * cosa + _split_dot(ak, rak_ref[...]) * sina
        cosc = cosc_ref[...]
        sinc = sinc_ref[...]
        cq = cq * cosc + _split_dot(cq, rc_ref[...]) * sinc
        ck = ck * cosc + _split_dot(ck, rc_ref[...]) * sinc
    qa_o[...] = (aq * (A_D ** -0.5)).astype(BF16)
    ka_o[...] = ak.astype(BF16)
    va_o[...] = av.astype(BF16)
    qc_o[...] = (cq * (C_DQ ** -0.5)).astype(BF16)
    kc_o[...] = ck.astype(BF16)
    vc_o[...] = cv.astype(BF16)

    gq_o[...] = p[:, P_BQ:P_BQ + B_KW] * (B_DK ** -0.5)
    gk_o[...] = p[:, P_BK:P_BK + B_KW]
    gv_o[...] = p[:, P_BV:P_BV + B_VW]
    gr_o[...] = _silu(p[:, P_BR:P_BR + B_VW])
    z = _dot(p[:, P_BG:P_BG + 128].astype(BF16), wup_ref[...]) + bgla_ref[...]
    log_sig = jnp.minimum(z, 0.0) - jnp.log1p(jnp.exp(-jnp.abs(z)))
    gla_o[...] = log_sig * (1.0 / B_TAU)


def _proj(x, mod_l, gn, w_p, wup, bgla, gains, consts, rope_tabs, *, ctx):
    n = N_CTX if ctx else N_LAT
    off = 0 if ctx else N_CTX // TM_PROJ
    tiles_per_batch = (T_CTX if ctx else T_LAT) // TM_PROJ
    rope = not ctx
    full = lambda shape: pl.BlockSpec(shape, lambda i: (0,) * len(shape))
    in_specs = [
        pl.BlockSpec((TM_PROJ, D_MODEL), lambda i: (i + off, 0)),
        pl.BlockSpec((8, D_MODEL), lambda i: (0, 3)),
        pl.BlockSpec((8, D_MODEL), lambda i: (0, 4)),
        full((1, D_MODEL)),
        full((D_MODEL, P_W)),
        full((128, 2 * B_KW)),
        full((1, 2 * B_KW)),
        full((1, 384)), full((1, 128)), full((1, 384)), full((1, 384)),
        full((384, 384)), full((128, 128)), full((384, 384)),
    ]
    args = [x, mod_l, mod_l, gn, w_p, wup, bgla, *gains, consts["seg64_384"], consts["seg64_128"], consts["seg48_384"]]
    if rope:
        tpb = tiles_per_batch
        in_specs += [
            pl.BlockSpec((TM_PROJ, 128), lambda i: (i % tpb, 0)),
            pl.BlockSpec((TM_PROJ, 128), lambda i: (i % tpb, 0)),
            pl.BlockSpec((TM_PROJ, 384), lambda i: (i % tpb, 0)),
            pl.BlockSpec((TM_PROJ, 384), lambda i: (i % tpb, 0)),
            full((384, 384)), full((128, 128)), full((384, 384)),
        ]
        args += [*rope_tabs, consts["rot_a384"], consts["rot_a128"], consts["rot_c384"]]
    widths = [(384, BF16), (128, BF16), (128, BF16), (384, BF16), (384, BF16), (384, BF16),
              (B_KW, F32), (B_KW, F32), (B_VW, F32), (2 * B_KW, F32), (B_VW, F32)]
    if ctx:
        widths += [(128, F32), (128, F32), (384, F32), (384, F32)]
    out_shape = [jax.ShapeDtypeStruct((n, w), dt) for w, dt in widths]
    out_specs = [pl.BlockSpec((TM_PROJ, w), lambda i: (i, 0)) for w, _ in widths]
    return pl.pallas_call(
        functools.partial(_proj_body, rope=rope, ctx=ctx, tiles_per_batch=tiles_per_batch),
        out_shape=out_shape,
        grid=(n // TM_PROJ,),
        in_specs=in_specs,
        out_specs=out_specs,
        compiler_params=_cparams(("parallel",)),
        name="proj_ctx" if ctx else "proj_lat",
    )(*args)


def _gla_body(gq_ref, gk_ref, gv_ref, gla_ref, gr_ref, s0_ref, gg_ref, seg_ref, ob_ref, sfin_ref,
              of_scr, or_scr, *, seq):
    n = seq // B_CHUNK
    C = B_CHUNK
    ri = lax.broadcasted_iota(jnp.int32, (C, C), 0)
    ci = lax.broadcasted_iota(jnp.int32, (C, C), 1)
    cum = (jnp.where(ci <= ri, 1.0, 0.0).astype(BF16), jnp.where(ci >= ri, 1.0, 0.0).astype(BF16))
    rk = lax.broadcasted_iota(jnp.int32, (B_H * C, B_KW), 0) >> 6
    ck = lax.broadcasted_iota(jnp.int32, (B_H * C, B_KW), 1) >> 5
    hm_k = rk == ck
    rv = lax.broadcasted_iota(jnp.int32, (B_H * C, B_VW), 0) >> 6
    cv = lax.broadcasted_iota(jnp.int32, (B_H * C, B_VW), 1) >> 6
    hm_v = rv == cv
    ra = lax.broadcasted_iota(jnp.int32, (C, B_H * C), 0)
    ca = lax.broadcasted_iota(jnp.int32, (C, B_H * C), 1) & (C - 1)
    tri = (ca <= ra, ca >= ra)

    def expand_state(st):
        return jnp.where(hm_k, jnp.concatenate([st] * B_H, axis=0), 0.0)

    def chunk(d, c, S):
        rows = pl.ds(pl.multiple_of(c * C, C), C)
        q = gq_ref[rows, :]
        k = gk_ref[rows, :]
        v = gv_ref[rows, :].astype(BF16)
        la = gla_ref[rows, d * B_KW:(d + 1) * B_KW]
        b = _split_dot_l(cum[d], la)
        tot = b[C - 1:C, :] if d == 0 else b[0:1, :]
        e = b - b[C // 2:C // 2 + 1, :]
        qt = (q * jnp.exp(e)).astype(BF16)
        kt = (k * jnp.exp(-e)).astype(BF16)
        qd = (q * jnp.exp(b)).astype(BF16)
        kd = (k * jnp.exp(tot - b)).astype(BF16)
        kbd = jnp.where(hm_k, jnp.concatenate([kt] * B_H, axis=0), jnp.zeros((), BF16))
        a = _dot_nt(qt, kbd)
        a = jnp.where(tri[d], a, 0.0).astype(BF16)
        vbd = jnp.where(hm_v, jnp.concatenate([v] * B_H, axis=0), jnp.zeros((), BF16))
        o = _dot(a, vbd) + _dot_nt(qd, S.astype(BF16))
        kv = _dot_tn(v, kd)
        S_new = S * jnp.exp(tot) + jnp.where(hm_k, kv, 0.0)
        return rows, o, S_new

    def step(i, carry):
        Sf, Sb = carry
        rows, o, Sf = chunk(0, i, Sf)
        of_scr[rows, :] = o
        rows, o, Sb = chunk(1, n - 1 - i, Sb)
        or_scr[rows, :] = o
        return Sf, Sb

    Sf0 = expand_state(s0_ref[0, 0])
    Sb0 = expand_state(s0_ref[0, 1])
    Sf, Sb = lax.fori_loop(0, n, step, (Sf0, Sb0))

    def collapse_state(S):
        Sm = jnp.where(hm_k, S, 0.0)
        acc = Sm[0:C, :]
        for hh in range(1, B_H):
            acc = acc + Sm[hh * C:(hh + 1) * C, :]
        return acc

    sfin_ref[0, 0] = collapse_state(Sf)
    sfin_ref[0, 1] = collapse_state(Sb)

    RT = 256

    def fin(t, _):
        rows = pl.ds(pl.multiple_of(t * RT, RT), RT)
        o = of_scr[rows, :] + or_scr[rows, :]
        o = _seg_rms(o, seg_ref[...], B_DV) * gg_ref[...]
        ob_ref[rows, :] = (o * gr_ref[rows, :]).astype(BF16)
        return 0

    lax.fori_loop(0, seq // RT, fin, 0)


def _split_dot_l(m, x):
    hi = x.astype(BF16)
    lo = (x - hi.astype(F32)).astype(BF16)
    return _dot(m, hi) + _dot(m, lo)


def _gla(gq, gk, gv, gla, gr, s0, gg, seg, *, nb, seq):
    tok = lambda w: pl.BlockSpec((seq, w), lambda b: (b, 0))
    return pl.pallas_call(
        functools.partial(_gla_body, seq=seq),
        out_shape=[jax.ShapeDtypeStruct((nb * seq, B_VW), BF16),
                   jax.ShapeDtypeStruct((nb, 2, B_DV, B_KW), F32)],
        grid=(nb,),
        in_specs=[tok(B_KW), tok(B_KW), tok(B_VW), tok(2 * B_KW), tok(B_VW),
                  pl.BlockSpec((1, 2, B_DV, B_KW), lambda b: (b, 0, 0, 0)),
                  pl.BlockSpec((1, B_VW), lambda b: (0, 0)),
                  pl.BlockSpec((B_VW, B_VW), lambda b: (0, 0))],
        out_specs=[tok(B_VW), pl.BlockSpec((1, 2, B_DV, B_KW), lambda b: (b, 0, 0, 0))],
        scratch_shapes=[pltpu.VMEM((seq, B_VW), F32), pltpu.VMEM((seq, B_VW), F32)],
        compiler_params=_cparams(("parallel",)),
        name="gla_ctx" if seq == T_CTX else "gla_lat",
    )(gq, gk, gv, gla, gr, s0, gg, seg)


def _attn_body(x_ref, qa_ref, qc_ref, ob_ref, ka_ref, va_ref, kc_ref, vc_ref, wout_ref, gt_ref,
               lam_ref, gco_ref, seg_ref, o_ref, mix_scr, *, lam_init, ctx):
    b = pl.program_id(0)
    r = 0 if ctx else 1 + b
    tq = x_ref.shape[0]
    zero16 = jnp.zeros((), BF16)

    ka = ka_ref[0]
    va = va_ref[0]
    lane = lax.broadcasted_iota(jnp.int32, (1, 128), 1)
    for g in range(A_G):
        qg = qa_ref[:, g * 128:(g + 1) * 128]
        acc = jnp.zeros((tq, 128), F32)
        for hh in range(A_KV):
            m = (lane >= hh * A_D) & (lane < (hh + 1) * A_D)
            s = _dot_nt(jnp.where(m, qg, zero16), ka)
            p = jnp.exp(s - jnp.max(s, axis=-1, keepdims=True))
            l = jnp.sum(p, axis=-1, keepdims=True)
            o = _dot(p.astype(BF16), va) * (1.0 / l)
            acc = acc + jnp.where(m, o, 0.0)
        mix_scr[:, g * 128:(g + 1) * 128] = acc.astype(BF16)

    mix_scr[:, A_W:A_W + B_VW] = ob_ref[...]

    lm = lam_ref[...]
    lam = (jnp.exp(jnp.sum(lm[0:1] * lm[1:2], axis=-1, keepdims=True))
           - jnp.exp(jnp.sum(lm[2:3] * lm[3:4], axis=-1, keepdims=True)) + lam_init)
    lane2 = lax.broadcasted_iota(jnp.int32, (1, 256), 1)
    outs = []
    for win in range(2):
        base = win * 128
        kc = kc_ref[0, :, base:base + 256]
        vc = vc_ref[0, :, base:base + 256]
        qc = qc_ref[:, base:base + 256]
        ow = jnp.zeros((tq, 256), F32)
        for hh in (2 * win, 2 * win + 1):
            ps = []
            for mm in range(2):
                lo = (hh * 2 + mm) * C_DQ - base
                m = (lane2 >= lo) & (lane2 < lo + C_DQ)
                s = _dot_nt(jnp.where(m, qc, zero16), kc)
                p = jnp.exp(s - jnp.max(s, axis=-1, keepdims=True))
                ps.append(p * (1.0 / jnp.sum(p, axis=-1, keepdims=True)))
            w = (ps[0] - lam * ps[1]).astype(BF16)
            rr = _dot(w, vc)
            vlo = hh * C_DV - base
            vm = (lane2 >= vlo) & (lane2 < vlo + C_DV)
            ow = ow + jnp.where(vm, rr, 0.0)
        outs.append(ow)
    oc = jnp.concatenate([outs[0][:, :128], outs[0][:, 128:] + outs[1][:, :128], outs[1][:, 128:]], axis=1)
    oc = _seg_rms(oc, seg_ref[...], C_DV) * gco_ref[...] * (1.0 - lam_init)
    mix_scr[:, A_W + B_VW:] = oc.astype(BF16)

    mixed = _dot(mix_scr[...], wout_ref[...])
    o_ref[...] = x_ref[...] + gt_ref[pl.ds(r, 1), :] * mixed


def _attn(x, qa, qc, ob, ka, va, kc, vc, wout, mod_l, lam_c, gco, seg96, *, lam_init, ctx):
    nb, seq = (NB_CTX, T_CTX) if ctx else (NB_LAT, T_LAT)
    nq = seq // TQ_ATT
    off = 0 if ctx else N_CTX // TQ_ATT
    tk = ka.shape[1]
    qspec = lambda w: pl.BlockSpec((TQ_ATT, w), lambda b, q: (b * nq + q, 0))
    kspec = lambda w: pl.BlockSpec((1, tk, w), lambda b, q: (b, 0, 0))
    full = lambda shape: pl.BlockSpec(shape, lambda b, q: (0,) * len(shape))
    xspec = pl.BlockSpec((TQ_ATT, D_MODEL), lambda b, q: (off + b * nq + q, 0))
    return pl.pallas_call(
        functools.partial(_attn_body, lam_init=lam_init, ctx=ctx),
        out_shape=jax.ShapeDtypeStruct((N_TOK, D_MODEL), F32),
        grid=(nb, nq),
        in_specs=[xspec, qspec(384), qspec(384), qspec(B_VW),
                  kspec(128), kspec(128), kspec(384), kspec(384),
                  full((D_MODEL, D_MODEL)),
                  pl.BlockSpec((8, D_MODEL), lambda b, q: (0, 5)),
                  full((4, C_DQ)), full((1, 384)), full((384, 384))],
        out_specs=xspec,
        scratch_shapes=[pltpu.VMEM((TQ_ATT, D_MODEL), BF16)],
        input_output_aliases={0: 0},
        compiler_params=_cparams(("parallel", "arbitrary")),
        name="attn_ctx" if ctx else "attn_lat",
    )(x, qa, qc, ob, ka, va, kc, vc, wout, mod_l, lam_c, gco, seg96)


def _block_ones(width, seg):
    idx = np.arange(width) // seg
    return jnp.asarray((idx[:, None] == idx[None, :]).astype(np.float32), dtype=BF16)


def _rot_matrix(width, half):
    m = np.zeros((width, width), np.float32)
    for j in range(width):
        if (j % (2 * half)) < half:
            m[j + half, j] = -1.0
        else:
            m[j - half, j] = 1.0
    return jnp.asarray(m, dtype=BF16)


def _rope_tables():
    t = np.arange(T_LAT)
    row = (t // GRID_W).astype(np.float32)
    col = (t % GRID_W).astype(np.float32)

    def tab(head_dim, n_heads):
        m = head_dim // 4
        freqs = ROPE_BASE ** (-jnp.arange(m, dtype=F32) / m)
        ang_r = jnp.asarray(row)[:, None] * freqs[None, :]
        ang_c = jnp.asarray(col)[:, None] * freqs[None, :]
        cs = jnp.concatenate([jnp.cos(ang_r)] * 2 + [jnp.cos(ang_c)] * 2, axis=1)
        sn = jnp.concatenate([jnp.sin(ang_r)] * 2 + [jnp.sin(ang_c)] * 2, axis=1)
        return jnp.tile(cs, (1, n_heads)), jnp.tile(sn, (1, n_heads))

    cosa, sina = tab(A_D, 2)
    cosc, sinc = tab(C_DQ, 2 * C_H)
    return cosa, sina, cosc, sinc


def kernel(x_prompt, x_sample, c, cache_a_k, cache_a_v, cache_c_k, cache_c_v, state_gla, c_ctx, w_ada, b_ada,
           g_norm, w_ffn_gate, w_ffn_up, w_ffn_down, w_in, g_a_q, g_a_k, w_gla_up, b_gla, g_gla, g_c_q, g_c_k,
           lam_c, g_c_out, w_out):
    lam_inits = [0.8 - 0.6 * math.exp(-0.3 * l) for l in range(DEPTH)]

    consts = {
        "seg64_384": _block_ones(384, 64), "seg64_128": _block_ones(128, 64),
        "seg48_384": _block_ones(384, 48), "seg96_384": _block_ones(384, 96),
        "seg64_256": _block_ones(256, 64),
        "rot_a384": _rot_matrix(384, 16), "rot_a128": _rot_matrix(128, 16), "rot_c384": _rot_matrix(384, 12),
    }
    rope_tabs = _rope_tables()

    wq = w_in[:, :, 0:384].reshape(DEPTH, D_MODEL, A_KV, A_G, A_D).transpose(0, 1, 3, 2, 4).reshape(DEPTH, D_MODEL, 384)
    seg = lambda a, b: w_in[:, :, a:b]
    w_p = jnp.concatenate([
        wq, seg(384, 512), seg(512, 640),
        seg(640, 768), seg(768, 896), seg(896, 1152),
        seg(1184, 1440),
        seg(1440, 1824), seg(1824, 2208), seg(2208, 2592),
        seg(1152, 1184), jnp.zeros((DEPTH, D_MODEL, 96), F32),
    ], axis=2).astype(BF16)
    wo_a = w_out[:, 0:384].reshape(DEPTH, A_KV, A_G, A_D, D_MODEL).transpose(0, 2, 1, 3, 4).reshape(DEPTH, 384, D_MODEL)
    w_o = jnp.concatenate([wo_a, w_out[:, 384:]], axis=1).astype(BF16)
    wup = jnp.zeros((DEPTH, 128, 2 * B_KW), F32)
    wup = wup.at[:, 0:B_RANK, 0:B_KW].set(w_gla_up[:, 0]).at[:, B_RANK:2 * B_RANK, B_KW:].set(w_gla_up[:, 1])
    wup = wup.astype(BF16)
    bgla = b_gla.reshape(DEPTH, 1, 2 * B_KW)
    wg16 = w_ffn_gate.astype(BF16)
    wu16 = w_ffn_up.astype(BF16)
    wd16 = w_ffn_down.astype(BF16)
    gaq = jnp.tile(g_a_q, (1, 6)).reshape(DEPTH, 1, 384)
    gak = jnp.tile(g_a_k, (1, 2)).reshape(DEPTH, 1, 128)
    gcq = jnp.tile(g_c_q.reshape(DEPTH, 96), (1, 4)).reshape(DEPTH, 1, 384)
    gck = jnp.tile(g_c_k.reshape(DEPTH, 96), (1, 4)).reshape(DEPTH, 1, 384)
    gco = jnp.tile(g_c_out, (1, 4)).reshape(DEPTH, 1, 384)
    ggl = jnp.tile(g_gla, (1, 4)).reshape(DEPTH, 1, 256)

    cond8 = jnp.zeros((8, D_MODEL), F32).at[0].set(c_ctx).at[1:3].set(c)
    mod = _adaln(cond8, w_ada, b_ada)

    x = jnp.concatenate([x_prompt.reshape(N_CTX, D_MODEL), x_sample.reshape(N_LAT, D_MODEL)], axis=0)
    s0_ctx = jnp.zeros((NB_CTX, 2, B_DV, B_KW), F32)
    ak_l, av_l, ck_l, cv_l, sg_l = [], [], [], [], []
    for l in range(DEPTH):
        mod_l = mod[l]
        gn = g_norm[l].reshape(3, 1, D_MODEL)
        x = _ffn(x, mod_l, 0, gn[0], wg16[l, 0], wu16[l, 0], wd16[l, 0])
        gains = (gaq[l], gak[l], gcq[l], gck[l])

        (qa, ka, va, qc, kc, vc, gq, gk, gv, gla, gr, ka32, va32, kc32, vc32) = _proj(
            x, mod_l, gn[1], w_p[l], wup[l], bgla[l], gains, consts, None, ctx=True)
        ob, sfin = _gla(gq, gk, gv, gla, gr, s0_ctx, ggl[l], consts["seg64_256"], nb=NB_CTX, seq=T_CTX)
        r3 = lambda a: a.reshape(NB_CTX, T_CTX, a.shape[-1])
        x = _attn(x, qa, qc, ob, r3(ka), r3(va), r3(kc), r3(vc), w_o[l], mod_l, lam_c[l], gco[l],
                  consts["seg96_384"], lam_init=lam_inits[l], ctx=True)
        ak_l.append(ka32.reshape(NB_CTX, T_CTX, A_KV, A_D))
        av_l.append(va32.reshape(NB_CTX, T_CTX, A_KV, A_D))
        ck_l.append(kc32.reshape(NB_CTX, T_CTX, C_H, 2 * C_DQ))
        cv_l.append(vc32.reshape(NB_CTX, T_CTX, C_H, C_DV))
        sg_l.append(jnp.swapaxes(sfin, 2, 3).reshape(NB_CTX, 2, B_H, B_DK, B_DV))

        (qa, ka, va, qc, kc, vc, gq, gk, gv, gla, gr) = _proj(
            x, mod_l, gn[1], w_p[l], wup[l], bgla[l], gains, consts, rope_tabs, ctx=False)
        s0 = jnp.swapaxes(state_gla[:, l].reshape(NB_LAT, 2, B_KW, B_DV), 2, 3)
        ob, _ = _gla(gq, gk, gv, gla, gr, s0, ggl[l], consts["seg64_256"], nb=NB_LAT, seq=T_LAT)
        cat = lambda new, old, w: jnp.concatenate(
            [new.reshape(NB_LAT, T_LAT, w), old[:, l].reshape(NB_LAT, PAST_LEN, w).astype(BF16)], axis=1)
        x = _attn(x, qa, qc, ob, cat(ka, cache_a_k, 128), cat(va, cache_a_v, 128), cat(kc, cache_c_k, 384),
                  cat(vc, cache_c_v, 384), w_o[l], mod_l, lam_c[l], gco[l], consts["seg96_384"],
                  lam_init=lam_inits[l], ctx=False)

        x = _ffn(x, mod_l, 6, gn[2], wg16[l, 1], wu16[l, 1], wd16[l, 1])

    y_prompt = x[:N_CTX].reshape(NB_CTX, T_CTX, D_MODEL)
    y_sample = x[N_CTX:].reshape(NB_LAT, T_LAT, D_MODEL)
    return (y_prompt, y_sample, jnp.stack(ak_l, axis=1), jnp.stack(av_l, axis=1), jnp.stack(ck_l, axis=1),
            jnp.stack(cv_l, axis=1), jnp.stack(sg_l, axis=1))
```

```python
import functools
import math

import numpy as np
import jax
import jax.numpy as jnp
from jax import lax
from jax.experimental import pallas as pl
from jax.experimental.pallas import tpu as pltpu

F32 = jnp.float32
BF16 = jnp.bfloat16

D_MODEL = 1024
D_FF = 2816
DEPTH = 4
N_MOD = 9
EPS = 1e-6
ROPE_BASE = 10000.0
GRID_W = 64
LOG2E = math.log2(math.e)

NB_CTX, T_CTX = 32, 256
NB_LAT, T_LAT = 2, 2048
PAST_LEN = 512
N_CTX = NB_CTX * T_CTX
N_LAT = NB_LAT * T_LAT
N_TOK = N_CTX + N_LAT

A_KV, A_G, A_D = 2, 3, 64
A_W = A_KV * A_G * A_D
B_H, B_DK, B_DV = 4, 32, 64
B_KW = B_H * B_DK
B_VW = B_H * B_DV
B_RANK = 16
B_TAU = 16.0
B_CHUNK = 64
C_H, C_DQ, C_DV = 4, 48, 96
C_W = C_H * C_DV

P_AQ, P_AK, P_AV = 0, 384, 512
P_BQ, P_BK, P_BV, P_BR = 640, 768, 896, 1152
P_CQ, P_CK, P_CV = 1408, 1792, 2176
P_BG = 2560
P_W = 2688

TM_FFN = 1024
TF_FFN = 256
TM_PROJ = 256
TQ_ATT = 256
TN_ADA = 1152
VMEM_LIMIT = 56 * 1024 * 1024


def _cparams(sem):
    return pltpu.CompilerParams(dimension_semantics=sem, vmem_limit_bytes=VMEM_LIMIT)


def _dot(a, b):
    return jnp.dot(a, b, preferred_element_type=F32)


def _dot_nt(a, b):
    return lax.dot_general(a, b, (((1,), (1,)), ((), ())), preferred_element_type=F32)


def _dot_tn(a, b):
    return lax.dot_general(a, b, (((0,), (0,)), ((), ())), preferred_element_type=F32)


def _split_dot(x, m):
    hi = x.astype(BF16)
    lo = (x - hi.astype(F32)).astype(BF16)
    return _dot(hi, m) + _dot(lo, m)


def _split_dot_l(m, x):
    hi = x.astype(BF16)
    lo = (x - hi.astype(F32)).astype(BF16)
    return _dot(m, hi) + _dot(m, lo)


def _seg_rms(x, seg, n):
    ms = _dot((x * x).astype(BF16), seg) * (1.0 / n)
    return x * lax.rsqrt(ms + EPS)


def _silu(x):
    return x * jax.nn.sigmoid(x)


def _norm_mod(x, gn, sc, sh):
    ms = jnp.mean(x * x, axis=-1, keepdims=True)
    return ((x * lax.rsqrt(ms + EPS)) * gn * (1.0 + sc) + sh).astype(BF16)


def _adaln_body(cond_ref, w_ref, b_ref, o_ref):
    c = cond_ref[...]
    sc = _silu(c).astype(BF16)
    o_ref[0] = _dot(sc, w_ref[0].astype(BF16)) + b_ref[0]


def _adaln(cond8, w_ada, b_ada):
    nj = (N_MOD * D_MODEL) // TN_ADA
    return pl.pallas_call(
        _adaln_body,
        out_shape=jax.ShapeDtypeStruct((DEPTH, 8, N_MOD * D_MODEL), F32),
        grid=(DEPTH, nj),
        in_specs=[
            pl.BlockSpec((8, D_MODEL), lambda l, j: (0, 0)),
            pl.BlockSpec((1, D_MODEL, TN_ADA), lambda l, j: (l, 0, j)),
            pl.BlockSpec((1, 1, TN_ADA), lambda l, j: (l, 0, j)),
        ],
        out_specs=pl.BlockSpec((1, 8, TN_ADA), lambda l, j: (l, 0, j)),
        compiler_params=_cparams(("parallel", "parallel")),
        name="adaln",
    )(cond8, w_ada, b_ada.reshape(DEPTH, 1, N_MOD * D_MODEL))


FFN_TILES_CTX = N_CTX // TM_FFN
FFN_TILES_PER_LAT = T_LAT // TM_FFN


def _ffn_body(*refs, first, last):
    it = iter(refs)
    x_refs = (next(it), next(it)) if first else (next(it),)
    sh_ref, sc_ref, gt_ref, gn_ref, wg_ref, wu_ref, wd_ref = (next(it) for _ in range(7))
    o_refs = (next(it), next(it)) if last else (next(it),)
    h_scr, acc_scr = next(it), next(it)

    i = pl.program_id(0)
    j = pl.program_id(1)
    is_ctx = i < FFN_TILES_CTX
    r = jnp.where(is_ctx, 0, 1 + jnp.maximum(i - FFN_TILES_CTX, 0) // FFN_TILES_PER_LAT)

    def on_tile(pred, n_variants, fn):
        if n_variants == 1:
            pl.when(pred)(lambda: fn(0))
        else:
            pl.when(pred & is_ctx)(lambda: fn(0))
            pl.when(pred & jnp.logical_not(is_ctx))(lambda: fn(1))

    def prologue(k):
        h_scr[...] = _norm_mod(x_refs[k][...], gn_ref[...], sc_ref[pl.ds(r, 1), :], sh_ref[pl.ds(r, 1), :])
        acc_scr[...] = jnp.zeros_like(acc_scr)

    on_tile(j == 0, len(x_refs), prologue)

    h = h_scr[...]
    g = _dot(h, wg_ref[...].astype(BF16))
    u = _dot(h, wu_ref[...].astype(BF16))
    a = (_silu(g) * u).astype(BF16)
    acc_scr[...] += _dot(a, wd_ref[...].astype(BF16))

    def epilogue(k):
        x_ref = x_refs[k if first else 0]
        o_ref = o_refs[k if last else 0]
        o_ref[...] = x_ref[...] + 0.5 * gt_ref[pl.ds(r, 1), :] * acc_scr[...]

    on_tile(j == pl.num_programs(1) - 1, max(len(x_refs), len(o_refs)), epilogue)


def _ffn(xs, mod, gn4, wg, wu, wd, *, l, s, first=False, last=False):
    ni = N_TOK // TM_FFN
    nj = D_FF // TF_FFN
    k0 = 6 * s
    gi = 2 * s
    tc = FFN_TILES_CTX
    split_specs = [pl.BlockSpec((TM_FFN, D_MODEL), lambda i, j: (jnp.minimum(i, tc - 1), 0)),
                   pl.BlockSpec((TM_FFN, D_MODEL), lambda i, j: (jnp.maximum(i - tc, 0), 0))]
    one_spec = [pl.BlockSpec((TM_FFN, D_MODEL), lambda i, j: (i, 0))]
    mspec = lambda k: pl.BlockSpec((None, 8, D_MODEL), lambda i, j: (l, 0, k))
    in_specs = (split_specs if first else one_spec) + [
        mspec(k0), mspec(k0 + 1), mspec(k0 + 2),
        pl.BlockSpec((None, None, 1, D_MODEL), lambda i, j: (l, gi, 0, 0)),
        pl.BlockSpec((None, None, D_MODEL, TF_FFN), lambda i, j: (l, s, 0, j)),
        pl.BlockSpec((None, None, D_MODEL, TF_FFN), lambda i, j: (l, s, 0, j)),
        pl.BlockSpec((None, None, TF_FFN, D_MODEL), lambda i, j: (l, s, j, 0)),
    ]
    if last:
        out_shape = [jax.ShapeDtypeStruct((N_CTX, D_MODEL), F32), jax.ShapeDtypeStruct((N_LAT, D_MODEL), F32)]
        out_specs = split_specs
    else:
        out_shape = jax.ShapeDtypeStruct((N_TOK, D_MODEL), F32)
        out_specs = one_spec[0]
    return pl.pallas_call(
        functools.partial(_ffn_body, first=first, last=last),
        out_shape=out_shape,
        grid=(ni, nj),
        in_specs=in_specs,
        out_specs=out_specs,
        scratch_shapes=[
            pltpu.VMEM((TM_FFN, D_MODEL), BF16),
            pltpu.VMEM((TM_FFN, D_MODEL), F32),
        ],
        compiler_params=_cparams(("arbitrary", "arbitrary")),
        name="ffn",
    )(*xs, mod, mod, mod, gn4, wg, wu, wd)


def _proj_body(*refs, rope, ctx, tiles_per_batch):
    it = iter(refs)
    x_ref, sh_ref, sc_ref, gn_ref, w_ref, wup_ref, bgla_ref = (next(it) for _ in range(7))
    gaq_ref, gak_ref, gcq_ref, gck_ref = (next(it) for _ in range(4))
    s64a_ref, s64k_ref, s48_ref = (next(it) for _ in range(3))
    if rope:
        cosa_ref, sina_ref, cosc_ref, sinc_ref, ra_ref, rak_ref, rc_ref = (next(it) for _ in range(7))
    if ctx:
        for _ in range(4):
            next(it)
    qa_o, ka_o, va_o, qc_o, kc_o, vc_o, gq_o, gk_o, gv_o, gla_o, gr_o = (next(it) for _ in range(11))
    if ctx:
        ka32_o, va32_o, kc32_o, vc32_o = (next(it) for _ in range(4))

    i = pl.program_id(0)
    r = 0 if ctx else 1 + i // tiles_per_batch

    h = _norm_mod(x_ref[...], gn_ref[...], sc_ref[pl.ds(r, 1), :], sh_ref[pl.ds(r, 1), :])
    p = _dot(h, w_ref[...])

    aq = _seg_rms(p[:, P_AQ:P_AQ + A_W], s64a_ref[...], A_D) * gaq_ref[...]
    ak = _seg_rms(p[:, P_AK:P_AK + 128], s64k_ref[...], A_D) * gak_ref[...]
    av = p[:, P_AV:P_AV + 128]
    cq = _seg_rms(p[:, P_CQ:P_CQ + 384], s48_ref[...], C_DQ) * gcq_ref[...]
    ck = _seg_rms(p[:, P_CK:P_CK + 384], s48_ref[...], C_DQ) * gck_ref[...]
    cv = p[:, P_CV:P_CV + 384]
    if ctx:
        ka32_o[...] = ak
        va32_o[...] = av
        kc32_o[...] = ck
        vc32_o[...] = cv
    if rope:
        cosa = cosa_ref[...]
        sina = sina_ref[...]
        cosa3 = jnp.concatenate([cosa, cosa, cosa], axis=1)
        sina3 = jnp.concatenate([sina, sina, sina], axis=1)
        aq = aq * cosa3 + _dot(aq.astype(BF16), ra_ref[...]) * sina3
        ak = ak * cosa + _dot(ak.astype(BF16), rak_ref[...]) * sina
        cosc = cosc_ref[...]
        sinc = sinc_ref[...]
        cq = cq * cosc + _dot(cq.astype(BF16), rc_ref[...]) * sinc
        ck = ck * cosc + _dot(ck.astype(BF16), rc_ref[...]) * sinc
    qa_o[...] = (aq * (A_D ** -0.5 * LOG2E)).astype(BF16)
    ka_o[...] = ak.astype(BF16)
    va_o[...] = av.astype(BF16)
    qc_o[...] = (cq * (C_DQ ** -0.5 * LOG2E)).astype(BF16)
    kc_o[...] = ck.astype(BF16)
    vc_o[...] = cv.astype(BF16)

    gq_o[...] = p[:, P_BQ:P_BQ + B_KW] * (B_DK ** -0.5)
    gk_o[...] = p[:, P_BK:P_BK + B_KW]
    gv_o[...] = p[:, P_BV:P_BV + B_VW]
    gr_o[...] = _silu(p[:, P_BR:P_BR + B_VW])
    z = _dot(p[:, P_BG:P_BG + 128].astype(BF16), wup_ref[...]) + bgla_ref[...]
    log_sig = jnp.minimum(z, 0.0) - jnp.log1p(jnp.exp(-jnp.abs(z)))
    gla_o[...] = log_sig * (1.0 / B_TAU)


def _proj(x, prm, consts, rope_tabs, caches, *, l, ctx):
    n = N_CTX if ctx else N_LAT
    off = 0 if ctx else N_CTX // TM_PROJ
    tiles_per_batch = (T_CTX if ctx else T_LAT) // TM_PROJ
    rope = not ctx
    full = lambda shape: pl.BlockSpec(shape, lambda i: (0,) * len(shape))
    lay = lambda shape: pl.BlockSpec((None,) + shape, lambda i: (l,) + (0,) * len(shape))
    in_specs = [
        pl.BlockSpec((TM_PROJ, D_MODEL), lambda i: (i + off, 0)),
        pl.BlockSpec((None, 8, D_MODEL), lambda i: (l, 0, 3)),
        pl.BlockSpec((None, 8, D_MODEL), lambda i: (l, 0, 4)),
        pl.BlockSpec((None, None, 1, D_MODEL), lambda i: (l, 1, 0, 0)),
        lay((D_MODEL, P_W)),
        lay((128, 2 * B_KW)),
        lay((1, 2 * B_KW)),
        lay((1, 384)), lay((1, 128)), lay((1, 384)), lay((1, 384)),
        full((384, 384)), full((128, 128)), full((384, 384)),
    ]
    args = [x, prm["mod"], prm["mod"], prm["gn4"], prm["w_p"], prm["wup"], prm["bgla"],
            prm["gaq"], prm["gak"], prm["gcq"], prm["gck"],
            consts["seg64_384"], consts["seg64_128"], consts["seg48_384"]]
    if rope:
        tpb = tiles_per_batch
        in_specs += [
            pl.BlockSpec((TM_PROJ, 128), lambda i: (i % tpb, 0)),
            pl.BlockSpec((TM_PROJ, 128), lambda i: (i % tpb, 0)),
            pl.BlockSpec((TM_PROJ, 384), lambda i: (i % tpb, 0)),
            pl.BlockSpec((TM_PROJ, 384), lambda i: (i % tpb, 0)),
            full((384, 384)), full((128, 128)), full((384, 384)),
        ]
        args += [*rope_tabs, consts["rot_a384"], consts["rot_a128"], consts["rot_c384"]]
    widths = [(384, BF16), (128, BF16), (128, BF16), (384, BF16), (384, BF16), (384, BF16),
              (B_KW, F32), (B_KW, F32), (B_VW, F32), (2 * B_KW, F32), (B_VW, F32)]
    out_shape = [jax.ShapeDtypeStruct((n, w), dt) for w, dt in widths]
    out_specs = [pl.BlockSpec((TM_PROJ, w), lambda i: (i, 0)) for w, _ in widths]
    aliases = {}
    if ctx:
        assert TM_PROJ == T_CTX
        for k, cache in enumerate(caches):
            aliases[len(args)] = len(out_shape)
            in_specs.append(pl.BlockSpec(memory_space=pl.ANY))
            args.append(cache)
            out_shape.append(jax.ShapeDtypeStruct(cache.shape, F32))
            out_specs.append(pl.BlockSpec((None, None, T_CTX, cache.shape[-1]), lambda i: (i, l, 0, 0)))
    return pl.pallas_call(
        functools.partial(_proj_body, rope=rope, ctx=ctx, tiles_per_batch=tiles_per_batch),
        out_shape=out_shape,
        grid=(n // TM_PROJ,),
        in_specs=in_specs,
        out_specs=out_specs,
        input_output_aliases=aliases,
        compiler_params=_cparams(("parallel",)),
        name="proj_ctx" if ctx else "proj_lat",
    )(*args)


def _gla_body(*refs, seq, ctx):
    it = iter(refs)
    gq_ref, gk_ref, gv_ref, gla_ref, gr_ref, s0_ref, gg_ref, seg_ref = (next(it) for _ in range(8))
    if ctx:
        next(it)
    ob_ref, sfin_ref, of_scr, or_scr = (next(it) for _ in range(4))

    n = seq // B_CHUNK
    C = B_CHUNK
    ri = lax.broadcasted_iota(jnp.int32, (C, C), 0)
    ci = lax.broadcasted_iota(jnp.int32, (C, C), 1)
    cum = (jnp.where(ci <= ri, 1.0, 0.0).astype(BF16), jnp.where(ci >= ri, 1.0, 0.0).astype(BF16))
    rk = lax.broadcasted_iota(jnp.int32, (B_H * C, B_KW), 0) >> 6
    ck = lax.broadcasted_iota(jnp.int32, (B_H * C, B_KW), 1) >> 5
    hm_k = rk == ck
    rv = lax.broadcasted_iota(jnp.int32, (B_H * C, B_VW), 0) >> 6
    cv = lax.broadcasted_iota(jnp.int32, (B_H * C, B_VW), 1) >> 6
    hm_v = rv == cv
    ra = lax.broadcasted_iota(jnp.int32, (C, B_H * C), 0)
    ca = lax.broadcasted_iota(jnp.int32, (C, B_H * C), 1) & (C - 1)
    tri = (ca <= ra, ca >= ra)
    zero16 = jnp.zeros((), BF16)

    def expand_state(st):
        return jnp.where(hm_k, jnp.concatenate([st] * B_H, axis=0), 0.0)

    def chunk(d, c, S):
        start = c * C
        rows = pl.ds(start if isinstance(start, int) else pl.multiple_of(start, C), C)
        q = gq_ref[rows, :]
        k = gk_ref[rows, :]
        v = gv_ref[rows, :].astype(BF16)
        la = gla_ref[rows, d * B_KW:(d + 1) * B_KW]
        b = _split_dot_l(cum[d], la)
        tot = b[C - 1:C, :] if d == 0 else b[0:1, :]
        e = b - b[C // 2:C // 2 + 1, :]
        qt = (q * jnp.exp(e)).astype(BF16)
        kt = (k * jnp.exp(-e)).astype(BF16)
        qd = (q * jnp.exp(b)).astype(BF16)
        kd = (k * jnp.exp(tot - b)).astype(BF16)
        kbd = jnp.where(hm_k, jnp.concatenate([kt] * B_H, axis=0), zero16)
        a = _dot_nt(qt, kbd)
        a = jnp.where(tri[d], a, 0.0).astype(BF16)
        vbd = jnp.where(hm_v, jnp.concatenate([v] * B_H, axis=0), zero16)
        o = _dot(a, vbd) + _dot_nt(qd, S.astype(BF16))
        kv = _dot_tn(v, kd)
        S_new = S * jnp.exp(tot) + jnp.where(hm_k, kv, 0.0)
        return rows, o, S_new

    def step(i, carry):
        Sf, Sb = carry
        rows, o, Sf = chunk(0, i, Sf)
        of_scr[rows, :] = o
        rows, o, Sb = chunk(1, n - 1 - i, Sb)
        or_scr[rows, :] = o
        return Sf, Sb

    carry = (expand_state(s0_ref[0, 0]), expand_state(s0_ref[0, 1]))
    if n <= 4:
        for i in range(n):
            carry = step(i, carry)
    else:
        carry = lax.fori_loop(0, n, step, carry, unroll=2)
    Sf, Sb = carry

    def collapse_state(S):
        Sm = jnp.where(hm_k, S, 0.0)
        acc = Sm[0:C, :]
        for hh in range(1, B_H):
            acc = acc + Sm[hh * C:(hh + 1) * C, :]
        return acc

    sfin_ref[0] = collapse_state(Sf)
    sfin_ref[1] = collapse_state(Sb)

    RT = 256

    def fin(t, _):
        rows = pl.ds(pl.multiple_of(t * RT, RT), RT)
        o = of_scr[rows, :] + or_scr[rows, :]
        ms = _split_dot(o * o, seg_ref[...]) * (1.0 / B_DV)
        o = o * lax.rsqrt(ms + EPS) * gg_ref[...]
        ob_ref[rows, :] = (o * gr_ref[rows, :]).astype(BF16)
        return 0

    lax.fori_loop(0, seq // RT, fin, 0)


def _gla(gq, gk, gv, gla, gr, s0, prm, seg, state_out, *, l, ctx):
    nb, seq = (NB_CTX, T_CTX) if ctx else (NB_LAT, T_LAT)
    tok = lambda w: pl.BlockSpec((seq, w), lambda b: (b, 0))
    in_specs = [tok(B_KW), tok(B_KW), tok(B_VW), tok(2 * B_KW), tok(B_VW),
                pl.BlockSpec((1, 2, B_DV, B_KW), lambda b: (b, 0, 0, 0)),
                pl.BlockSpec((None, 1, B_VW), lambda b: (l, 0, 0)),
                pl.BlockSpec((B_VW, B_VW), lambda b: (0, 0))]
    args = [gq, gk, gv, gla, gr, s0, prm["ggl"], seg]
    aliases = {}
    if ctx:
        aliases[len(args)] = 1
        in_specs.append(pl.BlockSpec(memory_space=pl.ANY))
        args.append(state_out)
        st_shape = jax.ShapeDtypeStruct(state_out.shape, F32)
        st_spec = pl.BlockSpec((None, None, 2, B_DV, B_KW), lambda b: (b, l, 0, 0, 0))
    else:
        st_shape = jax.ShapeDtypeStruct((nb, 2, B_DV, B_KW), F32)
        st_spec = pl.BlockSpec((None, 2, B_DV, B_KW), lambda b: (b, 0, 0, 0))
    return pl.pallas_call(
        functools.partial(_gla_body, seq=seq, ctx=ctx),
        out_shape=[jax.ShapeDtypeStruct((nb * seq, B_VW), BF16), st_shape],
        grid=(nb,),
        in_specs=in_specs,
        out_specs=[tok(B_VW), st_spec],
        scratch_shapes=[pltpu.VMEM((seq, B_VW), F32), pltpu.VMEM((seq, B_VW), F32)],
        input_output_aliases=aliases,
        compiler_params=_cparams(("parallel",)),
        name="gla_ctx" if ctx else "gla_lat",
    )(*args)


def _attn_body(x_ref, qa_ref, qc_ref, ob_ref, ka_ref, va_ref, kc_ref, vc_ref, wout_ref, gt_ref,
               lam_ref, gco_ref, seg_ref, o_ref, mix_scr, *, lam_init, ctx):
    b = pl.program_id(0)
    r = 0 if ctx else 1 + b
    tq = x_ref.shape[0]
    zero16 = jnp.zeros((), BF16)

    ka = ka_ref[0]
    va = va_ref[0]
    lane = lax.broadcasted_iota(jnp.int32, (1, 128), 1)
    for g in range(A_G):
        qg = qa_ref[:, g * 128:(g + 1) * 128]
        acc = jnp.zeros((tq, 128), F32)
        for hh in range(A_KV):
            m = (lane >= hh * A_D) & (lane < (hh + 1) * A_D)
            s = _dot_nt(jnp.where(m, qg, zero16), ka)
            p = jnp.exp2(s - jnp.max(s, axis=-1, keepdims=True))
            l = jnp.sum(p, axis=-1, keepdims=True)
            o = _dot(p.astype(BF16), va) * (1.0 / l)
            acc = acc + jnp.where(m, o, 0.0)
        mix_scr[:, g * 128:(g + 1) * 128] = acc.astype(BF16)

    mix_scr[:, A_W:A_W + B_VW] = ob_ref[...]

    lm = lam_ref[...]
    lam = (jnp.exp(jnp.sum(lm[0:1] * lm[1:2], axis=-1, keepdims=True))
           - jnp.exp(jnp.sum(lm[2:3] * lm[3:4], axis=-1, keepdims=True)) + lam_init)
    lane2 = lax.broadcasted_iota(jnp.int32, (1, 256), 1)
    outs = []
    for win in range(2):
        base = win * 128
        kc = kc_ref[0, :, base:base + 256]
        vc = vc_ref[0, :, base:base + 256]
        qc = qc_ref[:, base:base + 256]
        ow = jnp.zeros((tq, 256), F32)
        for hh in (2 * win, 2 * win + 1):
            es, ls = [], []
            for mm in range(2):
                lo = (hh * 2 + mm) * C_DQ - base
                m = (lane2 >= lo) & (lane2 < lo + C_DQ)
                s = _dot_nt(jnp.where(m, qc, zero16), kc)
                e = jnp.exp2(s - jnp.max(s, axis=-1, keepdims=True))
                es.append(e)
                ls.append(jnp.sum(e, axis=-1, keepdims=True))
            inv0 = 1.0 / ls[0]
            w = (es[0] - (lam * ls[0] / ls[1]) * es[1]).astype(BF16)
            rr = _dot(w, vc) * inv0
            vlo = hh * C_DV - base
            vm = (lane2 >= vlo) & (lane2 < vlo + C_DV)
            ow = ow + jnp.where(vm, rr, 0.0)
        outs.append(ow)
    oc = jnp.concatenate([outs[0][:, :128], outs[0][:, 128:] + outs[1][:, :128], outs[1][:, 128:]], axis=1)
    ms = _split_dot(oc * oc, seg_ref[...]) * (1.0 / C_DV)
    oc = oc * lax.rsqrt(ms + EPS) * gco_ref[...] * (1.0 - lam_init)
    mix_scr[:, A_W + B_VW:] = oc.astype(BF16)

    mixed = _dot(mix_scr[...], wout_ref[...])
    o_ref[...] = x_ref[...] + gt_ref[pl.ds(r, 1), :] * mixed


def _attn(x, qa, qc, ob, ka, va, kc, vc, prm, seg96, *, l, lam_init, ctx):
    nb, seq = (NB_CTX, T_CTX) if ctx else (NB_LAT, T_LAT)
    nq = seq // TQ_ATT
    off = 0 if ctx else N_CTX // TQ_ATT
    tk = ka.shape[1]
    qspec = lambda w: pl.BlockSpec((TQ_ATT, w), lambda b, q: (b * nq + q, 0))
    kspec = lambda w: pl.BlockSpec((1, tk, w), lambda b, q: (b, 0, 0))
    lay = lambda shape: pl.BlockSpec((None,) + shape, lambda b, q: (l,) + (0,) * len(shape))
    xspec = pl.BlockSpec((TQ_ATT, D_MODEL), lambda b, q: (off + b * nq + q, 0))
    return pl.pallas_call(
        functools.partial(_attn_body, lam_init=lam_init, ctx=ctx),
        out_shape=jax.ShapeDtypeStruct((N_TOK, D_MODEL), F32),
        grid=(nb, nq),
        in_specs=[xspec, qspec(384), qspec(384), qspec(B_VW),
                  kspec(128), kspec(128), kspec(384), kspec(384),
                  lay((D_MODEL, D_MODEL)),
                  pl.BlockSpec((None, 8, D_MODEL), lambda b, q: (l, 0, 5)),
                  lay((4, C_DQ)), lay((1, 384)),
                  pl.BlockSpec((384, 384), lambda b, q: (0, 0))],
        out_specs=xspec,
        scratch_shapes=[pltpu.VMEM((TQ_ATT, D_MODEL), BF16)],
        input_output_aliases={0: 0},
        compiler_params=_cparams(("parallel", "arbitrary")),
        name="attn_ctx" if ctx else "attn_lat",
    )(x, qa, qc, ob, ka, va, kc, vc, prm["w_o"], prm["mod"], prm["lam_c"], prm["gco"], seg96)


def _block_ones(width, seg):
    idx = np.arange(width) // seg
    return jnp.asarray((idx[:, None] == idx[None, :]).astype(np.float32), dtype=BF16)


def _rot_matrix(width, half):
    m = np.zeros((width, width), np.float32)
    for j in range(width):
        if (j % (2 * half)) < half:
            m[j + half, j] = -1.0
        else:
            m[j - half, j] = 1.0
    return jnp.asarray(m, dtype=BF16)


def _rope_tables():
    t = np.arange(T_LAT)
    row = (t // GRID_W).astype(np.float32)
    col = (t % GRID_W).astype(np.float32)

    def tab(head_dim, n_heads):
        m = head_dim // 4
        freqs = ROPE_BASE ** (-jnp.arange(m, dtype=F32) / m)
        ang_r = jnp.asarray(row)[:, None] * freqs[None, :]
        ang_c = jnp.asarray(col)[:, None] * freqs[None, :]
        cs = jnp.concatenate([jnp.cos(ang_r)] * 2 + [jnp.cos(ang_c)] * 2, axis=1)
        sn = jnp.concatenate([jnp.sin(ang_r)] * 2 + [jnp.sin(ang_c)] * 2, axis=1)
        return jnp.tile(cs, (1, n_heads)), jnp.tile(sn, (1, n_heads))

    cosa, sina = tab(A_D, 2)
    cosc, sinc = tab(C_DQ, 2 * C_H)
    return cosa, sina, cosc, sinc


def kernel(x_prompt, x_sample, c, cache_a_k, cache_a_v, cache_c_k, cache_c_v, state_gla, c_ctx, w_ada, b_ada,
           g_norm, w_ffn_gate, w_ffn_up, w_ffn_down, w_in, g_a_q, g_a_k, w_gla_up, b_gla, g_gla, g_c_q, g_c_k,
           lam_c, g_c_out, w_out):
    lam_inits = [0.8 - 0.6 * math.exp(-0.3 * l) for l in range(DEPTH)]

    consts = {
        "seg64_384": _block_ones(384, 64), "seg64_128": _block_ones(128, 64),
        "seg48_384": _block_ones(384, 48), "seg96_384": _block_ones(384, 96),
        "seg64_256": _block_ones(256, 64),
        "rot_a384": _rot_matrix(384, 16), "rot_a128": _rot_matrix(128, 16), "rot_c384": _rot_matrix(384, 12),
    }
    rope_tabs = _rope_tables()

    wq = w_in[:, :, 0:384].reshape(DEPTH, D_MODEL, A_KV, A_G, A_D).transpose(0, 1, 3, 2, 4).reshape(DEPTH, D_MODEL, 384)
    seg = lambda a, b: w_in[:, :, a:b]
    w_p = jnp.concatenate([
        wq, seg(384, 512), seg(512, 640),
        seg(640, 768), seg(768, 896), seg(896, 1152),
        seg(1184, 1440),
        seg(1440, 1824), seg(1824, 2208), seg(2208, 2592),
        seg(1152, 1184), jnp.zeros((DEPTH, D_MODEL, 96), F32),
    ], axis=2).astype(BF16)
    wo_a = w_out[:, 0:384].reshape(DEPTH, A_KV, A_G, A_D, D_MODEL).transpose(0, 2, 1, 3, 4).reshape(DEPTH, 384, D_MODEL)
    wup = jnp.zeros((DEPTH, 128, 2 * B_KW), F32)
    wup = wup.at[:, 0:B_RANK, 0:B_KW].set(w_gla_up[:, 0]).at[:, B_RANK:2 * B_RANK, B_KW:].set(w_gla_up[:, 1])
    cond8 = jnp.zeros((8, D_MODEL), F32).at[0].set(c_ctx).at[1:3].set(c)
    prm = {
        "mod": _adaln(cond8, w_ada, b_ada),
        "gn4": g_norm.reshape(DEPTH, 3, 1, D_MODEL),
        "w_p": w_p,
        "w_o": jnp.concatenate([wo_a, w_out[:, 384:]], axis=1).astype(BF16),
        "wup": wup.astype(BF16),
        "bgla": b_gla.reshape(DEPTH, 1, 2 * B_KW),
        "gaq": jnp.tile(g_a_q, (1, 6)).reshape(DEPTH, 1, 384),
        "gak": jnp.tile(g_a_k, (1, 2)).reshape(DEPTH, 1, 128),
        "gcq": jnp.tile(g_c_q.reshape(DEPTH, 96), (1, 4)).reshape(DEPTH, 1, 384),
        "gck": jnp.tile(g_c_k.reshape(DEPTH, 96), (1, 4)).reshape(DEPTH, 1, 384),
        "gco": jnp.tile(g_c_out, (1, 4)).reshape(DEPTH, 1, 384),
        "ggl": jnp.tile(g_gla, (1, 4)).reshape(DEPTH, 1, 256),
        "lam_c": lam_c,
    }

    s0_ctx = jnp.zeros((NB_CTX, 2, B_DV, B_KW), F32)
    caches = [jnp.zeros((NB_CTX, DEPTH, T_CTX, w), F32) for w in (128, 128, 384, 384)]
    states = jnp.zeros((NB_CTX, DEPTH, 2, B_DV, B_KW), F32)
    ffn = functools.partial(_ffn, mod=prm["mod"], gn4=prm["gn4"], wg=w_ffn_gate, wu=w_ffn_up, wd=w_ffn_down)
    x = None
    for l in range(DEPTH):
        if l == 0:
            x = ffn((x_prompt.reshape(N_CTX, D_MODEL), x_sample.reshape(N_LAT, D_MODEL)), l=0, s=0, first=True)
        else:
            x = ffn((x,), l=l, s=0)

        outs = _proj(x, prm, consts, None, caches, l=l, ctx=True)
        qa, ka, va, qc, kc, vc, gq, gk, gv, gla, gr = outs[:11]
        caches = list(outs[11:])
        ob, states = _gla(gq, gk, gv, gla, gr, s0_ctx, prm, consts["seg64_256"], states, l=l, ctx=True)
        r3 = lambda a: a.reshape(NB_CTX, T_CTX, a.shape[-1])
        x = _attn(x, qa, qc, ob, r3(ka), r3(va), r3(kc), r3(vc), prm, consts["seg96_384"],
                  l=l, lam_init=lam_inits[l], ctx=True)

        qa, ka, va, qc, kc, vc, gq, gk, gv, gla, gr = _proj(x, prm, consts, rope_tabs, None, l=l, ctx=False)
        s0 = jnp.swapaxes(state_gla[:, l].reshape(NB_LAT, 2, B_KW, B_DV), 2, 3)
        ob, _ = _gla(gq, gk, gv, gla, gr, s0, prm, consts["seg64_256"], None, l=l, ctx=False)
        cat = lambda new, old, w: jnp.concatenate(
            [new.reshape(NB_LAT, T_LAT, w), old[:, l].reshape(NB_LAT, PAST_LEN, w).astype(BF16)], axis=1)
        x = _attn(x, qa, qc, ob, cat(ka, cache_a_k, 128), cat(va, cache_a_v, 128), cat(kc, cache_c_k, 384),
                  cat(vc, cache_c_v, 384), prm, consts["seg96_384"], l=l, lam_init=lam_inits[l], ctx=False)

        if l == DEPTH - 1:
            y_prompt, y_sample = ffn((x,), l=l, s=1, last=True)
        else:
            x = ffn((x,), l=l, s=1)

    return (y_prompt.reshape(NB_CTX, T_CTX, D_MODEL), y_sample.reshape(NB_LAT, T_LAT, D_MODEL),
            caches[0].reshape(NB_CTX, DEPTH, T_CTX, A_KV, A_D), caches[1].reshape(NB_CTX, DEPTH, T_CTX, A_KV, A_D),
            caches[2].reshape(NB_CTX, DEPTH, T_CTX, C_H, 2 * C_DQ), caches[3].reshape(NB_CTX, DEPTH, T_CTX, C_H, C_DV),
            jnp.swapaxes(states, 3, 4).reshape(NB_CTX, DEPTH, 2, B_H, B_DK, B_DV))
```

```python
import functools
import math

import numpy as np
import jax
import jax.numpy as jnp
from jax import lax
from jax.experimental import pallas as pl
from jax.experimental.pallas import tpu as pltpu

F32 = jnp.float32
BF16 = jnp.bfloat16

D_MODEL = 1024
D_FF = 2816
DEPTH = 4
N_MOD = 9
EPS = 1e-6
ROPE_BASE = 10000.0
GRID_W = 64
LOG2E = math.log2(math.e)

NB_CTX, T_CTX = 32, 256
NB_LAT, T_LAT = 2, 2048
PAST_LEN = 512
N_CTX = NB_CTX * T_CTX
N_LAT = NB_LAT * T_LAT
N_TOK = N_CTX + N_LAT

A_KV, A_G, A_D = 2, 3, 64
A_W = A_KV * A_G * A_D
B_H, B_DK, B_DV = 4, 32, 64
B_KW = B_H * B_DK
B_VW = B_H * B_DV
B_RANK = 16
B_TAU = 16.0
B_CHUNK = 64
C_H, C_DQ, C_DV = 4, 48, 96
C_W = C_H * C_DV

P_AQ, P_AK, P_AV = 0, 384, 512
P_BQ, P_BK, P_BV, P_BR = 640, 768, 896, 1152
P_CQ, P_CK, P_CV = 1408, 1792, 2176
P_BG = 2560
P_W = 2688

TM_FFN = 1024
TF_FFN = 256
TM_PROJ = 256
TQ_ATT = 256
TN_ADA = 1152
GLA_NBB_CTX = 4
GLA_NBB_LAT = 2
A_STACK_CTX, A_STACK_LAT = 6, 2
C_STACK_CTX, C_STACK_LAT = 4, 2
VMEM_LIMIT = 56 * 1024 * 1024


def _cparams(sem):
    return pltpu.CompilerParams(dimension_semantics=sem, vmem_limit_bytes=VMEM_LIMIT)


def _dot(a, b):
    return jnp.dot(a, b, preferred_element_type=F32)


def _dot_nt(a, b):
    return lax.dot_general(a, b, (((1,), (1,)), ((), ())), preferred_element_type=F32)


def _dot_tn(a, b):
    return lax.dot_general(a, b, (((0,), (0,)), ((), ())), preferred_element_type=F32)


def _split_dot(x, m):
    hi = x.astype(BF16)
    lo = (x - hi.astype(F32)).astype(BF16)
    return _dot(hi, m) + _dot(lo, m)


def _split_dot_l(m, x):
    hi = x.astype(BF16)
    lo = (x - hi.astype(F32)).astype(BF16)
    return _dot(m, hi) + _dot(m, lo)


def _seg_rms(x, seg, n):
    ms = _dot((x * x).astype(BF16), seg) * (1.0 / n)
    return x * lax.rsqrt(ms + EPS)


def _silu(x):
    return x * jax.nn.sigmoid(x)


def _norm_mod(x, gn, sc, sh):
    ms = jnp.mean(x * x, axis=-1, keepdims=True)
    return ((x * lax.rsqrt(ms + EPS)) * gn * (1.0 + sc) + sh).astype(BF16)


def _adaln_body(cond_ref, w_ref, b_ref, o_ref):
    c = cond_ref[...]
    sc = _silu(c).astype(BF16)
    o_ref[0] = _dot(sc, w_ref[0].astype(BF16)) + b_ref[0]


def _adaln(cond8, w_ada, b_ada):
    nj = (N_MOD * D_MODEL) // TN_ADA
    return pl.pallas_call(
        _adaln_body,
        out_shape=jax.ShapeDtypeStruct((DEPTH, 8, N_MOD * D_MODEL), F32),
        grid=(DEPTH, nj),
        in_specs=[
            pl.BlockSpec((8, D_MODEL), lambda l, j: (0, 0)),
            pl.BlockSpec((1, D_MODEL, TN_ADA), lambda l, j: (l, 0, j)),
            pl.BlockSpec((1, 1, TN_ADA), lambda l, j: (l, 0, j)),
        ],
        out_specs=pl.BlockSpec((1, 8, TN_ADA), lambda l, j: (l, 0, j)),
        compiler_params=_cparams(("parallel", "parallel")),
        name="adaln",
    )(cond8, w_ada, b_ada.reshape(DEPTH, 1, N_MOD * D_MODEL))


FFN_TILES_CTX = N_CTX // TM_FFN
FFN_TILES_PER_LAT = T_LAT // TM_FFN


def _ffn_body(*refs, first, last):
    it = iter(refs)
    x_refs = (next(it), next(it)) if first else (next(it),)
    sh_ref, sc_ref, gt_ref, gn_ref, wg_ref, wu_ref, wd_ref = (next(it) for _ in range(7))
    o_refs = (next(it), next(it)) if last else (next(it),)
    h_scr, acc_scr = next(it), next(it)

    i = pl.program_id(0)
    j = pl.program_id(1)
    is_ctx = i < FFN_TILES_CTX
    r = jnp.where(is_ctx, 0, 1 + jnp.maximum(i - FFN_TILES_CTX, 0) // FFN_TILES_PER_LAT)

    def on_tile(pred, n_variants, fn):
        if n_variants == 1:
            pl.when(pred)(lambda: fn(0))
        else:
            pl.when(pred & is_ctx)(lambda: fn(0))
            pl.when(pred & jnp.logical_not(is_ctx))(lambda: fn(1))

    def prologue(k):
        h_scr[...] = _norm_mod(x_refs[k][...], gn_ref[...], sc_ref[pl.ds(r, 1), :], sh_ref[pl.ds(r, 1), :])
        acc_scr[...] = jnp.zeros_like(acc_scr)

    on_tile(j == 0, len(x_refs), prologue)

    h = h_scr[...]
    g = _dot(h, wg_ref[...].astype(BF16))
    u = _dot(h, wu_ref[...].astype(BF16))
    a = (_silu(g) * u).astype(BF16)
    acc_scr[...] += _dot(a, wd_ref[...].astype(BF16))

    def epilogue(k):
        x_ref = x_refs[k if first else 0]
        o_ref = o_refs[k if last else 0]
        o_ref[...] = x_ref[...] + 0.5 * gt_ref[pl.ds(r, 1), :] * acc_scr[...]

    on_tile(j == pl.num_programs(1) - 1, max(len(x_refs), len(o_refs)), epilogue)


FFN_NORM_ROWS = 128
FFN_NORM_STEPS = TM_FFN // FFN_NORM_ROWS


def _ffn_pipe_body(*refs, last):
    it = iter(refs)
    x_ref, xn_ref, sh_ref, sc_ref, gt_ref, gn_ref, wg_ref, wu_ref, wd_ref = (next(it) for _ in range(9))
    o_refs = (next(it), next(it)) if last else (next(it),)
    h_even, h_odd, acc_scr = next(it), next(it), next(it)

    i = pl.program_id(0)
    j = pl.program_id(1)
    ni = pl.num_programs(0)

    def mod_row(t):
        return jnp.where(t < FFN_TILES_CTX, 0, 1 + jnp.maximum(t - FFN_TILES_CTX, 0) // FFN_TILES_PER_LAT)

    r = mod_row(i)
    rn = mod_row(jnp.minimum(i + 1, ni - 1))

    @pl.when((i == 0) & (j == 0))
    def _():
        h_even[...] = _norm_mod(x_ref[...], gn_ref[...], sc_ref[pl.ds(r, 1), :], sh_ref[pl.ds(r, 1), :])
        acc_scr[...] = jnp.zeros_like(acc_scr)

    def body(h_cur, h_nxt):
        c = jnp.minimum(j, FFN_NORM_STEPS - 1)
        rows = pl.ds(pl.multiple_of(c * FFN_NORM_ROWS, FFN_NORM_ROWS), FFN_NORM_ROWS)
        h_nxt[rows, :] = _norm_mod(xn_ref[rows, :], gn_ref[...], sc_ref[pl.ds(rn, 1), :], sh_ref[pl.ds(rn, 1), :])
        h = h_cur[...]
        g = _dot(h, wg_ref[...].astype(BF16))
        u = _dot(h, wu_ref[...].astype(BF16))
        a = (_silu(g) * u).astype(BF16)
        acc_scr[...] += _dot(a, wd_ref[...].astype(BF16))

    pl.when(i % 2 == 0)(lambda: body(h_even, h_odd))
    pl.when(i % 2 == 1)(lambda: body(h_odd, h_even))

    def epilogue(o_ref):
        o_ref[...] = x_ref[...] + 0.5 * gt_ref[pl.ds(r, 1), :] * acc_scr[...]
        acc_scr[...] = jnp.zeros_like(acc_scr)

    done = j == pl.num_programs(1) - 1
    if last:
        pl.when(done & (i < FFN_TILES_CTX))(lambda: epilogue(o_refs[0]))
        pl.when(done & (i >= FFN_TILES_CTX))(lambda: epilogue(o_refs[1]))
    else:
        pl.when(done)(lambda: epilogue(o_refs[0]))


def _ffn(xs, mod, gn4, wg, wu, wd, *, l, s, first=False, last=False):
    ni = N_TOK // TM_FFN
    nj = D_FF // TF_FFN
    assert FFN_NORM_STEPS <= nj
    k0 = 6 * s
    gi = 2 * s
    tc = FFN_TILES_CTX
    split_specs = [pl.BlockSpec((TM_FFN, D_MODEL), lambda i, j: (jnp.minimum(i, tc - 1), 0)),
                   pl.BlockSpec((TM_FFN, D_MODEL), lambda i, j: (jnp.maximum(i - tc, 0), 0))]
    one_spec = [pl.BlockSpec((TM_FFN, D_MODEL), lambda i, j: (i, 0))]
    next_spec = [pl.BlockSpec((TM_FFN, D_MODEL), lambda i, j: (jnp.minimum(i + 1, ni - 1), 0))]
    mspec = lambda k: pl.BlockSpec((None, 8, D_MODEL), lambda i, j: (l, 0, k))
    in_specs = (split_specs if first else one_spec + next_spec) + [
        mspec(k0), mspec(k0 + 1), mspec(k0 + 2),
        pl.BlockSpec((None, None, 1, D_MODEL), lambda i, j: (l, gi, 0, 0)),
        pl.BlockSpec((None, None, D_MODEL, TF_FFN), lambda i, j: (l, s, 0, j)),
        pl.BlockSpec((None, None, D_MODEL, TF_FFN), lambda i, j: (l, s, 0, j)),
        pl.BlockSpec((None, None, TF_FFN, D_MODEL), lambda i, j: (l, s, j, 0)),
    ]
    if last:
        out_shape = [jax.ShapeDtypeStruct((N_CTX, D_MODEL), F32), jax.ShapeDtypeStruct((N_LAT, D_MODEL), F32)]
        out_specs = split_specs
    else:
        out_shape = jax.ShapeDtypeStruct((N_TOK, D_MODEL), F32)
        out_specs = one_spec[0]
    h_bufs = 1 if first else 2
    body = functools.partial(_ffn_body, first=True, last=last) if first else functools.partial(_ffn_pipe_body, last=last)
    return pl.pallas_call(
        body,
        out_shape=out_shape,
        grid=(ni, nj),
        in_specs=in_specs,
        out_specs=out_specs,
        scratch_shapes=[pltpu.VMEM((TM_FFN, D_MODEL), BF16)] * h_bufs + [pltpu.VMEM((TM_FFN, D_MODEL), F32)],
        compiler_params=_cparams(("arbitrary", "arbitrary")),
        name="ffn",
    )(*(xs if first else xs + xs), mod, mod, mod, gn4, wg, wu, wd)


def _proj_body(*refs, rope, ctx, tiles_per_batch):
    it = iter(refs)
    x_ref, sh_ref, sc_ref, gn_ref, w_ref, wup_ref, bgla_ref = (next(it) for _ in range(7))
    gaq_ref, gak_ref, gcq_ref, gck_ref = (next(it) for _ in range(4))
    s64a_ref, s64k_ref, s48_ref = (next(it) for _ in range(3))
    if rope:
        cosa_ref, sina_ref, cosc_ref, sinc_ref, ra_ref, rak_ref, rc_ref = (next(it) for _ in range(7))
    if ctx:
        for _ in range(4):
            next(it)
    qa_o, ka_o, va_o, qc_o, kc_o, vc_o, gq_o, gk_o, gv_o, gla_o, gr_o = (next(it) for _ in range(11))
    if ctx:
        ka32_o, va32_o, kc32_o, vc32_o = (next(it) for _ in range(4))

    i = pl.program_id(0)
    r = 0 if ctx else 1 + i // tiles_per_batch

    h = _norm_mod(x_ref[...], gn_ref[...], sc_ref[pl.ds(r, 1), :], sh_ref[pl.ds(r, 1), :])
    p = _dot(h, w_ref[...])

    aq = _seg_rms(p[:, P_AQ:P_AQ + A_W], s64a_ref[...], A_D) * gaq_ref[...]
    ak = _seg_rms(p[:, P_AK:P_AK + 128], s64k_ref[...], A_D) * gak_ref[...]
    av = p[:, P_AV:P_AV + 128]
    cq = _seg_rms(p[:, P_CQ:P_CQ + 384], s48_ref[...], C_DQ) * gcq_ref[...]
    ck = _seg_rms(p[:, P_CK:P_CK + 384], s48_ref[...], C_DQ) * gck_ref[...]
    cv = p[:, P_CV:P_CV + 384]
    if ctx:
        ka32_o[...] = ak
        va32_o[...] = av
        kc32_o[...] = ck
        vc32_o[...] = cv
    if rope:
        cosa = cosa_ref[...]
        sina = sina_ref[...]
        cosa3 = jnp.concatenate([cosa, cosa, cosa], axis=1)
        sina3 = jnp.concatenate([sina, sina, sina], axis=1)
        aq = aq * cosa3 + _dot(aq.astype(BF16), ra_ref[...]) * sina3
        ak = ak * cosa + _dot(ak.astype(BF16), rak_ref[...]) * sina
        cosc = cosc_ref[...]
        sinc = sinc_ref[...]
        cq = cq * cosc + _dot(cq.astype(BF16), rc_ref[...]) * sinc
        ck = ck * cosc + _dot(ck.astype(BF16), rc_ref[...]) * sinc
    qa_o[...] = (aq * (A_D ** -0.5 * LOG2E)).astype(BF16)
    ka_o[...] = ak.astype(BF16)
    va_o[...] = av.astype(BF16)
    qc_o[...] = (cq * (C_DQ ** -0.5 * LOG2E)).astype(BF16)
    kc_o[...] = ck.astype(BF16)
    vc_o[...] = cv.astype(BF16)

    gq_o[...] = p[:, P_BQ:P_BQ + B_KW] * (B_DK ** -0.5)
    gk_o[...] = p[:, P_BK:P_BK + B_KW]
    gv_o[...] = p[:, P_BV:P_BV + B_VW]
    gr_o[...] = _silu(p[:, P_BR:P_BR + B_VW])
    z = _dot(p[:, P_BG:P_BG + 128].astype(BF16), wup_ref[...]) + bgla_ref[...]
    log_sig = jnp.minimum(z, 0.0) - jnp.log1p(jnp.exp(-jnp.abs(z)))
    gla_o[...] = log_sig * (1.0 / B_TAU)


def _proj(x, prm, consts, rope_tabs, caches, *, l, ctx):
    n = N_CTX if ctx else N_LAT
    off = 0 if ctx else N_CTX // TM_PROJ
    tiles_per_batch = (T_CTX if ctx else T_LAT) // TM_PROJ
    rope = not ctx
    full = lambda shape: pl.BlockSpec(shape, lambda i: (0,) * len(shape))
    lay = lambda shape: pl.BlockSpec((None,) + shape, lambda i: (l,) + (0,) * len(shape))
    in_specs = [
        pl.BlockSpec((TM_PROJ, D_MODEL), lambda i: (i + off, 0)),
        pl.BlockSpec((None, 8, D_MODEL), lambda i: (l, 0, 3)),
        pl.BlockSpec((None, 8, D_MODEL), lambda i: (l, 0, 4)),
        pl.BlockSpec((None, None, 1, D_MODEL), lambda i: (l, 1, 0, 0)),
        lay((D_MODEL, P_W)),
        lay((128, 2 * B_KW)),
        lay((1, 2 * B_KW)),
        lay((1, 384)), lay((1, 128)), lay((1, 384)), lay((1, 384)),
        full((384, 384)), full((128, 128)), full((384, 384)),
    ]
    args = [x, prm["mod"], prm["mod"], prm["gn4"], prm["w_p"], prm["wup"], prm["bgla"],
            prm["gaq"], prm["gak"], prm["gcq"], prm["gck"],
            consts["seg64_384"], consts["seg64_128"], consts["seg48_384"]]
    if rope:
        tpb = tiles_per_batch
        in_specs += [
            pl.BlockSpec((TM_PROJ, 128), lambda i: (i % tpb, 0)),
            pl.BlockSpec((TM_PROJ, 128), lambda i: (i % tpb, 0)),
            pl.BlockSpec((TM_PROJ, 384), lambda i: (i % tpb, 0)),
            pl.BlockSpec((TM_PROJ, 384), lambda i: (i % tpb, 0)),
            full((384, 384)), full((128, 128)), full((384, 384)),
        ]
        args += [*rope_tabs, consts["rot_a384"], consts["rot_a128"], consts["rot_c384"]]
    widths = [(384, BF16), (128, BF16), (128, BF16), (384, BF16), (384, BF16), (384, BF16),
              (B_KW, F32), (B_KW, F32), (B_VW, F32), (2 * B_KW, F32), (B_VW, F32)]
    out_shape = [jax.ShapeDtypeStruct((n, w), dt) for w, dt in widths]
    out_specs = [pl.BlockSpec((TM_PROJ, w), lambda i: (i, 0)) for w, _ in widths]
    aliases = {}
    if ctx:
        assert TM_PROJ == T_CTX
        for k, cache in enumerate(caches):
            aliases[len(args)] = len(out_shape)
            in_specs.append(pl.BlockSpec(memory_space=pl.ANY))
            args.append(cache)
            out_shape.append(jax.ShapeDtypeStruct(cache.shape, F32))
            out_specs.append(pl.BlockSpec((None, None, T_CTX, cache.shape[-1]), lambda i: (i, l, 0, 0)))
    return pl.pallas_call(
        functools.partial(_proj_body, rope=rope, ctx=ctx, tiles_per_batch=tiles_per_batch),
        out_shape=out_shape,
        grid=(n // TM_PROJ,),
        in_specs=in_specs,
        out_specs=out_specs,
        input_output_aliases=aliases,
        compiler_params=_cparams(("parallel",)),
        name="proj_ctx" if ctx else "proj_lat",
    )(*args)


GLA_BLK = 4 * B_CHUNK


def _gla_body(*refs, seq, ctx, nbb):
    it = iter(refs)
    gq_ref, gk_ref, gv_ref, gla_ref, gr_ref, s0_ref, gg_ref, seg_ref, cm_ref = (next(it) for _ in range(9))
    if ctx:
        next(it)
    ob_ref, sfin_ref, of_scr, or_scr = (next(it) for _ in range(4))

    C, R = B_CHUNK, GLA_BLK
    ncb = R // C
    nblk = seq // R
    zero16 = jnp.zeros((), BF16)
    rk = lax.broadcasted_iota(jnp.int32, (B_VW, B_KW), 0) >> 6
    ck = lax.broadcasted_iota(jnp.int32, (B_VW, B_KW), 1) >> 5
    hm_k = rk == ck
    ri = lax.broadcasted_iota(jnp.int32, (B_H * R, R), 0) & (R - 1)
    ci = lax.broadcasted_iota(jnp.int32, (B_H * R, R), 1)
    same = (ri >> 6) == (ci >> 6)
    causal = (same & (ci <= ri), same & (ci >= ri))
    lane_k = lax.broadcasted_iota(jnp.int32, (1, B_KW), 1) >> 5
    lane_v = lax.broadcasted_iota(jnp.int32, (1, B_VW), 1) >> 6
    row_c = lax.broadcasted_iota(jnp.int32, (R, 1), 0) >> 6

    def expand_state(st):
        return jnp.where(hm_k, jnp.concatenate([st] * B_H, axis=0), 0.0)

    def block(d, bb, blk, S):
        start = bb * seq + blk * R
        rows = pl.ds(start if isinstance(start, int) else pl.multiple_of(start, R), R)
        q = gq_ref[rows, :]
        k = gk_ref[rows, :]
        v = gv_ref[rows, :].astype(BF16)
        la = gla_ref[rows, d * B_KW:(d + 1) * B_KW]
        bm = _split_dot_l(cm_ref[d], la)
        b, tot, e = bm[0:R], bm[R:2 * R], bm[0:R] - bm[2 * R:3 * R]
        qt = (q * jnp.exp(e)).astype(BF16)
        kt = (k * jnp.exp(-e)).astype(BF16)
        qd = q * jnp.exp(b)
        kd = k * jnp.exp(tot - b)
        qs = jnp.concatenate([jnp.where(lane_k == hh, qt, zero16) for hh in range(B_H)], axis=0)
        a = _dot_nt(qs, kt)
        a = jnp.where(causal[d], a, 0.0).astype(BF16)
        pv = _dot(a, v)
        o = jnp.where(lane_v == 0, pv[0:R], 0.0)
        for hh in range(1, B_H):
            o = o + jnp.where(lane_v == hh, pv[hh * R:(hh + 1) * R], 0.0)
        kd_bd = jnp.concatenate([jnp.where(row_c == c, kd, 0.0) for c in range(ncb)], axis=1).astype(BF16)
        kv_all = _dot_tn(v, kd_bd)
        order = range(ncb) if d == 0 else range(ncb - 1, -1, -1)
        s_in = [None] * ncb
        for c in order:
            s_in[c] = S.astype(BF16)
            S = S * jnp.exp(tot[c * C:c * C + 1, :]) + jnp.where(hm_k, kv_all[:, c * B_KW:(c + 1) * B_KW], 0.0)
        qd_bd = jnp.concatenate([jnp.where(row_c == c, qd, 0.0) for c in range(ncb)], axis=1).astype(BF16)
        o = o + _dot_nt(qd_bd, jnp.concatenate(s_in, axis=1))
        return rows, o, S

    def step(i, carry):
        new = []
        for bb in range(nbb):
            rows, o, Sf = block(0, bb, i, carry[2 * bb])
            of_scr[rows, :] = o
            rows, o, Sb = block(1, bb, nblk - 1 - i, carry[2 * bb + 1])
            or_scr[rows, :] = o
            new += [Sf, Sb]
        return tuple(new)

    carry = tuple(expand_state(s0_ref[bb, d]) for bb in range(nbb) for d in range(2))
    if nblk == 1:
        carry = step(0, carry)
    else:
        carry = lax.fori_loop(0, nblk, step, carry)

    def collapse_state(S):
        Sm = jnp.where(hm_k, S, 0.0)
        acc = Sm[0:B_DV, :]
        for hh in range(1, B_H):
            acc = acc + Sm[hh * B_DV:(hh + 1) * B_DV, :]
        return acc

    for bb in range(nbb):
        sfin_ref[bb, 0] = collapse_state(carry[2 * bb])
        sfin_ref[bb, 1] = collapse_state(carry[2 * bb + 1])

    def fin(t, _):
        rows = pl.ds(pl.multiple_of(t * R, R), R)
        o = of_scr[rows, :] + or_scr[rows, :]
        ms = _split_dot(o * o, seg_ref[...]) * (1.0 / B_DV)
        o = o * lax.rsqrt(ms + EPS) * gg_ref[...]
        ob_ref[rows, :] = (o * gr_ref[rows, :]).astype(BF16)
        return 0

    lax.fori_loop(0, nbb * seq // R, fin, 0)


def _gla_chunk_matrices():
    i = np.arange(GLA_BLK)[:, None]
    j = np.arange(GLA_BLK)[None, :]
    same = (i // B_CHUNK) == (j // B_CHUNK)
    half = B_CHUNK // 2
    fwd = [same & (j <= i), same, same & ((j % B_CHUNK) <= half)]
    bwd = [same & (j >= i), same, same & ((j % B_CHUNK) >= half)]
    m = np.stack([np.concatenate(fwd, axis=0), np.concatenate(bwd, axis=0)]).astype(np.float32)
    return jnp.asarray(m, dtype=BF16)


def _gla(gq, gk, gv, gla, gr, s0, prm, seg, cmat, state_out, *, l, ctx):
    nb, seq, nbb = (NB_CTX, T_CTX, GLA_NBB_CTX) if ctx else (NB_LAT, T_LAT, GLA_NBB_LAT)
    tok = lambda w: pl.BlockSpec((nbb * seq, w), lambda b: (b, 0))
    in_specs = [tok(B_KW), tok(B_KW), tok(B_VW), tok(2 * B_KW), tok(B_VW),
                pl.BlockSpec((nbb, 2, B_DV, B_KW), lambda b: (b, 0, 0, 0)),
                pl.BlockSpec((None, 1, B_VW), lambda b: (l, 0, 0)),
                pl.BlockSpec((B_VW, B_VW), lambda b: (0, 0)),
                pl.BlockSpec((2, 3 * GLA_BLK, GLA_BLK), lambda b: (0, 0, 0))]
    args = [gq, gk, gv, gla, gr, s0, prm["ggl"], seg, cmat]
    aliases = {}
    if ctx:
        aliases[len(args)] = 1
        in_specs.append(pl.BlockSpec(memory_space=pl.ANY))
        args.append(state_out)
        st_shape = jax.ShapeDtypeStruct(state_out.shape, F32)
        st_spec = pl.BlockSpec((nbb, None, 2, B_DV, B_KW), lambda b: (b, l, 0, 0, 0))
    else:
        st_shape = jax.ShapeDtypeStruct((nb, 2, B_DV, B_KW), F32)
        st_spec = pl.BlockSpec((nbb, 2, B_DV, B_KW), lambda b: (b, 0, 0, 0))
    return pl.pallas_call(
        functools.partial(_gla_body, seq=seq, ctx=ctx, nbb=nbb),
        out_shape=[jax.ShapeDtypeStruct((nb * seq, B_VW), BF16), st_shape],
        grid=(nb // nbb,),
        in_specs=in_specs,
        out_specs=[tok(B_VW), st_spec],
        scratch_shapes=[pltpu.VMEM((nbb * seq, B_VW), F32), pltpu.VMEM((nbb * seq, B_VW), F32)],
        input_output_aliases=aliases,
        compiler_params=_cparams(("parallel",)),
        name="gla_ctx" if ctx else "gla_lat",
    )(*args)


def _attn_body(x_ref, qa_ref, qc_ref, ob_ref, ka_ref, va_ref, kc_ref, vc_ref, wout_ref, gt_ref,
               lam_ref, gco_ref, seg_ref, o_ref, mix_scr, *, lam_init, ctx):
    b = pl.program_id(0)
    r = 0 if ctx else 1 + b
    tq = x_ref.shape[0]
    zero16 = jnp.zeros((), BF16)

    a_stack, c_stack = (A_STACK_CTX, C_STACK_CTX) if ctx else (A_STACK_LAT, C_STACK_LAT)

    def softmax_terms(s):
        e = jnp.exp2(s - jnp.max(s, axis=-1, keepdims=True))
        return e, jnp.sum(e, axis=-1, keepdims=True)

    ka = ka_ref[0]
    va = va_ref[0]
    lane = lax.broadcasted_iota(jnp.int32, (1, 128), 1)
    hmask = [(lane >= hh * A_D) & (lane < (hh + 1) * A_D) for hh in range(A_KV)]
    maps = [(g, hh) for g in range(A_G) for hh in range(A_KV)]
    acc = [jnp.zeros((tq, 128), F32) for _ in range(A_G)]
    for g0 in range(0, len(maps), a_stack):
        grp = maps[g0:g0 + a_stack]
        qs = jnp.concatenate([jnp.where(hmask[hh], qa_ref[:, g * 128:(g + 1) * 128], zero16) for g, hh in grp], axis=0)
        e, l = softmax_terms(_dot_nt(qs, ka))
        o = _dot(e.astype(BF16), va) * (1.0 / l)
        for k, (g, hh) in enumerate(grp):
            acc[g] = acc[g] + jnp.where(hmask[hh], o[k * tq:(k + 1) * tq], 0.0)
    for g in range(A_G):
        mix_scr[:, g * 128:(g + 1) * 128] = acc[g].astype(BF16)

    mix_scr[:, A_W:A_W + B_VW] = ob_ref[...]

    lm = lam_ref[...]
    lam = (jnp.exp(jnp.sum(lm[0:1] * lm[1:2], axis=-1, keepdims=True))
           - jnp.exp(jnp.sum(lm[2:3] * lm[3:4], axis=-1, keepdims=True)) + lam_init)
    lane2 = lax.broadcasted_iota(jnp.int32, (1, 256), 1)
    outs = []
    for win in range(2):
        base = win * 128
        kc = kc_ref[0, :, base:base + 256]
        vc = vc_ref[0, :, base:base + 256]
        qc = qc_ref[:, base:base + 256]
        ow = jnp.zeros((tq, 256), F32)
        cmaps = [(hh, mm) for hh in (2 * win, 2 * win + 1) for mm in range(2)]
        for g0 in range(0, len(cmaps), c_stack):
            grp = cmaps[g0:g0 + c_stack]
            qparts = []
            for hh, mm in grp:
                lo = (hh * 2 + mm) * C_DQ - base
                qparts.append(jnp.where((lane2 >= lo) & (lane2 < lo + C_DQ), qc, zero16))
            e, l = softmax_terms(_dot_nt(jnp.concatenate(qparts, axis=0), kc))
            ws, invs = [], []
            for k in range(0, len(grp), 2):
                l0 = l[k * tq:(k + 1) * tq]
                l1 = l[(k + 1) * tq:(k + 2) * tq]
                ws.append((e[k * tq:(k + 1) * tq] - (lam * l0 / l1) * e[(k + 1) * tq:(k + 2) * tq]).astype(BF16))
                invs.append(1.0 / l0)
            rr = _dot(jnp.concatenate(ws, axis=0) if len(ws) > 1 else ws[0], vc)
            for k in range(len(ws)):
                hh = grp[2 * k][0]
                vlo = hh * C_DV - base
                vm = (lane2 >= vlo) & (lane2 < vlo + C_DV)
                ow = ow + jnp.where(vm, rr[k * tq:(k + 1) * tq] * invs[k], 0.0)
        outs.append(ow)
    oc = jnp.concatenate([outs[0][:, :128], outs[0][:, 128:] + outs[1][:, :128], outs[1][:, 128:]], axis=1)
    ms = _split_dot(oc * oc, seg_ref[...]) * (1.0 / C_DV)
    oc = oc * lax.rsqrt(ms + EPS) * gco_ref[...] * (1.0 - lam_init)
    mix_scr[:, A_W + B_VW:] = oc.astype(BF16)

    mixed = _dot(mix_scr[...], wout_ref[...])
    o_ref[...] = x_ref[...] + gt_ref[pl.ds(r, 1), :] * mixed


def _attn(x, qa, qc, ob, ka, va, kc, vc, prm, seg96, *, l, lam_init, ctx):
    nb, seq = (NB_CTX, T_CTX) if ctx else (NB_LAT, T_LAT)
    nq = seq // TQ_ATT
    off = 0 if ctx else N_CTX // TQ_ATT
    tk = ka.shape[1]
    qspec = lambda w: pl.BlockSpec((TQ_ATT, w), lambda b, q: (b * nq + q, 0))
    kspec = lambda w: pl.BlockSpec((1, tk, w), lambda b, q: (b, 0, 0))
    lay = lambda shape: pl.BlockSpec((None,) + shape, lambda b, q: (l,) + (0,) * len(shape))
    xspec = pl.BlockSpec((TQ_ATT, D_MODEL), lambda b, q: (off + b * nq + q, 0))
    return pl.pallas_call(
        functools.partial(_attn_body, lam_init=lam_init, ctx=ctx),
        out_shape=jax.ShapeDtypeStruct((N_TOK, D_MODEL), F32),
        grid=(nb, nq),
        in_specs=[xspec, qspec(384), qspec(384), qspec(B_VW),
                  kspec(128), kspec(128), kspec(384), kspec(384),
                  lay((D_MODEL, D_MODEL)),
                  pl.BlockSpec((None, 8, D_MODEL), lambda b, q: (l, 0, 5)),
                  lay((4, C_DQ)), lay((1, 384)),
                  pl.BlockSpec((384, 384), lambda b, q: (0, 0))],
        out_specs=xspec,
        scratch_shapes=[pltpu.VMEM((TQ_ATT, D_MODEL), BF16)],
        input_output_aliases={0: 0},
        compiler_params=_cparams(("parallel", "arbitrary")),
        name="attn_ctx" if ctx else "attn_lat",
    )(x, qa, qc, ob, ka, va, kc, vc, prm["w_o"], prm["mod"], prm["lam_c"], prm["gco"], seg96)


def _block_ones(width, seg):
    idx = np.arange(width) // seg
    return jnp.asarray((idx[:, None] == idx[None, :]).astype(np.float32), dtype=BF16)


def _rot_matrix(width, half):
    m = np.zeros((width, width), np.float32)
    for j in range(width):
        if (j % (2 * half)) < half:
            m[j + half, j] = -1.0
        else:
            m[j - half, j] = 1.0
    return jnp.asarray(m, dtype=BF16)


def _rope_tables():
    t = np.arange(T_LAT)
    row = (t // GRID_W).astype(np.float32)
    col = (t % GRID_W).astype(np.float32)

    def tab(head_dim, n_heads):
        m = head_dim // 4
        freqs = ROPE_BASE ** (-jnp.arange(m, dtype=F32) / m)
        ang_r = jnp.asarray(row)[:, None] * freqs[None, :]
        ang_c = jnp.asarray(col)[:, None] * freqs[None, :]
        cs = jnp.concatenate([jnp.cos(ang_r)] * 2 + [jnp.cos(ang_c)] * 2, axis=1)
        sn = jnp.concatenate([jnp.sin(ang_r)] * 2 + [jnp.sin(ang_c)] * 2, axis=1)
        return jnp.tile(cs, (1, n_heads)), jnp.tile(sn, (1, n_heads))

    cosa, sina = tab(A_D, 2)
    cosc, sinc = tab(C_DQ, 2 * C_H)
    return cosa, sina, cosc, sinc


def kernel(x_prompt, x_sample, c, cache_a_k, cache_a_v, cache_c_k, cache_c_v, state_gla, c_ctx, w_ada, b_ada,
           g_norm, w_ffn_gate, w_ffn_up, w_ffn_down, w_in, g_a_q, g_a_k, w_gla_up, b_gla, g_gla, g_c_q, g_c_k,
           lam_c, g_c_out, w_out):
    lam_inits = [0.8 - 0.6 * math.exp(-0.3 * l) for l in range(DEPTH)]

    consts = {
        "seg64_384": _block_ones(384, 64), "seg64_128": _block_ones(128, 64),
        "seg48_384": _block_ones(384, 48), "seg96_384": _block_ones(384, 96),
        "seg64_256": _block_ones(256, 64), "gla_cm": _gla_chunk_matrices(),
        "rot_a384": _rot_matrix(384, 16), "rot_a128": _rot_matrix(128, 16), "rot_c384": _rot_matrix(384, 12),
    }
    rope_tabs = _rope_tables()

    wq = w_in[:, :, 0:384].reshape(DEPTH, D_MODEL, A_KV, A_G, A_D).transpose(0, 1, 3, 2, 4).reshape(DEPTH, D_MODEL, 384)
    seg = lambda a, b: w_in[:, :, a:b]
    w_p = jnp.concatenate([
        wq, seg(384, 512), seg(512, 640),
        seg(640, 768), seg(768, 896), seg(896, 1152),
        seg(1184, 1440),
        seg(1440, 1824), seg(1824, 2208), seg(2208, 2592),
        seg(1152, 1184), jnp.zeros((DEPTH, D_MODEL, 96), F32),
    ], axis=2).astype(BF16)
    wo_a = w_out[:, 0:384].reshape(DEPTH, A_KV, A_G, A_D, D_MODEL).transpose(0, 2, 1, 3, 4).reshape(DEPTH, 384, D_MODEL)
    wup = jnp.zeros((DEPTH, 128, 2 * B_KW), F32)
    wup = wup.at[:, 0:B_RANK, 0:B_KW].set(w_gla_up[:, 0]).at[:, B_RANK:2 * B_RANK, B_KW:].set(w_gla_up[:, 1])
    cond8 = jnp.zeros((8, D_MODEL), F32).at[0].set(c_ctx).at[1:3].set(c)
    prm = {
        "mod": _adaln(cond8, w_ada, b_ada),
        "gn4": g_norm.reshape(DEPTH, 3, 1, D_MODEL),
        "w_p": w_p,
        "w_o": jnp.concatenate([wo_a, w_out[:, 384:]], axis=1).astype(BF16),
        "wup": wup.astype(BF16),
        "bgla": b_gla.reshape(DEPTH, 1, 2 * B_KW),
        "gaq": jnp.tile(g_a_q, (1, 6)).reshape(DEPTH, 1, 384),
        "gak": jnp.tile(g_a_k, (1, 2)).reshape(DEPTH, 1, 128),
        "gcq": jnp.tile(g_c_q.reshape(DEPTH, 96), (1, 4)).reshape(DEPTH, 1, 384),
        "gck": jnp.tile(g_c_k.reshape(DEPTH, 96), (1, 4)).reshape(DEPTH, 1, 384),
        "gco": jnp.tile(g_c_out, (1, 4)).reshape(DEPTH, 1, 384),
        "ggl": jnp.tile(g_gla, (1, 4)).reshape(DEPTH, 1, 256),
        "lam_c": lam_c,
    }

    s0_ctx = jnp.zeros((NB_CTX, 2, B_DV, B_KW), F32)
    caches = [jnp.zeros((NB_CTX, DEPTH, T_CTX, w), F32) for w in (128, 128, 384, 384)]
    states = jnp.zeros((NB_CTX, DEPTH, 2, B_DV, B_KW), F32)
    ffn = functools.partial(_ffn, mod=prm["mod"], gn4=prm["gn4"], wg=w_ffn_gate, wu=w_ffn_up, wd=w_ffn_down)
    x = None
    for l in range(DEPTH):
        if l == 0:
            x = ffn((x_prompt.reshape(N_CTX, D_MODEL), x_sample.reshape(N_LAT, D_MODEL)), l=0, s=0, first=True)
        else:
            x = ffn((x,), l=l, s=0)

        outs = _proj(x, prm, consts, None, caches, l=l, ctx=True)
        qa, ka, va, qc, kc, vc, gq, gk, gv, gla, gr = outs[:11]
        caches = list(outs[11:])
        ob, states = _gla(gq, gk, gv, gla, gr, s0_ctx, prm, consts["seg64_256"], consts["gla_cm"], states, l=l, ctx=True)
        r3 = lambda a: a.reshape(NB_CTX, T_CTX, a.shape[-1])
        x = _attn(x, qa, qc, ob, r3(ka), r3(va), r3(kc), r3(vc), prm, consts["seg96_384"],
                  l=l, lam_init=lam_inits[l], ctx=True)

        qa, ka, va, qc, kc, vc, gq, gk, gv, gla, gr = _proj(x, prm, consts, rope_tabs, None, l=l, ctx=False)
        s0 = jnp.swapaxes(state_gla[:, l].reshape(NB_LAT, 2, B_KW, B_DV), 2, 3)
        ob, _ = _gla(gq, gk, gv, gla, gr, s0, prm, consts["seg64_256"], consts["gla_cm"], None, l=l, ctx=False)
        cat = lambda new, old, w: jnp.concatenate(
            [new.reshape(NB_LAT, T_LAT, w), old[:, l].reshape(NB_LAT, PAST_LEN, w).astype(BF16)], axis=1)
        x = _attn(x, qa, qc, ob, cat(ka, cache_a_k, 128), cat(va, cache_a_v, 128), cat(kc, cache_c_k, 384),
                  cat(vc, cache_c_v, 384), prm, consts["seg96_384"], l=l, lam_init=lam_inits[l], ctx=False)

        if l == DEPTH - 1:
            y_prompt, y_sample = ffn((x,), l=l, s=1, last=True)
        else:
            x = ffn((x,), l=l, s=1)

    return (y_prompt.reshape(NB_CTX, T_CTX, D_MODEL), y_sample.reshape(NB_LAT, T_LAT, D_MODEL),
            caches[0].reshape(NB_CTX, DEPTH, T_CTX, A_KV, A_D), caches[1].reshape(NB_CTX, DEPTH, T_CTX, A_KV, A_D),
            caches[2].reshape(NB_CTX, DEPTH, T_CTX, C_H, 2 * C_DQ), caches[3].reshape(NB_CTX, DEPTH, T_CTX, C_H, C_DV),
            jnp.swapaxes(states, 3, 4).reshape(NB_CTX, DEPTH, 2, B_H, B_DK, B_DV))
```

```python
import functools
import math

import numpy as np
import jax
import jax.numpy as jnp
from jax import lax
from jax.experimental import pallas as pl
from jax.experimental.pallas import tpu as pltpu

F32 = jnp.float32
BF16 = jnp.bfloat16

D_MODEL = 1024
D_FF = 2816
DEPTH = 4
N_MOD = 9
EPS = 1e-6
ROPE_BASE = 10000.0
GRID_W = 64
LOG2E = math.log2(math.e)

NB_CTX, T_CTX = 32, 256
NB_LAT, T_LAT = 2, 2048
PAST_LEN = 512
N_CTX = NB_CTX * T_CTX
N_LAT = NB_LAT * T_LAT
N_TOK = N_CTX + N_LAT

A_KV, A_G, A_D = 2, 3, 64
A_W = A_KV * A_G * A_D
B_H, B_DK, B_DV = 4, 32, 64
B_KW = B_H * B_DK
B_VW = B_H * B_DV
B_RANK = 16
B_TAU = 16.0
B_CHUNK = 64
C_H, C_DQ, C_DV = 4, 48, 96
C_W = C_H * C_DV

P_AQ, P_AK, P_AV = 0, 384, 512
P_BQ, P_BK, P_BV, P_BR = 640, 768, 896, 1152
P_CQ, P_CK, P_CV = 1408, 1792, 2176
P_BG = 2560
P_W = 2688

TM_FFN = 1024
TF_FFN = 256
TM_PROJ = 256
TQ_ATT_LAT = 512
ATT_NBB_CTX = 4
TN_ADA = 1152
GLA_NBB_CTX = 4
GLA_NBB_LAT = 2
A_STACK_CTX, A_STACK_LAT = 6, 2
C_STACK_CTX, C_STACK_LAT = 4, 2
VMEM_LIMIT = 56 * 1024 * 1024


def _cparams(sem):
    return pltpu.CompilerParams(dimension_semantics=sem, vmem_limit_bytes=VMEM_LIMIT)


def _dot(a, b):
    return jnp.dot(a, b, preferred_element_type=F32)


def _dot_nt(a, b):
    return lax.dot_general(a, b, (((1,), (1,)), ((), ())), preferred_element_type=F32)


def _dot_tn(a, b):
    return lax.dot_general(a, b, (((0,), (0,)), ((), ())), preferred_element_type=F32)


def _split_dot(x, m):
    hi = x.astype(BF16)
    lo = (x - hi.astype(F32)).astype(BF16)
    return _dot(hi, m) + _dot(lo, m)


def _split_dot_l(m, x):
    hi = x.astype(BF16)
    lo = (x - hi.astype(F32)).astype(BF16)
    return _dot(m, hi) + _dot(m, lo)


def _seg_rms(x, seg, n):
    ms = _dot((x * x).astype(BF16), seg) * (1.0 / n)
    return x * lax.rsqrt(ms + EPS)


def _silu(x):
    return x * jax.nn.sigmoid(x)


def _norm_mod(x, gn, sc, sh):
    ms = jnp.mean(x * x, axis=-1, keepdims=True)
    return ((x * lax.rsqrt(ms + EPS)) * gn * (1.0 + sc) + sh).astype(BF16)


def _adaln_body(cond_ref, w_ref, b_ref, o_ref):
    c = cond_ref[...]
    sc = _silu(c).astype(BF16)
    o_ref[0] = _dot(sc, w_ref[0].astype(BF16)) + b_ref[0]


def _adaln(cond8, w_ada, b_ada):
    nj = (N_MOD * D_MODEL) // TN_ADA
    return pl.pallas_call(
        _adaln_body,
        out_shape=jax.ShapeDtypeStruct((DEPTH, 8, N_MOD * D_MODEL), F32),
        grid=(DEPTH, nj),
        in_specs=[
            pl.BlockSpec((8, D_MODEL), lambda l, j: (0, 0)),
            pl.BlockSpec((1, D_MODEL, TN_ADA), lambda l, j: (l, 0, j)),
            pl.BlockSpec((1, 1, TN_ADA), lambda l, j: (l, 0, j)),
        ],
        out_specs=pl.BlockSpec((1, 8, TN_ADA), lambda l, j: (l, 0, j)),
        compiler_params=_cparams(("parallel", "parallel")),
        name="adaln",
    )(cond8, w_ada, b_ada.reshape(DEPTH, 1, N_MOD * D_MODEL))


FFN_TILES_CTX = N_CTX // TM_FFN
FFN_TILES_PER_LAT = T_LAT // TM_FFN


def _ffn_body(*refs, first, last):
    it = iter(refs)
    x_refs = (next(it), next(it)) if first else (next(it),)
    sh_ref, sc_ref, gt_ref, gn_ref, wg_ref, wu_ref, wd_ref = (next(it) for _ in range(7))
    o_refs = (next(it), next(it)) if last else (next(it),)
    h_scr, acc_scr = next(it), next(it)

    i = pl.program_id(0)
    j = pl.program_id(1)
    is_ctx = i < FFN_TILES_CTX
    r = jnp.where(is_ctx, 0, 1 + jnp.maximum(i - FFN_TILES_CTX, 0) // FFN_TILES_PER_LAT)

    def on_tile(pred, n_variants, fn):
        if n_variants == 1:
            pl.when(pred)(lambda: fn(0))
        else:
            pl.when(pred & is_ctx)(lambda: fn(0))
            pl.when(pred & jnp.logical_not(is_ctx))(lambda: fn(1))

    def prologue(k):
        h_scr[...] = _norm_mod(x_refs[k][...], gn_ref[...], sc_ref[pl.ds(r, 1), :], sh_ref[pl.ds(r, 1), :])
        acc_scr[...] = jnp.zeros_like(acc_scr)

    on_tile(j == 0, len(x_refs), prologue)

    h = h_scr[...]
    g = _dot(h, wg_ref[...].astype(BF16))
    u = _dot(h, wu_ref[...].astype(BF16))
    a = (_silu(g) * u).astype(BF16)
    acc_scr[...] += _dot(a, wd_ref[...].astype(BF16))

    def epilogue(k):
        x_ref = x_refs[k if first else 0]
        o_ref = o_refs[k if last else 0]
        o_ref[...] = x_ref[...] + 0.5 * gt_ref[pl.ds(r, 1), :] * acc_scr[...]

    on_tile(j == pl.num_programs(1) - 1, max(len(x_refs), len(o_refs)), epilogue)


def _ffn(xs, mod, gn4, wg, wu, wd, *, l, s, first=False, last=False):
    ni = N_TOK // TM_FFN
    nj = D_FF // TF_FFN
    k0 = 6 * s
    gi = 2 * s
    tc = FFN_TILES_CTX
    split_specs = [pl.BlockSpec((TM_FFN, D_MODEL), lambda i, j: (jnp.minimum(i, tc - 1), 0)),
                   pl.BlockSpec((TM_FFN, D_MODEL), lambda i, j: (jnp.maximum(i - tc, 0), 0))]
    one_spec = [pl.BlockSpec((TM_FFN, D_MODEL), lambda i, j: (i, 0))]
    mspec = lambda k: pl.BlockSpec((None, 8, D_MODEL), lambda i, j: (l, 0, k))
    in_specs = (split_specs if first else one_spec) + [
        mspec(k0), mspec(k0 + 1), mspec(k0 + 2),
        pl.BlockSpec((None, None, 1, D_MODEL), lambda i, j: (l, gi, 0, 0)),
        pl.BlockSpec((None, None, D_MODEL, TF_FFN), lambda i, j: (l, s, 0, j)),
        pl.BlockSpec((None, None, D_MODEL, TF_FFN), lambda i, j: (l, s, 0, j)),
        pl.BlockSpec((None, None, TF_FFN, D_MODEL), lambda i, j: (l, s, j, 0)),
    ]
    if last:
        out_shape = [jax.ShapeDtypeStruct((N_CTX, D_MODEL), F32), jax.ShapeDtypeStruct((N_LAT, D_MODEL), F32)]
        out_specs = split_specs
    else:
        out_shape = jax.ShapeDtypeStruct((N_TOK, D_MODEL), F32)
        out_specs = one_spec[0]
    return pl.pallas_call(
        functools.partial(_ffn_body, first=first, last=last),
        out_shape=out_shape,
        grid=(ni, nj),
        in_specs=in_specs,
        out_specs=out_specs,
        scratch_shapes=[pltpu.VMEM((TM_FFN, D_MODEL), BF16), pltpu.VMEM((TM_FFN, D_MODEL), F32)],
        compiler_params=_cparams(("arbitrary", "arbitrary")),
        name="ffn",
    )(*xs, mod, mod, mod, gn4, wg, wu, wd)


def _proj_body(*refs, rope, ctx, tiles_per_batch):
    it = iter(refs)
    x_ref, sh_ref, sc_ref, gn_ref, w_ref, wup_ref, bgla_ref = (next(it) for _ in range(7))
    gaq_ref, gak_ref, gcq_ref, gck_ref = (next(it) for _ in range(4))
    s64a_ref, s64k_ref, s48_ref = (next(it) for _ in range(3))
    if rope:
        cosa_ref, sina_ref, cosc_ref, sinc_ref, ra_ref, rak_ref, rc_ref = (next(it) for _ in range(7))
    if ctx:
        for _ in range(4):
            next(it)
    qa_o, ka_o, va_o, qc_o, kc_o, vc_o, gq_o, gk_o, gv_o, gla_o, gr_o = (next(it) for _ in range(11))
    if ctx:
        ka32_o, va32_o, kc32_o, vc32_o = (next(it) for _ in range(4))

    i = pl.program_id(0)
    r = 0 if ctx else 1 + i // tiles_per_batch

    h = _norm_mod(x_ref[...], gn_ref[...], sc_ref[pl.ds(r, 1), :], sh_ref[pl.ds(r, 1), :])
    p = _dot(h, w_ref[...])

    aq = _seg_rms(p[:, P_AQ:P_AQ + A_W], s64a_ref[...], A_D) * gaq_ref[...]
    ak = _seg_rms(p[:, P_AK:P_AK + 128], s64k_ref[...], A_D) * gak_ref[...]
    av = p[:, P_AV:P_AV + 128]
    cq = _seg_rms(p[:, P_CQ:P_CQ + 384], s48_ref[...], C_DQ) * gcq_ref[...]
    ck = _seg_rms(p[:, P_CK:P_CK + 384], s48_ref[...], C_DQ) * gck_ref[...]
    cv = p[:, P_CV:P_CV + 384]
    if ctx:
        ka32_o[...] = ak
        va32_o[...] = av
        kc32_o[...] = ck
        vc32_o[...] = cv
    if rope:
        cosa = cosa_ref[...]
        sina = sina_ref[...]
        cosa3 = jnp.concatenate([cosa, cosa, cosa], axis=1)
        sina3 = jnp.concatenate([sina, sina, sina], axis=1)
        aq = aq * cosa3 + _dot(aq.astype(BF16), ra_ref[...]) * sina3
        ak = ak * cosa + _dot(ak.astype(BF16), rak_ref[...]) * sina
        cosc = cosc_ref[...]
        sinc = sinc_ref[...]
        cq = cq * cosc + _dot(cq.astype(BF16), rc_ref[...]) * sinc
        ck = ck * cosc + _dot(ck.astype(BF16), rc_ref[...]) * sinc
    qa_o[...] = (aq * (A_D ** -0.5 * LOG2E)).astype(BF16)
    ka_o[...] = ak.astype(BF16)
    va_o[...] = av.astype(BF16)
    qc_o[...] = (cq * (C_DQ ** -0.5 * LOG2E)).astype(BF16)
    kc_o[...] = ck.astype(BF16)
    vc_o[...] = cv.astype(BF16)

    gq_o[...] = p[:, P_BQ:P_BQ + B_KW] * (B_DK ** -0.5)
    gk_o[...] = p[:, P_BK:P_BK + B_KW]
    gv_o[...] = p[:, P_BV:P_BV + B_VW]
    gr_o[...] = _silu(p[:, P_BR:P_BR + B_VW])
    z = _dot(p[:, P_BG:P_BG + 128].astype(BF16), wup_ref[...]) + bgla_ref[...]
    log_sig = jnp.minimum(z, 0.0) - jnp.log1p(jnp.exp(-jnp.abs(z)))
    gla_o[...] = log_sig * (1.0 / B_TAU)


def _proj(x, prm, consts, rope_tabs, caches, *, l, ctx):
    n = N_CTX if ctx else N_LAT
    off = 0 if ctx else N_CTX // TM_PROJ
    tiles_per_batch = (T_CTX if ctx else T_LAT) // TM_PROJ
    rope = not ctx
    full = lambda shape: pl.BlockSpec(shape, lambda i: (0,) * len(shape))
    lay = lambda shape: pl.BlockSpec((None,) + shape, lambda i: (l,) + (0,) * len(shape))
    in_specs = [
        pl.BlockSpec((TM_PROJ, D_MODEL), lambda i: (i + off, 0)),
        pl.BlockSpec((None, 8, D_MODEL), lambda i: (l, 0, 3)),
        pl.BlockSpec((None, 8, D_MODEL), lambda i: (l, 0, 4)),
        pl.BlockSpec((None, None, 1, D_MODEL), lambda i: (l, 1, 0, 0)),
        lay((D_MODEL, P_W)),
        lay((128, 2 * B_KW)),
        lay((1, 2 * B_KW)),
        lay((1, 384)), lay((1, 128)), lay((1, 384)), lay((1, 384)),
        full((384, 384)), full((128, 128)), full((384, 384)),
    ]
    args = [x, prm["mod"], prm["mod"], prm["gn4"], prm["w_p"], prm["wup"], prm["bgla"],
            prm["gaq"], prm["gak"], prm["gcq"], prm["gck"],
            consts["seg64_384"], consts["seg64_128"], consts["seg48_384"]]
    if rope:
        tpb = tiles_per_batch
        in_specs += [
            pl.BlockSpec((TM_PROJ, 128), lambda i: (i % tpb, 0)),
            pl.BlockSpec((TM_PROJ, 128), lambda i: (i % tpb, 0)),
            pl.BlockSpec((TM_PROJ, 384), lambda i: (i % tpb, 0)),
            pl.BlockSpec((TM_PROJ, 384), lambda i: (i % tpb, 0)),
            full((384, 384)), full((128, 128)), full((384, 384)),
        ]
        args += [*rope_tabs, consts["rot_a384"], consts["rot_a128"], consts["rot_c384"]]
    widths = [(384, BF16), (128, BF16), (128, BF16), (384, BF16), (384, BF16), (384, BF16),
              (B_KW, F32), (B_KW, F32), (B_VW, F32), (2 * B_KW, F32), (B_VW, F32)]
    out_shape = [jax.ShapeDtypeStruct((n, w), dt) for w, dt in widths]
    out_specs = [pl.BlockSpec((TM_PROJ, w), lambda i: (i, 0)) for w, _ in widths]
    aliases = {}
    if ctx:
        assert TM_PROJ == T_CTX
        for k, cache in enumerate(caches):
            aliases[len(args)] = len(out_shape)
            in_specs.append(pl.BlockSpec(memory_space=pl.ANY))
            args.append(cache)
            out_shape.append(jax.ShapeDtypeStruct(cache.shape, F32))
            out_specs.append(pl.BlockSpec((None, None, T_CTX, cache.shape[-1]), lambda i: (i, l, 0, 0)))
    return pl.pallas_call(
        functools.partial(_proj_body, rope=rope, ctx=ctx, tiles_per_batch=tiles_per_batch),
        out_shape=out_shape,
        grid=(n // TM_PROJ,),
        in_specs=in_specs,
        out_specs=out_specs,
        input_output_aliases=aliases,
        compiler_params=_cparams(("parallel",)),
        name="proj_ctx" if ctx else "proj_lat",
    )(*args)


def _gla_body(*refs, seq, ctx, nbb):
    it = iter(refs)
    gq_ref, gk_ref, gv_ref, gla_ref, gr_ref, s0_ref, gg_ref, seg_ref = (next(it) for _ in range(8))
    if ctx:
        next(it)
    ob_ref, sfin_ref, of_scr, or_scr = (next(it) for _ in range(4))

    n = seq // B_CHUNK
    C = B_CHUNK
    ri = lax.broadcasted_iota(jnp.int32, (C, C), 0)
    ci = lax.broadcasted_iota(jnp.int32, (C, C), 1)
    cum = (jnp.where(ci <= ri, 1.0, 0.0).astype(BF16), jnp.where(ci >= ri, 1.0, 0.0).astype(BF16))
    rk = lax.broadcasted_iota(jnp.int32, (B_H * C, B_KW), 0) >> 6
    ck = lax.broadcasted_iota(jnp.int32, (B_H * C, B_KW), 1) >> 5
    hm_k = rk == ck
    rv = lax.broadcasted_iota(jnp.int32, (B_H * C, B_VW), 0) >> 6
    cv = lax.broadcasted_iota(jnp.int32, (B_H * C, B_VW), 1) >> 6
    hm_v = rv == cv
    ra = lax.broadcasted_iota(jnp.int32, (C, B_H * C), 0)
    ca = lax.broadcasted_iota(jnp.int32, (C, B_H * C), 1) & (C - 1)
    tri = (ca <= ra, ca >= ra)
    zero16 = jnp.zeros((), BF16)

    def expand_state(st):
        return jnp.where(hm_k, jnp.concatenate([st] * B_H, axis=0), 0.0)

    def chunk(d, bb, c, S):
        start = bb * seq + c * C
        rows = pl.ds(start if isinstance(start, int) else pl.multiple_of(start, C), C)
        q = gq_ref[rows, :]
        k = gk_ref[rows, :]
        v = gv_ref[rows, :].astype(BF16)
        la = gla_ref[rows, d * B_KW:(d + 1) * B_KW]
        b = _split_dot_l(cum[d], la)
        tot = b[C - 1:C, :] if d == 0 else b[0:1, :]
        e = b - b[C // 2:C // 2 + 1, :]
        qt = (q * jnp.exp(e)).astype(BF16)
        kt = (k * jnp.exp(-e)).astype(BF16)
        qd = (q * jnp.exp(b)).astype(BF16)
        kd = (k * jnp.exp(tot - b)).astype(BF16)
        kbd = jnp.where(hm_k, jnp.concatenate([kt] * B_H, axis=0), zero16)
        a = _dot_nt(qt, kbd)
        a = jnp.where(tri[d], a, 0.0).astype(BF16)
        vbd = jnp.where(hm_v, jnp.concatenate([v] * B_H, axis=0), zero16)
        o = _dot(a, vbd) + _dot_nt(qd, S.astype(BF16))
        kv = _dot_tn(v, kd)
        S_new = S * jnp.exp(tot) + jnp.where(hm_k, kv, 0.0)
        return rows, o, S_new

    def step(i, carry):
        new = []
        for bb in range(nbb):
            rows, o, Sf = chunk(0, bb, i, carry[2 * bb])
            of_scr[rows, :] = o
            rows, o, Sb = chunk(1, bb, n - 1 - i, carry[2 * bb + 1])
            or_scr[rows, :] = o
            new += [Sf, Sb]
        return tuple(new)

    carry = tuple(expand_state(s0_ref[bb, d]) for bb in range(nbb) for d in range(2))
    if n <= 4:
        for i in range(n):
            carry = step(i, carry)
    else:
        carry = lax.fori_loop(0, n, step, carry, unroll=2)

    def collapse_state(S):
        Sm = jnp.where(hm_k, S, 0.0)
        acc = Sm[0:C, :]
        for hh in range(1, B_H):
            acc = acc + Sm[hh * C:(hh + 1) * C, :]
        return acc

    for bb in range(nbb):
        sfin_ref[bb, 0] = collapse_state(carry[2 * bb])
        sfin_ref[bb, 1] = collapse_state(carry[2 * bb + 1])

    RT = 256

    def fin(t, _):
        rows = pl.ds(pl.multiple_of(t * RT, RT), RT)
        o = of_scr[rows, :] + or_scr[rows, :]
        ms = _split_dot(o * o, seg_ref[...]) * (1.0 / B_DV)
        o = o * lax.rsqrt(ms + EPS) * gg_ref[...]
        ob_ref[rows, :] = (o * gr_ref[rows, :]).astype(BF16)
        return 0

    lax.fori_loop(0, nbb * seq // RT, fin, 0)


def _gla(gq, gk, gv, gla, gr, s0, prm, seg, state_out, *, l, ctx):
    nb, seq, nbb = (NB_CTX, T_CTX, GLA_NBB_CTX) if ctx else (NB_LAT, T_LAT, GLA_NBB_LAT)
    tok = lambda w: pl.BlockSpec((nbb * seq, w), lambda b: (b, 0))
    in_specs = [tok(B_KW), tok(B_KW), tok(B_VW), tok(2 * B_KW), tok(B_VW),
                pl.BlockSpec((nbb, 2, B_DV, B_KW), lambda b: (b, 0, 0, 0)),
                pl.BlockSpec((None, 1, B_VW), lambda b: (l, 0, 0)),
                pl.BlockSpec((B_VW, B_VW), lambda b: (0, 0))]
    args = [gq, gk, gv, gla, gr, s0, prm["ggl"], seg]
    aliases = {}
    if ctx:
        aliases[len(args)] = 1
        in_specs.append(pl.BlockSpec(memory_space=pl.ANY))
        args.append(state_out)
        st_shape = jax.ShapeDtypeStruct(state_out.shape, F32)
        st_spec = pl.BlockSpec((nbb, None, 2, B_DV, B_KW), lambda b: (b, l, 0, 0, 0))
    else:
        st_shape = jax.ShapeDtypeStruct((nb, 2, B_DV, B_KW), F32)
        st_spec = pl.BlockSpec((nbb, 2, B_DV, B_KW), lambda b: (b, 0, 0, 0))
    return pl.pallas_call(
        functools.partial(_gla_body, seq=seq, ctx=ctx, nbb=nbb),
        out_shape=[jax.ShapeDtypeStruct((nb * seq, B_VW), BF16), st_shape],
        grid=(nb // nbb,),
        in_specs=in_specs,
        out_specs=[tok(B_VW), st_spec],
        scratch_shapes=[pltpu.VMEM((nbb * seq, B_VW), F32), pltpu.VMEM((nbb * seq, B_VW), F32)],
        input_output_aliases=aliases,
        compiler_params=_cparams(("parallel",)),
        name="gla_ctx" if ctx else "gla_lat",
    )(*args)


def _attn_body(x_ref, qa_ref, qc_ref, ob_ref, ka_ref, va_ref, kc_ref, vc_ref, wout_ref, gt_ref,
               lam_ref, gco_ref, seg_ref, o_ref, mix_scr, *, lam_init, ctx, nbb):
    b = pl.program_id(0)
    r = 0 if ctx else 1 + b
    tq = x_ref.shape[0] // nbb
    zero16 = jnp.zeros((), BF16)
    a_stack, c_stack = (A_STACK_CTX, C_STACK_CTX) if ctx else (A_STACK_LAT, C_STACK_LAT)
    lane = lax.broadcasted_iota(jnp.int32, (1, 128), 1)
    lane2 = lax.broadcasted_iota(jnp.int32, (1, 256), 1)
    hmask = [(lane >= hh * A_D) & (lane < (hh + 1) * A_D) for hh in range(A_KV)]
    lm = lam_ref[...]
    lam = (jnp.exp(jnp.sum(lm[0:1] * lm[1:2], axis=-1, keepdims=True))
           - jnp.exp(jnp.sum(lm[2:3] * lm[3:4], axis=-1, keepdims=True)) + lam_init)

    def softmax_terms(s):
        e = jnp.exp2(s - jnp.max(s, axis=-1, keepdims=True))
        return e, jnp.sum(e, axis=-1, keepdims=True)

    for bb in range(nbb):
        _attn_mix_rows(bb, tq, qa_ref, qc_ref, ob_ref, ka_ref, va_ref, kc_ref, vc_ref, gco_ref, seg_ref, mix_scr,
                       a_stack, c_stack, hmask, lane2, lam, lam_init, softmax_terms)
    mixed = _dot(mix_scr[...], wout_ref[...])
    o_ref[...] = x_ref[...] + gt_ref[pl.ds(r, 1), :] * mixed


def _attn_mix_rows(bb, tq, qa_ref, qc_ref, ob_ref, ka_ref, va_ref, kc_ref, vc_ref, gco_ref, seg_ref, mix_scr,
                   a_stack, c_stack, hmask, lane2, lam, lam_init, softmax_terms):
    rows = slice(bb * tq, (bb + 1) * tq)
    zero16 = jnp.zeros((), BF16)

    ka = ka_ref[bb]
    va = va_ref[bb]
    maps = [(g, hh) for g in range(A_G) for hh in range(A_KV)]
    acc = [jnp.zeros((tq, 128), F32) for _ in range(A_G)]
    for g0 in range(0, len(maps), a_stack):
        grp = maps[g0:g0 + a_stack]
        qs = jnp.concatenate(
            [jnp.where(hmask[hh], qa_ref[rows, g * 128:(g + 1) * 128], zero16) for g, hh in grp], axis=0)
        e, l = softmax_terms(_dot_nt(qs, ka))
        o = _dot(e.astype(BF16), va) * (1.0 / l)
        for k, (g, hh) in enumerate(grp):
            acc[g] = acc[g] + jnp.where(hmask[hh], o[k * tq:(k + 1) * tq], 0.0)
    for g in range(A_G):
        mix_scr[rows, g * 128:(g + 1) * 128] = acc[g].astype(BF16)

    mix_scr[rows, A_W:A_W + B_VW] = ob_ref[rows, :]

    outs = []
    for win in range(2):
        base = win * 128
        kc = kc_ref[bb, :, base:base + 256]
        vc = vc_ref[bb, :, base:base + 256]
        qc = qc_ref[rows, base:base + 256]
        ow = jnp.zeros((tq, 256), F32)
        cmaps = [(hh, mm) for hh in (2 * win, 2 * win + 1) for mm in range(2)]
        for g0 in range(0, len(cmaps), c_stack):
            grp = cmaps[g0:g0 + c_stack]
            qparts = []
            for hh, mm in grp:
                lo = (hh * 2 + mm) * C_DQ - base
                qparts.append(jnp.where((lane2 >= lo) & (lane2 < lo + C_DQ), qc, zero16))
            e, l = softmax_terms(_dot_nt(jnp.concatenate(qparts, axis=0), kc))
            ws, invs = [], []
            for k in range(0, len(grp), 2):
                l0 = l[k * tq:(k + 1) * tq]
                l1 = l[(k + 1) * tq:(k + 2) * tq]
                ws.append((e[k * tq:(k + 1) * tq] - (lam * l0 / l1) * e[(k + 1) * tq:(k + 2) * tq]).astype(BF16))
                invs.append(1.0 / l0)
            rr = _dot(jnp.concatenate(ws, axis=0) if len(ws) > 1 else ws[0], vc)
            for k in range(len(ws)):
                hh = grp[2 * k][0]
                vlo = hh * C_DV - base
                vm = (lane2 >= vlo) & (lane2 < vlo + C_DV)
                ow = ow + jnp.where(vm, rr[k * tq:(k + 1) * tq] * invs[k], 0.0)
        outs.append(ow)
    oc = jnp.concatenate([outs[0][:, :128], outs[0][:, 128:] + outs[1][:, :128], outs[1][:, 128:]], axis=1)
    ms = _split_dot(oc * oc, seg_ref[...]) * (1.0 / C_DV)
    oc = oc * lax.rsqrt(ms + EPS) * gco_ref[...] * (1.0 - lam_init)
    mix_scr[rows, A_W + B_VW:] = oc.astype(BF16)


def _attn(x, qa, qc, ob, ka, va, kc, vc, prm, seg96, *, l, lam_init, ctx):
    nb, seq, nbb, tq = (NB_CTX, T_CTX, ATT_NBB_CTX, T_CTX) if ctx else (NB_LAT, T_LAT, 1, TQ_ATT_LAT)
    nq = seq // tq
    rb = nbb * tq
    off = 0 if ctx else N_CTX // rb
    tk = ka.shape[1]
    qspec = lambda w: pl.BlockSpec((rb, w), lambda b, q: (b * nq + q, 0))
    kspec = lambda w: pl.BlockSpec((nbb, tk, w), lambda b, q: (b, 0, 0),
                                   pipeline_mode=None if ctx else pl.Buffered(1))
    lay = lambda shape: pl.BlockSpec((None,) + shape, lambda b, q: (l,) + (0,) * len(shape),
                                     pipeline_mode=pl.Buffered(1))
    xspec = pl.BlockSpec((rb, D_MODEL), lambda b, q: (off + b * nq + q, 0))
    return pl.pallas_call(
        functools.partial(_attn_body, lam_init=lam_init, ctx=ctx, nbb=nbb),
        out_shape=jax.ShapeDtypeStruct((N_TOK, D_MODEL), F32),
        grid=(nb // nbb, nq),
        in_specs=[xspec, qspec(384), qspec(384), qspec(B_VW),
                  kspec(128), kspec(128), kspec(384), kspec(384),
                  lay((D_MODEL, D_MODEL)),
                  pl.BlockSpec((None, 8, D_MODEL), lambda b, q: (l, 0, 5)),
                  lay((4, C_DQ)), lay((1, 384)),
                  pl.BlockSpec((384, 384), lambda b, q: (0, 0))],
        out_specs=xspec,
        scratch_shapes=[pltpu.VMEM((rb, D_MODEL), BF16)],
        input_output_aliases={0: 0},
        compiler_params=_cparams(("parallel", "arbitrary")),
        name="attn_ctx" if ctx else "attn_lat",
    )(x, qa, qc, ob, ka, va, kc, vc, prm["w_o"], prm["mod"], prm["lam_c"], prm["gco"], seg96)


def _block_ones(width, seg):
    idx = np.arange(width) // seg
    return jnp.asarray((idx[:, None] == idx[None, :]).astype(np.float32), dtype=BF16)


def _rot_matrix(width, half):
    m = np.zeros((width, width), np.float32)
    for j in range(width):
        if (j % (2 * half)) < half:
            m[j + half, j] = -1.0
        else:
            m[j - half, j] = 1.0
    return jnp.asarray(m, dtype=BF16)


def _rope_tables():
    t = np.arange(T_LAT)
    row = (t // GRID_W).astype(np.float32)
    col = (t % GRID_W).astype(np.float32)

    def tab(head_dim, n_heads):
        m = head_dim // 4
        freqs = ROPE_BASE ** (-jnp.arange(m, dtype=F32) / m)
        ang_r = jnp.asarray(row)[:, None] * freqs[None, :]
        ang_c = jnp.asarray(col)[:, None] * freqs[None, :]
        cs = jnp.concatenate([jnp.cos(ang_r)] * 2 + [jnp.cos(ang_c)] * 2, axis=1)
        sn = jnp.concatenate([jnp.sin(ang_r)] * 2 + [jnp.sin(ang_c)] * 2, axis=1)
        return jnp.tile(cs, (1, n_heads)), jnp.tile(sn, (1, n_heads))

    cosa, sina = tab(A_D, 2)
    cosc, sinc = tab(C_DQ, 2 * C_H)
    return cosa, sina, cosc, sinc


def kernel(x_prompt, x_sample, c, cache_a_k, cache_a_v, cache_c_k, cache_c_v, state_gla, c_ctx, w_ada, b_ada,
           g_norm, w_ffn_gate, w_ffn_up, w_ffn_down, w_in, g_a_q, g_a_k, w_gla_up, b_gla, g_gla, g_c_q, g_c_k,
           lam_c, g_c_out, w_out):
    lam_inits = [0.8 - 0.6 * math.exp(-0.3 * l) for l in range(DEPTH)]

    consts = {
        "seg64_384": _block_ones(384, 64), "seg64_128": _block_ones(128, 64),
        "seg48_384": _block_ones(384, 48), "seg96_384": _block_ones(384, 96),
        "seg64_256": _block_ones(256, 64),
        "rot_a384": _rot_matrix(384, 16), "rot_a128": _rot_matrix(128, 16), "rot_c384": _rot_matrix(384, 12),
    }
    rope_tabs = _rope_tables()

    wq = w_in[:, :, 0:384].reshape(DEPTH, D_MODEL, A_KV, A_G, A_D).transpose(0, 1, 3, 2, 4).reshape(DEPTH, D_MODEL, 384)
    seg = lambda a, b: w_in[:, :, a:b]
    w_p = jnp.concatenate([
        wq, seg(384, 512), seg(512, 640),
        seg(640, 768), seg(768, 896), seg(896, 1152),
        seg(1184, 1440),
        seg(1440, 1824), seg(1824, 2208), seg(2208, 2592),
        seg(1152, 1184), jnp.zeros((DEPTH, D_MODEL, 96), F32),
    ], axis=2).astype(BF16)
    wo_a = w_out[:, 0:384].reshape(DEPTH, A_KV, A_G, A_D, D_MODEL).transpose(0, 2, 1, 3, 4).reshape(DEPTH, 384, D_MODEL)
    wup = jnp.zeros((DEPTH, 128, 2 * B_KW), F32)
    wup = wup.at[:, 0:B_RANK, 0:B_KW].set(w_gla_up[:, 0]).at[:, B_RANK:2 * B_RANK, B_KW:].set(w_gla_up[:, 1])
    cond8 = jnp.zeros((8, D_MODEL), F32).at[0].set(c_ctx).at[1:3].set(c)
    prm = {
        "mod": _adaln(cond8, w_ada, b_ada),
        "gn4": g_norm.reshape(DEPTH, 3, 1, D_MODEL),
        "w_p": w_p,
        "w_o": jnp.concatenate([wo_a, w_out[:, 384:]], axis=1).astype(BF16),
        "wup": wup.astype(BF16),
        "bgla": b_gla.reshape(DEPTH, 1, 2 * B_KW),
        "gaq": jnp.tile(g_a_q, (1, 6)).reshape(DEPTH, 1, 384),
        "gak": jnp.tile(g_a_k, (1, 2)).reshape(DEPTH, 1, 128),
        "gcq": jnp.tile(g_c_q.reshape(DEPTH, 96), (1, 4)).reshape(DEPTH, 1, 384),
        "gck": jnp.tile(g_c_k.reshape(DEPTH, 96), (1, 4)).reshape(DEPTH, 1, 384),
        "gco": jnp.tile(g_c_out, (1, 4)).reshape(DEPTH, 1, 384),
        "ggl": jnp.tile(g_gla, (1, 4)).reshape(DEPTH, 1, 256),
        "lam_c": lam_c,
    }

    s0_ctx = jnp.zeros((NB_CTX, 2, B_DV, B_KW), F32)
    caches = [jnp.zeros((NB_CTX, DEPTH, T_CTX, w), F32) for w in (128, 128, 384, 384)]
    states = jnp.zeros((NB_CTX, DEPTH, 2, B_DV, B_KW), F32)
    ffn = functools.partial(_ffn, mod=prm["mod"], gn4=prm["gn4"], wg=w_ffn_gate, wu=w_ffn_up, wd=w_ffn_down)
    x = None
    for l in range(DEPTH):
        if l == 0:
            x = ffn((x_prompt.reshape(N_CTX, D_MODEL), x_sample.reshape(N_LAT, D_MODEL)), l=0, s=0, first=True)
        else:
            x = ffn((x,), l=l, s=0)

        outs = _proj(x, prm, consts, None, caches, l=l, ctx=True)
        qa, ka, va, qc, kc, vc, gq, gk, gv, gla, gr = outs[:11]
        caches = list(outs[11:])
        ob, states = _gla(gq, gk, gv, gla, gr, s0_ctx, prm, consts["seg64_256"],states, l=l, ctx=True)
        r3 = lambda a: a.reshape(NB_CTX, T_CTX, a.shape[-1])
        x = _attn(x, qa, qc, ob, r3(ka), r3(va), r3(kc), r3(vc), prm, consts["seg96_384"],
                  l=l, lam_init=lam_inits[l], ctx=True)

        qa, ka, va, qc, kc, vc, gq, gk, gv, gla, gr = _proj(x, prm, consts, rope_tabs, None, l=l, ctx=False)
        s0 = jnp.swapaxes(state_gla[:, l].reshape(NB_LAT, 2, B_KW, B_DV), 2, 3)
        ob, _ = _gla(gq, gk, gv, gla, gr, s0, prm, consts["seg64_256"],None, l=l, ctx=False)
        cat = lambda new, old, w: jnp.concatenate(
            [new.reshape(NB_LAT, T_LAT, w), old[:, l].reshape(NB_LAT, PAST_LEN, w).astype(BF16)], axis=1)
        x = _attn(x, qa, qc, ob, cat(ka, cache_a_k, 128), cat(va, cache_a_v, 128), cat(kc, cache_c_k, 384),
                  cat(vc, cache_c_v, 384), prm, consts["seg96_384"], l=l, lam_init=lam_inits[l], ctx=False)

        if l == DEPTH - 1:
            y_prompt, y_sample = ffn((x,), l=l, s=1, last=True)
        else:
            x = ffn((x,), l=l, s=1)

    return (y_prompt.reshape(NB_CTX, T_CTX, D_MODEL), y_sample.reshape(NB_LAT, T_LAT, D_MODEL),
            caches[0].reshape(NB_CTX, DEPTH, T_CTX, A_KV, A_D), caches[1].reshape(NB_CTX, DEPTH, T_CTX, A_KV, A_D),
            caches[2].reshape(NB_CTX, DEPTH, T_CTX, C_H, 2 * C_DQ), caches[3].reshape(NB_CTX, DEPTH, T_CTX, C_H, C_DV),
            jnp.swapaxes(states, 3, 4).reshape(NB_CTX, DEPTH, 2, B_H, B_DK, B_DV))
```

```python
import functools
import math

import numpy as np
import jax
import jax.numpy as jnp
from jax import lax
from jax.experimental import pallas as pl
from jax.experimental.pallas import tpu as pltpu

F32 = jnp.float32
BF16 = jnp.bfloat16

D_MODEL = 1024
D_FF = 2816
DEPTH = 4
N_MOD = 9
EPS = 1e-6
ROPE_BASE = 10000.0
GRID_W = 64
LOG2E = math.log2(math.e)

NB_CTX, T_CTX = 32, 256
NB_LAT, T_LAT = 2, 2048
PAST_LEN = 512
N_CTX = NB_CTX * T_CTX
N_LAT = NB_LAT * T_LAT
N_TOK = N_CTX + N_LAT

A_KV, A_G, A_D = 2, 3, 64
A_W = A_KV * A_G * A_D
B_H, B_DK, B_DV = 4, 32, 64
B_KW = B_H * B_DK
B_VW = B_H * B_DV
B_RANK = 16
B_TAU = 16.0
B_CHUNK = 64
C_H, C_DQ, C_DV = 4, 48, 96
C_W = C_H * C_DV

P_AQ, P_AK, P_AV = 0, 384, 512
P_BQ, P_BK, P_BV, P_BR = 640, 768, 896, 1152
P_CQ, P_CK, P_CV = 1408, 1792, 2176
P_BG = 2560
P_W = 2688

TM_FFN = 1024
TF_FFN = 256
TM_PROJ = 256
TQ_ATT_LAT = 512
ATT_NBB_CTX = 4
TN_ADA = 1152
GLA_NBB_LAT = 2
A_STACK_CTX, A_STACK_LAT = 6, 2
C_STACK_CTX, C_STACK_LAT = 4, 2
VMEM_LIMIT = 56 * 1024 * 1024


def _cparams(sem):
    return pltpu.CompilerParams(dimension_semantics=sem, vmem_limit_bytes=VMEM_LIMIT)


def _dot(a, b):
    return jnp.dot(a, b, preferred_element_type=F32)


def _dot_nt(a, b):
    return lax.dot_general(a, b, (((1,), (1,)), ((), ())), preferred_element_type=F32)


def _dot_tn(a, b):
    return lax.dot_general(a, b, (((0,), (0,)), ((), ())), preferred_element_type=F32)


def _split_dot(x, m):
    hi = x.astype(BF16)
    lo = (x - hi.astype(F32)).astype(BF16)
    return _dot(hi, m) + _dot(lo, m)


def _split_dot_l(m, x):
    hi = x.astype(BF16)
    lo = (x - hi.astype(F32)).astype(BF16)
    return _dot(m, hi) + _dot(m, lo)


def _seg_rms(x, seg, n):
    ms = _dot((x * x).astype(BF16), seg) * (1.0 / n)
    return x * lax.rsqrt(ms + EPS)


def _silu(x):
    return x * jax.nn.sigmoid(x)


def _norm_mod(x, gn, sc, sh):
    ms = jnp.mean(x * x, axis=-1, keepdims=True)
    return ((x * lax.rsqrt(ms + EPS)) * gn * (1.0 + sc) + sh).astype(BF16)


def _adaln_body(cond_ref, w_ref, b_ref, o_ref):
    c = cond_ref[...]
    sc = _silu(c).astype(BF16)
    o_ref[0] = _dot(sc, w_ref[0].astype(BF16)) + b_ref[0]


def _adaln(cond8, w_ada, b_ada):
    nj = (N_MOD * D_MODEL) // TN_ADA
    return pl.pallas_call(
        _adaln_body,
        out_shape=jax.ShapeDtypeStruct((DEPTH, 8, N_MOD * D_MODEL), F32),
        grid=(DEPTH, nj),
        in_specs=[
            pl.BlockSpec((8, D_MODEL), lambda l, j: (0, 0)),
            pl.BlockSpec((1, D_MODEL, TN_ADA), lambda l, j: (l, 0, j)),
            pl.BlockSpec((1, 1, TN_ADA), lambda l, j: (l, 0, j)),
        ],
        out_specs=pl.BlockSpec((1, 8, TN_ADA), lambda l, j: (l, 0, j)),
        compiler_params=_cparams(("parallel", "parallel")),
        name="adaln",
    )(cond8, w_ada, b_ada.reshape(DEPTH, 1, N_MOD * D_MODEL))


FFN_TILES_CTX = N_CTX // TM_FFN
FFN_TILES_PER_LAT = T_LAT // TM_FFN


def _ffn_body(*refs, first, last):
    it = iter(refs)
    x_refs = (next(it), next(it)) if first else (next(it),)
    sh_ref, sc_ref, gt_ref, gn_ref, wg_ref, wu_ref, wd_ref = (next(it) for _ in range(7))
    o_refs = (next(it), next(it)) if last else (next(it),)
    h_scr, acc_scr = next(it), next(it)

    i = pl.program_id(0)
    j = pl.program_id(1)
    is_ctx = i < FFN_TILES_CTX
    r = jnp.where(is_ctx, 0, 1 + jnp.maximum(i - FFN_TILES_CTX, 0) // FFN_TILES_PER_LAT)

    def on_tile(pred, n_variants, fn):
        if n_variants == 1:
            pl.when(pred)(lambda: fn(0))
        else:
            pl.when(pred & is_ctx)(lambda: fn(0))
            pl.when(pred & jnp.logical_not(is_ctx))(lambda: fn(1))

    def prologue(k):
        h_scr[...] = _norm_mod(x_refs[k][...], gn_ref[...], sc_ref[pl.ds(r, 1), :], sh_ref[pl.ds(r, 1), :])
        acc_scr[...] = jnp.zeros_like(acc_scr)

    on_tile(j == 0, len(x_refs), prologue)

    h = h_scr[...]
    g = _dot(h, wg_ref[...].astype(BF16))
    u = _dot(h, wu_ref[...].astype(BF16))
    a = (_silu(g) * u).astype(BF16)
    acc_scr[...] += _dot(a, wd_ref[...].astype(BF16))

    def epilogue(k):
        x_ref = x_refs[k if first else 0]
        o_ref = o_refs[k if last else 0]
        o_ref[...] = x_ref[...] + 0.5 * gt_ref[pl.ds(r, 1), :] * acc_scr[...]

    on_tile(j == pl.num_programs(1) - 1, max(len(x_refs), len(o_refs)), epilogue)


def _ffn(xs, mod, gn4, wg, wu, wd, *, l, s, first=False, last=False):
    ni = N_TOK // TM_FFN
    nj = D_FF // TF_FFN
    k0 = 6 * s
    gi = 2 * s
    tc = FFN_TILES_CTX
    split_specs = [pl.BlockSpec((TM_FFN, D_MODEL), lambda i, j: (jnp.minimum(i, tc - 1), 0)),
                   pl.BlockSpec((TM_FFN, D_MODEL), lambda i, j: (jnp.maximum(i - tc, 0), 0))]
    one_spec = [pl.BlockSpec((TM_FFN, D_MODEL), lambda i, j: (i, 0))]
    mspec = lambda k: pl.BlockSpec((None, 8, D_MODEL), lambda i, j: (l, 0, k))
    in_specs = (split_specs if first else one_spec) + [
        mspec(k0), mspec(k0 + 1), mspec(k0 + 2),
        pl.BlockSpec((None, None, 1, D_MODEL), lambda i, j: (l, gi, 0, 0)),
        pl.BlockSpec((None, None, D_MODEL, TF_FFN), lambda i, j: (l, s, 0, j)),
        pl.BlockSpec((None, None, D_MODEL, TF_FFN), lambda i, j: (l, s, 0, j)),
        pl.BlockSpec((None, None, TF_FFN, D_MODEL), lambda i, j: (l, s, j, 0)),
    ]
    if last:
        out_shape = [jax.ShapeDtypeStruct((N_CTX, D_MODEL), F32), jax.ShapeDtypeStruct((N_LAT, D_MODEL), F32)]
        out_specs = split_specs
    else:
        out_shape = jax.ShapeDtypeStruct((N_TOK, D_MODEL), F32)
        out_specs = one_spec[0]
    return pl.pallas_call(
        functools.partial(_ffn_body, first=first, last=last),
        out_shape=out_shape,
        grid=(ni, nj),
        in_specs=in_specs,
        out_specs=out_specs,
        scratch_shapes=[pltpu.VMEM((TM_FFN, D_MODEL), BF16), pltpu.VMEM((TM_FFN, D_MODEL), F32)],
        compiler_params=_cparams(("arbitrary", "arbitrary")),
        name="ffn",
    )(*xs, mod, mod, mod, gn4, wg, wu, wd)


def _proj_body(*refs, rope, ctx, tiles_per_batch):
    it = iter(refs)
    x_ref, sh_ref, sc_ref, gn_ref, w_ref, wup_ref, bgla_ref = (next(it) for _ in range(7))
    gaq_ref, gak_ref, gcq_ref, gck_ref = (next(it) for _ in range(4))
    s64a_ref, s64k_ref, s48_ref = (next(it) for _ in range(3))
    if rope:
        cosa_ref, sina_ref, cosc_ref, sinc_ref, ra_ref, rak_ref, rc_ref = (next(it) for _ in range(7))
    if ctx:
        for _ in range(4):
            next(it)
    qa_o, ka_o, va_o, qc_o, kc_o, vc_o, gq_o, gk_o, gv_o, gla_o, gr_o = (next(it) for _ in range(11))
    if ctx:
        ka32_o, va32_o, kc32_o, vc32_o = (next(it) for _ in range(4))

    i = pl.program_id(0)
    r = 0 if ctx else 1 + i // tiles_per_batch

    h = _norm_mod(x_ref[...], gn_ref[...], sc_ref[pl.ds(r, 1), :], sh_ref[pl.ds(r, 1), :])
    p = _dot(h, w_ref[...])

    aq = _seg_rms(p[:, P_AQ:P_AQ + A_W], s64a_ref[...], A_D) * gaq_ref[...]
    ak = _seg_rms(p[:, P_AK:P_AK + 128], s64k_ref[...], A_D) * gak_ref[...]
    av = p[:, P_AV:P_AV + 128]
    cq = _seg_rms(p[:, P_CQ:P_CQ + 384], s48_ref[...], C_DQ) * gcq_ref[...]
    ck = _seg_rms(p[:, P_CK:P_CK + 384], s48_ref[...], C_DQ) * gck_ref[...]
    cv = p[:, P_CV:P_CV + 384]
    if ctx:
        ka32_o[...] = ak
        va32_o[...] = av
        kc32_o[...] = ck
        vc32_o[...] = cv
    if rope:
        cosa = cosa_ref[...]
        sina = sina_ref[...]
        cosa3 = jnp.concatenate([cosa, cosa, cosa], axis=1)
        sina3 = jnp.concatenate([sina, sina, sina], axis=1)
        aq = aq * cosa3 + _dot(aq.astype(BF16), ra_ref[...]) * sina3
        ak = ak * cosa + _dot(ak.astype(BF16), rak_ref[...]) * sina
        cosc = cosc_ref[...]
        sinc = sinc_ref[...]
        cq = cq * cosc + _dot(cq.astype(BF16), rc_ref[...]) * sinc
        ck = ck * cosc + _dot(ck.astype(BF16), rc_ref[...]) * sinc
    qa_o[...] = (aq * (A_D ** -0.5 * LOG2E)).astype(BF16)
    ka_o[...] = ak.astype(BF16)
    va_o[...] = av.astype(BF16)
    qc_o[...] = (cq * (C_DQ ** -0.5 * LOG2E)).astype(BF16)
    kc_o[...] = ck.astype(BF16)
    vc_o[...] = cv.astype(BF16)

    gq_o[...] = p[:, P_BQ:P_BQ + B_KW] * (B_DK ** -0.5)
    gk_o[...] = p[:, P_BK:P_BK + B_KW]
    gv_o[...] = p[:, P_BV:P_BV + B_VW]
    gr_o[...] = _silu(p[:, P_BR:P_BR + B_VW])
    z = _dot(p[:, P_BG:P_BG + 128].astype(BF16), wup_ref[...]) + bgla_ref[...]
    log_sig = jnp.minimum(z, 0.0) - jnp.log1p(jnp.exp(-jnp.abs(z)))
    gla_o[...] = log_sig * (1.0 / B_TAU)


def _proj(x, prm, consts, rope_tabs, caches, *, l, ctx):
    n = N_CTX if ctx else N_LAT
    off = 0 if ctx else N_CTX // TM_PROJ
    tiles_per_batch = (T_CTX if ctx else T_LAT) // TM_PROJ
    rope = not ctx
    full = lambda shape: pl.BlockSpec(shape, lambda i: (0,) * len(shape))
    lay = lambda shape: pl.BlockSpec((None,) + shape, lambda i: (l,) + (0,) * len(shape))
    in_specs = [
        pl.BlockSpec((TM_PROJ, D_MODEL), lambda i: (i + off, 0)),
        pl.BlockSpec((None, 8, D_MODEL), lambda i: (l, 0, 3)),
        pl.BlockSpec((None, 8, D_MODEL), lambda i: (l, 0, 4)),
        pl.BlockSpec((None, None, 1, D_MODEL), lambda i: (l, 1, 0, 0)),
        lay((D_MODEL, P_W)),
        lay((128, 2 * B_KW)),
        lay((1, 2 * B_KW)),
        lay((1, 384)), lay((1, 128)), lay((1, 384)), lay((1, 384)),
        full((384, 384)), full((128, 128)), full((384, 384)),
    ]
    args = [x, prm["mod"], prm["mod"], prm["gn4"], prm["w_p"], prm["wup"], prm["bgla"],
            prm["gaq"], prm["gak"], prm["gcq"], prm["gck"],
            consts["seg64_384"], consts["seg64_128"], consts["seg48_384"]]
    if rope:
        tpb = tiles_per_batch
        in_specs += [
            pl.BlockSpec((TM_PROJ, 128), lambda i: (i % tpb, 0)),
            pl.BlockSpec((TM_PROJ, 128), lambda i: (i % tpb, 0)),
            pl.BlockSpec((TM_PROJ, 384), lambda i: (i % tpb, 0)),
            pl.BlockSpec((TM_PROJ, 384), lambda i: (i % tpb, 0)),
            full((384, 384)), full((128, 128)), full((384, 384)),
        ]
        args += [*rope_tabs, consts["rot_a384"], consts["rot_a128"], consts["rot_c384"]]
    widths = [(384, BF16), (128, BF16), (128, BF16), (384, BF16), (384, BF16), (384, BF16),
              (B_KW, F32), (B_KW, F32), (B_VW, F32), (2 * B_KW, F32), (B_VW, F32)]
    out_shape = [jax.ShapeDtypeStruct((n, w), dt) for w, dt in widths]
    out_specs = [pl.BlockSpec((TM_PROJ, w), lambda i: (i, 0)) for w, _ in widths]
    aliases = {}
    if ctx:
        assert TM_PROJ == T_CTX
        for k, cache in enumerate(caches):
            aliases[len(args)] = len(out_shape)
            in_specs.append(pl.BlockSpec(memory_space=pl.ANY))
            args.append(cache)
            out_shape.append(jax.ShapeDtypeStruct(cache.shape, F32))
            out_specs.append(pl.BlockSpec((None, None, T_CTX, cache.shape[-1]), lambda i: (i, l, 0, 0)))
    return pl.pallas_call(
        functools.partial(_proj_body, rope=rope, ctx=ctx, tiles_per_batch=tiles_per_batch),
        out_shape=out_shape,
        grid=(n // TM_PROJ,),
        in_specs=in_specs,
        out_specs=out_specs,
        input_output_aliases=aliases,
        compiler_params=_cparams(("parallel",)),
        name="proj_ctx" if ctx else "proj_lat",
    )(*args)


def _gla_tools(gq_ref, gk_ref, gv_ref, gla_ref):
    C = B_CHUNK
    ri = lax.broadcasted_iota(jnp.int32, (C, C), 0)
    ci = lax.broadcasted_iota(jnp.int32, (C, C), 1)
    cum = (jnp.where(ci <= ri, 1.0, 0.0).astype(BF16), jnp.where(ci >= ri, 1.0, 0.0).astype(BF16))
    rk = lax.broadcasted_iota(jnp.int32, (B_H * C, B_KW), 0) >> 6
    ck = lax.broadcasted_iota(jnp.int32, (B_H * C, B_KW), 1) >> 5
    hm_k = rk == ck
    rv = lax.broadcasted_iota(jnp.int32, (B_H * C, B_VW), 0) >> 6
    cv = lax.broadcasted_iota(jnp.int32, (B_H * C, B_VW), 1) >> 6
    hm_v = rv == cv
    ra = lax.broadcasted_iota(jnp.int32, (C, B_H * C), 0)
    ca = lax.broadcasted_iota(jnp.int32, (C, B_H * C), 1) & (C - 1)
    tri = (ca <= ra, ca >= ra)
    zero16 = jnp.zeros((), BF16)

    def expand_state(st):
        return jnp.where(hm_k, jnp.concatenate([st] * B_H, axis=0), 0.0)

    def chunk(d, start, S):
        rows = pl.ds(start if isinstance(start, int) else pl.multiple_of(start, C), C)
        q = gq_ref[rows, :]
        k = gk_ref[rows, :]
        v = gv_ref[rows, :].astype(BF16)
        la = gla_ref[rows, d * B_KW:(d + 1) * B_KW]
        b = _split_dot_l(cum[d], la)
        tot = b[C - 1:C, :] if d == 0 else b[0:1, :]
        e = b - b[C // 2:C // 2 + 1, :]
        qt = (q * jnp.exp(e)).astype(BF16)
        kt = (k * jnp.exp(-e)).astype(BF16)
        qd = (q * jnp.exp(b)).astype(BF16)
        kd = (k * jnp.exp(tot - b)).astype(BF16)
        kbd = jnp.where(hm_k, jnp.concatenate([kt] * B_H, axis=0), zero16)
        a = _dot_nt(qt, kbd)
        a = jnp.where(tri[d], a, 0.0).astype(BF16)
        vbd = jnp.where(hm_v, jnp.concatenate([v] * B_H, axis=0), zero16)
        o = _dot(a, vbd) + _dot_nt(qd, S.astype(BF16))
        kv = _dot_tn(v, kd)
        S_new = S * jnp.exp(tot) + jnp.where(hm_k, kv, 0.0)
        return rows, o, S_new

    def collapse_state(S):
        Sm = jnp.where(hm_k, S, 0.0)
        acc = Sm[0:C, :]
        for hh in range(1, B_H):
            acc = acc + Sm[hh * C:(hh + 1) * C, :]
        return acc

    return expand_state, chunk, collapse_state


def _gla_gate(o, seg, gg, gr):
    ms = _split_dot(o * o, seg) * (1.0 / B_DV)
    return (o * lax.rsqrt(ms + EPS) * gg * gr).astype(BF16)


def _gla_body(gq_ref, gk_ref, gv_ref, gla_ref, gr_ref, s0_ref, gg_ref, seg_ref, ob_ref, sfin_ref, of_scr, or_scr,
              *, seq, nbb):
    expand_state, chunk, collapse_state = _gla_tools(gq_ref, gk_ref, gv_ref, gla_ref)
    n = seq // B_CHUNK

    def step(i, carry):
        new = []
        for bb in range(nbb):
            rows, o, Sf = chunk(0, bb * seq + i * B_CHUNK, carry[2 * bb])
            of_scr[rows, :] = o
            rows, o, Sb = chunk(1, bb * seq + (n - 1 - i) * B_CHUNK, carry[2 * bb + 1])
            or_scr[rows, :] = o
            new += [Sf, Sb]
        return tuple(new)

    carry = tuple(expand_state(s0_ref[bb, d]) for bb in range(nbb) for d in range(2))
    carry = lax.fori_loop(0, n, step, carry, unroll=2)
    for bb in range(nbb):
        sfin_ref[bb, 0] = collapse_state(carry[2 * bb])
        sfin_ref[bb, 1] = collapse_state(carry[2 * bb + 1])

    RT = 256

    def fin(t, _):
        rows = pl.ds(pl.multiple_of(t * RT, RT), RT)
        ob_ref[rows, :] = _gla_gate(of_scr[rows, :] + or_scr[rows, :], seg_ref[...], gg_ref[...], gr_ref[rows, :])
        return 0

    lax.fori_loop(0, nbb * seq // RT, fin, 0)


def _gla_lat(gq, gk, gv, gla, gr, s0, prm, seg, *, l):
    nb, seq, nbb = NB_LAT, T_LAT, GLA_NBB_LAT
    tok = lambda w: pl.BlockSpec((nbb * seq, w), lambda b: (b, 0))
    st_spec = pl.BlockSpec((nbb, 2, B_DV, B_KW), lambda b: (b, 0, 0, 0))
    return pl.pallas_call(
        functools.partial(_gla_body, seq=seq, nbb=nbb),
        out_shape=[jax.ShapeDtypeStruct((nb * seq, B_VW), BF16), jax.ShapeDtypeStruct((nb, 2, B_DV, B_KW), F32)],
        grid=(nb // nbb,),
        in_specs=[tok(B_KW), tok(B_KW), tok(B_VW), tok(2 * B_KW), tok(B_VW), st_spec,
                  pl.BlockSpec((None, 1, B_VW), lambda b: (l, 0, 0)),
                  pl.BlockSpec((B_VW, B_VW), lambda b: (0, 0))],
        out_specs=[tok(B_VW), st_spec],
        scratch_shapes=[pltpu.VMEM((nbb * seq, B_VW), F32), pltpu.VMEM((nbb * seq, B_VW), F32)],
        compiler_params=_cparams(("parallel",)),
        name="gla_lat",
    )(gq, gk, gv, gla, gr, s0, prm["ggl"], seg)


def _attn_body(*refs, lam_init, ctx, nbb):
    it = iter(refs)
    x_ref, qa_ref, qc_ref = (next(it) for _ in range(3))
    if ctx:
        gq_ref, gk_ref, gv_ref, gla_ref, gr_ref, gg_ref, seg64_ref = (next(it) for _ in range(7))
        ob_ref = None
    else:
        ob_ref = next(it)
    ka_ref, va_ref, kc_ref, vc_ref, wout_ref, gt_ref, lam_ref, gco_ref, seg_ref = (next(it) for _ in range(9))
    if ctx:
        next(it)
    o_ref = next(it)
    if ctx:
        st_ref = next(it)
    mix_scr = next(it)
    if ctx:
        ogla_scr = next(it)

    b = pl.program_id(0)
    r = 0 if ctx else 1 + b
    tq = x_ref.shape[0] // nbb
    a_stack, c_stack = (A_STACK_CTX, C_STACK_CTX) if ctx else (A_STACK_LAT, C_STACK_LAT)
    lane = lax.broadcasted_iota(jnp.int32, (1, 128), 1)
    lane2 = lax.broadcasted_iota(jnp.int32, (1, 256), 1)
    hmask = [(lane >= hh * A_D) & (lane < (hh + 1) * A_D) for hh in range(A_KV)]
    lm = lam_ref[...]
    lam = (jnp.exp(jnp.sum(lm[0:1] * lm[1:2], axis=-1, keepdims=True))
           - jnp.exp(jnp.sum(lm[2:3] * lm[3:4], axis=-1, keepdims=True)) + lam_init)

    def softmax_terms(s):
        e = jnp.exp2(s - jnp.max(s, axis=-1, keepdims=True))
        return e, jnp.sum(e, axis=-1, keepdims=True)

    if ctx:
        _, chunk, collapse_state = _gla_tools(gq_ref, gk_ref, gv_ref, gla_ref)
        n = tq // B_CHUNK
    for bb in range(nbb):
        _attn_mix_rows(bb, tq, qa_ref, qc_ref, ob_ref, ka_ref, va_ref, kc_ref, vc_ref, gco_ref, seg_ref, mix_scr,
                       a_stack, c_stack, hmask, lane2, lam, lam_init, softmax_terms)
        if ctx:
            sf = sb = jnp.zeros((B_VW, B_KW), F32)
            o_f, o_b = [None] * n, [None] * n
            for i in range(n):
                _, o_f[i], sf = chunk(0, bb * tq + i * B_CHUNK, sf)
                _, o_b[n - 1 - i], sb = chunk(1, bb * tq + (n - 1 - i) * B_CHUNK, sb)
            for c in range(n):
                ogla_scr[bb * tq + c * B_CHUNK:bb * tq + (c + 1) * B_CHUNK, :] = o_f[c] + o_b[c]
            st_ref[bb, 0] = collapse_state(sf)
            st_ref[bb, 1] = collapse_state(sb)
    if ctx:
        mix_scr[:, A_W:A_W + B_VW] = _gla_gate(ogla_scr[...], seg64_ref[...], gg_ref[...], gr_ref[...])
    mixed = _dot(mix_scr[...], wout_ref[...])
    o_ref[...] = x_ref[...] + gt_ref[pl.ds(r, 1), :] * mixed


def _attn_mix_rows(bb, tq, qa_ref, qc_ref, ob_ref, ka_ref, va_ref, kc_ref, vc_ref, gco_ref, seg_ref, mix_scr,
                   a_stack, c_stack, hmask, lane2, lam, lam_init, softmax_terms):
    rows = slice(bb * tq, (bb + 1) * tq)
    zero16 = jnp.zeros((), BF16)

    ka = ka_ref[bb]
    va = va_ref[bb]
    maps = [(g, hh) for g in range(A_G) for hh in range(A_KV)]
    acc = [jnp.zeros((tq, 128), F32) for _ in range(A_G)]
    for g0 in range(0, len(maps), a_stack):
        grp = maps[g0:g0 + a_stack]
        qs = jnp.concatenate(
            [jnp.where(hmask[hh], qa_ref[rows, g * 128:(g + 1) * 128], zero16) for g, hh in grp], axis=0)
        e, l = softmax_terms(_dot_nt(qs, ka))
        o = _dot(e.astype(BF16), va) * (1.0 / l)
        for k, (g, hh) in enumerate(grp):
            acc[g] = acc[g] + jnp.where(hmask[hh], o[k * tq:(k + 1) * tq], 0.0)
    for g in range(A_G):
        mix_scr[rows, g * 128:(g + 1) * 128] = acc[g].astype(BF16)

    if ob_ref is not None:
        mix_scr[rows, A_W:A_W + B_VW] = ob_ref[rows, :]

    outs = []
    for win in range(2):
        base = win * 128
        kc = kc_ref[bb, :, base:base + 256]
        vc = vc_ref[bb, :, base:base + 256]
        qc = qc_ref[rows, base:base + 256]
        ow = jnp.zeros((tq, 256), F32)
        cmaps = [(hh, mm) for hh in (2 * win, 2 * win + 1) for mm in range(2)]
        for g0 in range(0, len(cmaps), c_stack):
            grp = cmaps[g0:g0 + c_stack]
            qparts = []
            for hh, mm in grp:
                lo = (hh * 2 + mm) * C_DQ - base
                qparts.append(jnp.where((lane2 >= lo) & (lane2 < lo + C_DQ), qc, zero16))
            e, l = softmax_terms(_dot_nt(jnp.concatenate(qparts, axis=0), kc))
            ws, invs = [], []
            for k in range(0, len(grp), 2):
                l0 = l[k * tq:(k + 1) * tq]
                l1 = l[(k + 1) * tq:(k + 2) * tq]
                ws.append((e[k * tq:(k + 1) * tq] - (lam * l0 / l1) * e[(k + 1) * tq:(k + 2) * tq]).astype(BF16))
                invs.append(1.0 / l0)
            rr = _dot(jnp.concatenate(ws, axis=0) if len(ws) > 1 else ws[0], vc)
            for k in range(len(ws)):
                hh = grp[2 * k][0]
                vlo = hh * C_DV - base
                vm = (lane2 >= vlo) & (lane2 < vlo + C_DV)
                ow = ow + jnp.where(vm, rr[k * tq:(k + 1) * tq] * invs[k], 0.0)
        outs.append(ow)
    oc = jnp.concatenate([outs[0][:, :128], outs[0][:, 128:] + outs[1][:, :128], outs[1][:, 128:]], axis=1)
    ms = _split_dot(oc * oc, seg_ref[...]) * (1.0 / C_DV)
    oc = oc * lax.rsqrt(ms + EPS) * gco_ref[...] * (1.0 - lam_init)
    mix_scr[rows, A_W + B_VW:] = oc.astype(BF16)


def _attn(x, qa, qc, mix_b, ka, va, kc, vc, prm, seg96, seg64, states, *, l, lam_init, ctx):
    nb, seq, nbb, tq = (NB_CTX, T_CTX, ATT_NBB_CTX, T_CTX) if ctx else (NB_LAT, T_LAT, 1, TQ_ATT_LAT)
    nq = seq // tq
    rb = nbb * tq
    off = 0 if ctx else N_CTX // rb
    tk = ka.shape[1]
    qspec = lambda w: pl.BlockSpec((rb, w), lambda b, q: (b * nq + q, 0))
    kspec = lambda w: pl.BlockSpec((nbb, tk, w), lambda b, q: (b, 0, 0),
                                   pipeline_mode=None if ctx else pl.Buffered(1))
    lay = lambda shape: pl.BlockSpec((None,) + shape, lambda b, q: (l,) + (0,) * len(shape),
                                     pipeline_mode=pl.Buffered(1))
    xspec = pl.BlockSpec((rb, D_MODEL), lambda b, q: (off + b * nq + q, 0))
    in_specs = [xspec, qspec(384), qspec(384)]
    args = [x, qa, qc]
    if ctx:
        in_specs += [qspec(B_KW), qspec(B_KW), qspec(B_VW), qspec(2 * B_KW), qspec(B_VW), lay((1, B_VW)),
                     pl.BlockSpec((B_VW, B_VW), lambda b, q: (0, 0))]
        args += [*mix_b, prm["ggl"], seg64]
    else:
        in_specs.append(qspec(B_VW))
        args.append(mix_b)
    in_specs += [kspec(128), kspec(128), kspec(384), kspec(384),
                 lay((D_MODEL, D_MODEL)),
                 pl.BlockSpec((None, 8, D_MODEL), lambda b, q: (l, 0, 5)),
                 lay((4, C_DQ)), lay((1, 384)),
                 pl.BlockSpec((384, 384), lambda b, q: (0, 0))]
    args += [ka, va, kc, vc, prm["w_o"], prm["mod"], prm["lam_c"], prm["gco"], seg96]
    out_shape = [jax.ShapeDtypeStruct((N_TOK, D_MODEL), F32)]
    out_specs = [xspec]
    scratch = [pltpu.VMEM((rb, D_MODEL), BF16)]
    aliases = {0: 0}
    if ctx:
        aliases[len(args)] = 1
        in_specs.append(pl.BlockSpec(memory_space=pl.ANY))
        args.append(states)
        out_shape.append(jax.ShapeDtypeStruct(states.shape, F32))
        out_specs.append(pl.BlockSpec((nbb, None, 2, B_DV, B_KW), lambda b, q: (b, l, 0, 0, 0)))
        scratch.append(pltpu.VMEM((rb, B_VW), F32))
    return pl.pallas_call(
        functools.partial(_attn_body, lam_init=lam_init, ctx=ctx, nbb=nbb),
        out_shape=out_shape,
        grid=(nb // nbb, nq),
        in_specs=in_specs,
        out_specs=out_specs,
        scratch_shapes=scratch,
        input_output_aliases=aliases,
        compiler_params=_cparams(("parallel", "arbitrary")),
        name="attn_ctx" if ctx else "attn_lat",
    )(*args)


def _block_ones(width, seg):
    idx = np.arange(width) // seg
    return jnp.asarray((idx[:, None] == idx[None, :]).astype(np.float32), dtype=BF16)


def _rot_matrix(width, half):
    m = np.zeros((width, width), np.float32)
    for j in range(width):
        if (j % (2 * half)) < half:
            m[j + half, j] = -1.0
        else:
            m[j - half, j] = 1.0
    return jnp.asarray(m, dtype=BF16)


def _rope_tables():
    t = np.arange(T_LAT)
    row = (t // GRID_W).astype(np.float32)
    col = (t % GRID_W).astype(np.float32)

    def tab(head_dim, n_heads):
        m = head_dim // 4
        freqs = ROPE_BASE ** (-jnp.arange(m, dtype=F32) / m)
        ang_r = jnp.asarray(row)[:, None] * freqs[None, :]
        ang_c = jnp.asarray(col)[:, None] * freqs[None, :]
        cs = jnp.concatenate([jnp.cos(ang_r)] * 2 + [jnp.cos(ang_c)] * 2, axis=1)
        sn = jnp.concatenate([jnp.sin(ang_r)] * 2 + [jnp.sin(ang_c)] * 2, axis=1)
        return jnp.tile(cs, (1, n_heads)), jnp.tile(sn, (1, n_heads))

    cosa, sina = tab(A_D, 2)
    cosc, sinc = tab(C_DQ, 2 * C_H)
    return cosa, sina, cosc, sinc


def kernel(x_prompt, x_sample, c, cache_a_k, cache_a_v, cache_c_k, cache_c_v, state_gla, c_ctx, w_ada, b_ada,
           g_norm, w_ffn_gate, w_ffn_up, w_ffn_down, w_in, g_a_q, g_a_k, w_gla_up, b_gla, g_gla, g_c_q, g_c_k,
           lam_c, g_c_out, w_out):
    lam_inits = [0.8 - 0.6 * math.exp(-0.3 * l) for l in range(DEPTH)]

    consts = {
        "seg64_384": _block_ones(384, 64), "seg64_128": _block_ones(128, 64),
        "seg48_384": _block_ones(384, 48), "seg96_384": _block_ones(384, 96),
        "seg64_256": _block_ones(256, 64),
        "rot_a384": _rot_matrix(384, 16), "rot_a128": _rot_matrix(128, 16), "rot_c384": _rot_matrix(384, 12),
    }
    rope_tabs = _rope_tables()

    wq = w_in[:, :, 0:384].reshape(DEPTH, D_MODEL, A_KV, A_G, A_D).transpose(0, 1, 3, 2, 4).reshape(DEPTH, D_MODEL, 384)
    seg = lambda a, b: w_in[:, :, a:b]
    w_p = jnp.concatenate([
        wq, seg(384, 512), seg(512, 640),
        seg(640, 768), seg(768, 896), seg(896, 1152),
        seg(1184, 1440),
        seg(1440, 1824), seg(1824, 2208), seg(2208, 2592),
        seg(1152, 1184), jnp.zeros((DEPTH, D_MODEL, 96), F32),
    ], axis=2).astype(BF16)
    wo_a = w_out[:, 0:384].reshape(DEPTH, A_KV, A_G, A_D, D_MODEL).transpose(0, 2, 1, 3, 4).reshape(DEPTH, 384, D_MODEL)
    wup = jnp.zeros((DEPTH, 128, 2 * B_KW), F32)
    wup = wup.at[:, 0:B_RANK, 0:B_KW].set(w_gla_up[:, 0]).at[:, B_RANK:2 * B_RANK, B_KW:].set(w_gla_up[:, 1])
    cond8 = jnp.zeros((8, D_MODEL), F32).at[0].set(c_ctx).at[1:3].set(c)
    prm = {
        "mod": _adaln(cond8, w_ada, b_ada),
        "gn4": g_norm.reshape(DEPTH, 3, 1, D_MODEL),
        "w_p": w_p,
        "w_o": jnp.concatenate([wo_a, w_out[:, 384:]], axis=1).astype(BF16),
        "wup": wup.astype(BF16),
        "bgla": b_gla.reshape(DEPTH, 1, 2 * B_KW),
        "gaq": jnp.tile(g_a_q, (1, 6)).reshape(DEPTH, 1, 384),
        "gak": jnp.tile(g_a_k, (1, 2)).reshape(DEPTH, 1, 128),
        "gcq": jnp.tile(g_c_q.reshape(DEPTH, 96), (1, 4)).reshape(DEPTH, 1, 384),
        "gck": jnp.tile(g_c_k.reshape(DEPTH, 96), (1, 4)).reshape(DEPTH, 1, 384),
        "gco": jnp.tile(g_c_out, (1, 4)).reshape(DEPTH, 1, 384),
        "ggl": jnp.tile(g_gla, (1, 4)).reshape(DEPTH, 1, 256),
        "lam_c": lam_c,
    }

    caches = [jnp.zeros((NB_CTX, DEPTH, T_CTX, w), F32) for w in (128, 128, 384, 384)]
    states = jnp.zeros((NB_CTX, DEPTH, 2, B_DV, B_KW), F32)
    ffn = functools.partial(_ffn, mod=prm["mod"], gn4=prm["gn4"], wg=w_ffn_gate, wu=w_ffn_up, wd=w_ffn_down)
    x = None
    for l in range(DEPTH):
        if l == 0:
            x = ffn((x_prompt.reshape(N_CTX, D_MODEL), x_sample.reshape(N_LAT, D_MODEL)), l=0, s=0, first=True)
        else:
            x = ffn((x,), l=l, s=0)

        outs = _proj(x, prm, consts, None, caches, l=l, ctx=True)
        qa, ka, va, qc, kc, vc, gq, gk, gv, gla, gr = outs[:11]
        caches = list(outs[11:])
        r3 = lambda a: a.reshape(NB_CTX, T_CTX, a.shape[-1])
        x, states = _attn(x, qa, qc, (gq, gk, gv, gla, gr), r3(ka), r3(va), r3(kc), r3(vc), prm, consts["seg96_384"],
                          consts["seg64_256"], states, l=l, lam_init=lam_inits[l], ctx=True)

        qa, ka, va, qc, kc, vc, gq, gk, gv, gla, gr = _proj(x, prm, consts, rope_tabs, None, l=l, ctx=False)
        s0 = jnp.swapaxes(state_gla[:, l].reshape(NB_LAT, 2, B_KW, B_DV), 2, 3)
        ob, _ = _gla_lat(gq, gk, gv, gla, gr, s0, prm, consts["seg64_256"], l=l)
        cat = lambda new, old, w: jnp.concatenate(
            [new.reshape(NB_LAT, T_LAT, w), old[:, l].reshape(NB_LAT, PAST_LEN, w).astype(BF16)], axis=1)
        x, = _attn(x, qa, qc, ob, cat(ka, cache_a_k, 128), cat(va, cache_a_v, 128), cat(kc, cache_c_k, 384),
                   cat(vc, cache_c_v, 384), prm, consts["seg96_384"], None, None, l=l, lam_init=lam_inits[l], ctx=False)

        if l == DEPTH - 1:
            y_prompt, y_sample = ffn((x,), l=l, s=1, last=True)
        else:
            x = ffn((x,), l=l, s=1)

    return (y_prompt.reshape(NB_CTX, T_CTX, D_MODEL), y_sample.reshape(NB_LAT, T_LAT, D_MODEL),
            caches[0].reshape(NB_CTX, DEPTH, T_CTX, A_KV, A_D), caches[1].reshape(NB_CTX, DEPTH, T_CTX, A_KV, A_D),
            caches[2].reshape(NB_CTX, DEPTH, T_CTX, C_H, 2 * C_DQ), caches[3].reshape(NB_CTX, DEPTH, T_CTX, C_H, C_DV),
            jnp.swapaxes(states, 3, 4).reshape(NB_CTX, DEPTH, 2, B_H, B_DK, B_DV))
```

```python
import functools
import math

import numpy as np
import jax
import jax.numpy as jnp
from jax import lax
from jax.experimental import pallas as pl
from jax.experimental.pallas import tpu as pltpu

F32 = jnp.float32
BF16 = jnp.bfloat16

D_MODEL = 1024
D_FF = 2816
DEPTH = 4
N_MOD = 9
EPS = 1e-6
ROPE_BASE = 10000.0
GRID_W = 64
LOG2E = math.log2(math.e)

NB_CTX, T_CTX = 32, 256
NB_LAT, T_LAT = 2, 2048
PAST_LEN = 512
N_CTX = NB_CTX * T_CTX
N_LAT = NB_LAT * T_LAT
N_TOK = N_CTX + N_LAT

A_KV, A_G, A_D = 2, 3, 64
A_W = A_KV * A_G * A_D
B_H, B_DK, B_DV = 4, 32, 64
B_KW = B_H * B_DK
B_VW = B_H * B_DV
B_RANK = 16
B_TAU = 16.0
B_CHUNK = 64
C_H, C_DQ, C_DV = 4, 48, 96
C_W = C_H * C_DV

P_AQ, P_AK, P_AV = 0, 384, 512
P_BQ, P_BK, P_BV, P_BR = 640, 768, 896, 1152
P_CQ, P_CK, P_CV = 1408, 1792, 2176
P_BG = 2560
P_W = 2688

TM_FFN = 1024
TF_FFN = 256
TM_PROJ = 256
TQ_ATT_LAT = 512
ATT_NBB_CTX = 4
TN_ADA = 1152
GLA_NBB_CTX = 4
GLA_NBB_LAT = 2
A_STACK_CTX, A_STACK_LAT = 6, 2
C_STACK_CTX, C_STACK_LAT = 4, 2
VMEM_LIMIT = 56 * 1024 * 1024


def _cparams(sem):
    return pltpu.CompilerParams(dimension_semantics=sem, vmem_limit_bytes=VMEM_LIMIT)


def _dot(a, b):
    return jnp.dot(a, b, preferred_element_type=F32)


def _dot_nt(a, b):
    return lax.dot_general(a, b, (((1,), (1,)), ((), ())), preferred_element_type=F32)


def _dot_tn(a, b):
    return lax.dot_general(a, b, (((0,), (0,)), ((), ())), preferred_element_type=F32)


def _split_dot(x, m):
    hi = x.astype(BF16)
    lo = (x - hi.astype(F32)).astype(BF16)
    return _dot(hi, m) + _dot(lo, m)


def _split_dot_l(m, x):
    hi = x.astype(BF16)
    lo = (x - hi.astype(F32)).astype(BF16)
    return _dot(m, hi) + _dot(m, lo)


def _seg_rms(x, seg, n):
    ms = _dot((x * x).astype(BF16), seg) * (1.0 / n)
    return x * lax.rsqrt(ms + EPS)


def _silu(x):
    return x * jax.nn.sigmoid(x)


def _norm_mod(x, gn, sc, sh):
    ms = jnp.mean(x * x, axis=-1, keepdims=True)
    return ((x * lax.rsqrt(ms + EPS)) * (gn * (1.0 + sc)) + sh).astype(BF16)


def _adaln_body(cond_ref, w_ref, b_ref, o_ref):
    c = cond_ref[...]
    sc = _silu(c).astype(BF16)
    o_ref[0] = _dot(sc, w_ref[0].astype(BF16)) + b_ref[0]


def _adaln(cond8, w_ada, b_ada):
    nj = (N_MOD * D_MODEL) // TN_ADA
    return pl.pallas_call(
        _adaln_body,
        out_shape=jax.ShapeDtypeStruct((DEPTH, 8, N_MOD * D_MODEL), F32),
        grid=(DEPTH, nj),
        in_specs=[
            pl.BlockSpec((8, D_MODEL), lambda l, j: (0, 0)),
            pl.BlockSpec((1, D_MODEL, TN_ADA), lambda l, j: (l, 0, j)),
            pl.BlockSpec((1, 1, TN_ADA), lambda l, j: (l, 0, j)),
        ],
        out_specs=pl.BlockSpec((1, 8, TN_ADA), lambda l, j: (l, 0, j)),
        compiler_params=_cparams(("parallel", "parallel")),
        name="adaln",
    )(cond8, w_ada, b_ada.reshape(DEPTH, 1, N_MOD * D_MODEL))


FFN_TILES_CTX = N_CTX // TM_FFN
FFN_TILES_PER_LAT = T_LAT // TM_FFN


def _ffn_body(*refs, first, last):
    it = iter(refs)
    x_refs = (next(it), next(it)) if first else (next(it),)
    sh_ref, sc_ref, gt_ref, gn_ref, wg_ref, wu_ref, wd_ref = (next(it) for _ in range(7))
    o_refs = (next(it), next(it)) if last else (next(it),)
    h_scr, acc_scr = next(it), next(it)

    i = pl.program_id(0)
    j = pl.program_id(1)
    is_ctx = i < FFN_TILES_CTX
    r = jnp.where(is_ctx, 0, 1 + jnp.maximum(i - FFN_TILES_CTX, 0) // FFN_TILES_PER_LAT)

    def on_tile(pred, n_variants, fn):
        if n_variants == 1:
            pl.when(pred)(lambda: fn(0))
        else:
            pl.when(pred & is_ctx)(lambda: fn(0))
            pl.when(pred & jnp.logical_not(is_ctx))(lambda: fn(1))

    def prologue(k):
        h_scr[...] = _norm_mod(x_refs[k][...], gn_ref[...], sc_ref[pl.ds(r, 1), :], sh_ref[pl.ds(r, 1), :])
        acc_scr[...] = jnp.zeros_like(acc_scr)

    on_tile(j == 0, len(x_refs), prologue)

    h = h_scr[...]
    g = _dot(h, wg_ref[...].astype(BF16))
    u = _dot(h, wu_ref[...].astype(BF16))
    a = (_silu(g) * u).astype(BF16)
    acc_scr[...] += _dot(a, wd_ref[...].astype(BF16))

    def epilogue(k):
        x_ref = x_refs[k if first else 0]
        o_ref = o_refs[k if last else 0]
        o_ref[...] = x_ref[...] + 0.5 * gt_ref[pl.ds(r, 1), :] * acc_scr[...]

    on_tile(j == pl.num_programs(1) - 1, max(len(x_refs), len(o_refs)), epilogue)


def _ffn(xs, mod, gn4, wg, wu, wd, *, l, s, first=False, last=False):
    ni = N_TOK // TM_FFN
    nj = D_FF // TF_FFN
    k0 = 6 * s
    gi = 2 * s
    tc = FFN_TILES_CTX
    split_specs = [pl.BlockSpec((TM_FFN, D_MODEL), lambda i, j: (jnp.minimum(i, tc - 1), 0)),
                   pl.BlockSpec((TM_FFN, D_MODEL), lambda i, j: (jnp.maximum(i - tc, 0), 0))]
    one_spec = [pl.BlockSpec((TM_FFN, D_MODEL), lambda i, j: (i, 0))]
    mspec = lambda k: pl.BlockSpec((None, 8, D_MODEL), lambda i, j: (l, 0, k))
    in_specs = (split_specs if first else one_spec) + [
        mspec(k0), mspec(k0 + 1), mspec(k0 + 2),
        pl.BlockSpec((None, None, 1, D_MODEL), lambda i, j: (l, gi, 0, 0)),
        pl.BlockSpec((None, None, D_MODEL, TF_FFN), lambda i, j: (l, s, 0, j)),
        pl.BlockSpec((None, None, D_MODEL, TF_FFN), lambda i, j: (l, s, 0, j)),
        pl.BlockSpec((None, None, TF_FFN, D_MODEL), lambda i, j: (l, s, j, 0)),
    ]
    if last:
        out_shape = [jax.ShapeDtypeStruct((N_CTX, D_MODEL), F32), jax.ShapeDtypeStruct((N_LAT, D_MODEL), F32)]
        out_specs = split_specs
    else:
        out_shape = jax.ShapeDtypeStruct((N_TOK, D_MODEL), F32)
        out_specs = one_spec[0]
    return pl.pallas_call(
        functools.partial(_ffn_body, first=first, last=last),
        out_shape=out_shape,
        grid=(ni, nj),
        in_specs=in_specs,
        out_specs=out_specs,
        scratch_shapes=[pltpu.VMEM((TM_FFN, D_MODEL), BF16), pltpu.VMEM((TM_FFN, D_MODEL), F32)],
        compiler_params=_cparams(("arbitrary", "arbitrary")),
        name="ffn",
    )(*xs, mod, mod, mod, gn4, wg, wu, wd)


def _proj_body(*refs, rope, ctx, tiles_per_batch, n_alias):
    it = iter(refs)
    x_ref, sh_ref, sc_ref, gn_ref, w_ref, wup_ref, bgla_ref = (next(it) for _ in range(7))
    gaq_ref, gak_ref, gcq_ref, gck_ref = (next(it) for _ in range(4))
    s64a_ref, s64k_ref, s48_ref = (next(it) for _ in range(3))
    if rope:
        cosa_ref, sina_ref, cosc_ref, sinc_ref, ra_ref, rak_ref, rc_ref = (next(it) for _ in range(7))
    for _ in range(n_alias):
        next(it)
    qa_o, ka_o, va_o, qc_o, kc_o, vc_o, gq_o, gk_o, gv_o, gla_o, gr_o = (next(it) for _ in range(11))
    if ctx:
        ka32_o, va32_o, kc32_o, vc32_o = (next(it) for _ in range(4))

    i = pl.program_id(0)
    r = 0 if ctx else 1 + i // tiles_per_batch

    h = _norm_mod(x_ref[...], gn_ref[...], sc_ref[pl.ds(r, 1), :], sh_ref[pl.ds(r, 1), :])
    p = _dot(h, w_ref[...])

    aq = _seg_rms(p[:, P_AQ:P_AQ + A_W], s64a_ref[...], A_D) * gaq_ref[...]
    ak = _seg_rms(p[:, P_AK:P_AK + 128], s64k_ref[...], A_D) * gak_ref[...]
    av = p[:, P_AV:P_AV + 128]
    cq = _seg_rms(p[:, P_CQ:P_CQ + 384], s48_ref[...], C_DQ) * gcq_ref[...]
    ck = _seg_rms(p[:, P_CK:P_CK + 384], s48_ref[...], C_DQ) * gck_ref[...]
    cv = p[:, P_CV:P_CV + 384]
    if ctx:
        ka32_o[...] = ak
        va32_o[...] = av
        kc32_o[...] = ck
        vc32_o[...] = cv
    if rope:
        cosa = cosa_ref[...]
        sina = sina_ref[...]
        cosa3 = jnp.concatenate([cosa, cosa, cosa], axis=1)
        sina3 = jnp.concatenate([sina, sina, sina], axis=1)
        aq = aq * cosa3 + _dot(aq.astype(BF16), ra_ref[...]) * sina3
        ak = ak * cosa + _dot(ak.astype(BF16), rak_ref[...]) * sina
        cosc = cosc_ref[...]
        sinc = sinc_ref[...]
        cq = cq * cosc + _dot(cq.astype(BF16), rc_ref[...]) * sinc
        ck = ck * cosc + _dot(ck.astype(BF16), rc_ref[...]) * sinc
    qa_o[...] = (aq * (A_D ** -0.5 * LOG2E)).astype(BF16)
    ka_o[...] = ak.astype(BF16)
    va_o[...] = av.astype(BF16)
    qc_o[...] = (cq * (C_DQ ** -0.5 * LOG2E)).astype(BF16)
    kc_o[...] = ck.astype(BF16)
    vc_o[...] = cv.astype(BF16)

    gq_o[...] = p[:, P_BQ:P_BQ + B_KW] * (B_DK ** -0.5)
    gk_o[...] = p[:, P_BK:P_BK + B_KW]
    gv_o[...] = p[:, P_BV:P_BV + B_VW]
    gr_o[...] = _silu(p[:, P_BR:P_BR + B_VW])
    z = _dot(p[:, P_BG:P_BG + 128].astype(BF16), wup_ref[...]) + bgla_ref[...]
    log_sig = jnp.minimum(z, 0.0) - jnp.log1p(jnp.exp(-jnp.abs(z)))
    gla_o[...] = log_sig * (1.0 / B_TAU)


def _proj(x, prm, consts, rope_tabs, caches, *, l, ctx):
    n = N_CTX if ctx else N_LAT
    off = 0 if ctx else N_CTX // TM_PROJ
    tiles_per_batch = (T_CTX if ctx else T_LAT) // TM_PROJ
    rope = not ctx
    full = lambda shape: pl.BlockSpec(shape, lambda i: (0,) * len(shape))
    lay = lambda shape: pl.BlockSpec((None,) + shape, lambda i: (l,) + (0,) * len(shape))
    in_specs = [
        pl.BlockSpec((TM_PROJ, D_MODEL), lambda i: (i + off, 0)),
        pl.BlockSpec((None, 8, D_MODEL), lambda i: (l, 0, 3)),
        pl.BlockSpec((None, 8, D_MODEL), lambda i: (l, 0, 4)),
        pl.BlockSpec((None, None, 1, D_MODEL), lambda i: (l, 1, 0, 0)),
        lay((D_MODEL, P_W)),
        lay((128, 2 * B_KW)),
        lay((1, 2 * B_KW)),
        lay((1, 384)), lay((1, 128)), lay((1, 384)), lay((1, 384)),
        full((384, 384)), full((128, 128)), full((384, 384)),
    ]
    args = [x, prm["mod"], prm["mod"], prm["gn4"], prm["w_p"], prm["wup"], prm["bgla"],
            prm["gaq"], prm["gak"], prm["gcq"], prm["gck"],
            consts["seg64_384"], consts["seg64_128"], consts["seg48_384"]]
    if rope:
        tpb = tiles_per_batch
        in_specs += [
            pl.BlockSpec((TM_PROJ, 128), lambda i: (i % tpb, 0)),
            pl.BlockSpec((TM_PROJ, 128), lambda i: (i % tpb, 0)),
            pl.BlockSpec((TM_PROJ, 384), lambda i: (i % tpb, 0)),
            pl.BlockSpec((TM_PROJ, 384), lambda i: (i % tpb, 0)),
            full((384, 384)), full((128, 128)), full((384, 384)),
        ]
        args += [*rope_tabs, consts["rot_a384"], consts["rot_a128"], consts["rot_c384"]]
    widths = [(384, BF16), (128, BF16), (128, BF16), (384, BF16), (384, BF16), (384, BF16),
              (B_KW, F32), (B_KW, F32), (B_VW, F32), (2 * B_KW, F32), (B_VW, F32)]
    out_shape = [jax.ShapeDtypeStruct((n, w), dt) for w, dt in widths]
    out_specs = [pl.BlockSpec((TM_PROJ, w), lambda i: (i, 0)) for w, _ in widths]
    aliases = {}
    if ctx:
        assert TM_PROJ == T_CTX
        for k, w in enumerate((128, 128, 384, 384)):
            if caches is not None:
                aliases[len(args)] = len(out_shape)
                in_specs.append(pl.BlockSpec(memory_space=pl.ANY))
                args.append(caches[k])
            out_shape.append(jax.ShapeDtypeStruct((NB_CTX, DEPTH, T_CTX, w), F32))
            out_specs.append(pl.BlockSpec((None, None, T_CTX, w), lambda i: (i, l, 0, 0)))
    return pl.pallas_call(
        functools.partial(_proj_body, rope=rope, ctx=ctx, tiles_per_batch=tiles_per_batch, n_alias=len(aliases)),
        out_shape=out_shape,
        grid=(n // TM_PROJ,),
        in_specs=in_specs,
        out_specs=out_specs,
        input_output_aliases=aliases,
        compiler_params=_cparams(("parallel",)),
        name="proj_ctx" if ctx else "proj_lat",
    )(*args)


def _gla_tools(gq_ref, gk_ref, gv_ref, gla_ref):
    C = B_CHUNK
    ri = lax.broadcasted_iota(jnp.int32, (C, C), 0)
    ci = lax.broadcasted_iota(jnp.int32, (C, C), 1)
    cum = (jnp.where(ci <= ri, 1.0, 0.0).astype(BF16), jnp.where(ci >= ri, 1.0, 0.0).astype(BF16))
    rk = lax.broadcasted_iota(jnp.int32, (B_H * C, B_KW), 0) >> 6
    ck = lax.broadcasted_iota(jnp.int32, (B_H * C, B_KW), 1) >> 5
    hm_k = rk == ck
    rv = lax.broadcasted_iota(jnp.int32, (B_H * C, B_VW), 0) >> 6
    cv = lax.broadcasted_iota(jnp.int32, (B_H * C, B_VW), 1) >> 6
    hm_v = rv == cv
    ra = lax.broadcasted_iota(jnp.int32, (C, B_H * C), 0)
    ca = lax.broadcasted_iota(jnp.int32, (C, B_H * C), 1) & (C - 1)
    tri = (ca <= ra, ca >= ra)
    zero16 = jnp.zeros((), BF16)

    def expand_state(st):
        return jnp.where(hm_k, jnp.concatenate([st] * B_H, axis=0), 0.0)

    def chunk(d, start, S):
        rows = pl.ds(start if isinstance(start, int) else pl.multiple_of(start, C), C)
        q = gq_ref[rows, :]
        k = gk_ref[rows, :]
        v = gv_ref[rows, :].astype(BF16)
        la = gla_ref[rows, d * B_KW:(d + 1) * B_KW]
        b = _split_dot_l(cum[d], la)
        tot = b[C - 1:C, :] if d == 0 else b[0:1, :]
        e = b - b[C // 2:C // 2 + 1, :]
        qt = (q * jnp.exp(e)).astype(BF16)
        kt = (k * jnp.exp(-e)).astype(BF16)
        qd = (q * jnp.exp(b)).astype(BF16)
        kd = (k * jnp.exp(tot - b)).astype(BF16)
        kbd = jnp.where(hm_k, jnp.concatenate([kt] * B_H, axis=0), zero16)
        a = _dot_nt(qt, kbd)
        a = jnp.where(tri[d], a, 0.0).astype(BF16)
        vbd = jnp.where(hm_v, jnp.concatenate([v] * B_H, axis=0), zero16)
        o = _dot(a, vbd) + _dot_nt(qd, S.astype(BF16))
        kv = _dot_tn(v, kd)
        S_new = S * jnp.exp(tot) + jnp.where(hm_k, kv, 0.0)
        return rows, o, S_new

    def collapse_state(S):
        Sm = jnp.where(hm_k, S, 0.0)
        acc = Sm[0:C, :]
        for hh in range(1, B_H):
            acc = acc + Sm[hh * C:(hh + 1) * C, :]
        return acc

    return expand_state, chunk, collapse_state


def _gla_gate(o, seg, gg, gr):
    ms = _split_dot(o * o, seg) * (1.0 / B_DV)
    return (o * lax.rsqrt(ms + EPS) * gg * gr).astype(BF16)


def _gla_body(*refs, seq, nbb, has_s0, has_alias):
    it = iter(refs)
    gq_ref, gk_ref, gv_ref, gla_ref, gr_ref = (next(it) for _ in range(5))
    s0_ref = next(it) if has_s0 else None
    gg_ref, seg_ref = next(it), next(it)
    if has_alias:
        next(it)
    ob_ref, sfin_ref, of_scr, or_scr = (next(it) for _ in range(4))
    expand_state, chunk, collapse_state = _gla_tools(gq_ref, gk_ref, gv_ref, gla_ref)
    n = seq // B_CHUNK

    def step(i, carry):
        new = []
        for bb in range(nbb):
            rows, o, Sf = chunk(0, bb * seq + i * B_CHUNK, carry[2 * bb])
            of_scr[rows, :] = o
            rows, o, Sb = chunk(1, bb * seq + (n - 1 - i) * B_CHUNK, carry[2 * bb + 1])
            or_scr[rows, :] = o
            new += [Sf, Sb]
        return tuple(new)

    if has_s0:
        carry = tuple(expand_state(s0_ref[bb, d]) for bb in range(nbb) for d in range(2))
    else:
        carry = tuple(jnp.zeros((B_VW, B_KW), F32) for _ in range(2 * nbb))
    if n <= 4:
        for i in range(n):
            carry = step(i, carry)
    else:
        carry = lax.fori_loop(0, n, step, carry, unroll=2)
    for bb in range(nbb):
        sfin_ref[bb, 0] = collapse_state(carry[2 * bb])
        sfin_ref[bb, 1] = collapse_state(carry[2 * bb + 1])

    RT = 256

    def fin(t, _):
        rows = pl.ds(pl.multiple_of(t * RT, RT), RT)
        ob_ref[rows, :] = _gla_gate(of_scr[rows, :] + or_scr[rows, :], seg_ref[...], gg_ref[...], gr_ref[rows, :])
        return 0

    lax.fori_loop(0, nbb * seq // RT, fin, 0)


def _gla(gq, gk, gv, gla, gr, s0, prm, seg, states, *, l, ctx):
    nb, seq, nbb = (NB_CTX, T_CTX, GLA_NBB_CTX) if ctx else (NB_LAT, T_LAT, GLA_NBB_LAT)
    tok = lambda w: pl.BlockSpec((nbb * seq, w), lambda b: (b, 0))
    in_specs = [tok(B_KW), tok(B_KW), tok(B_VW), tok(2 * B_KW), tok(B_VW)]
    args = [gq, gk, gv, gla, gr]
    if s0 is not None:
        in_specs.append(pl.BlockSpec((nbb, 2, B_DV, B_KW), lambda b: (b, 0, 0, 0)))
        args.append(s0)
    in_specs += [pl.BlockSpec((None, 1, B_VW), lambda b: (l, 0, 0)), pl.BlockSpec((B_VW, B_VW), lambda b: (0, 0))]
    args += [prm["ggl"], seg]
    aliases = {}
    if ctx:
        st_shape = jax.ShapeDtypeStruct((NB_CTX, DEPTH, 2, B_DV, B_KW), F32)
        st_spec = pl.BlockSpec((nbb, None, 2, B_DV, B_KW), lambda b: (b, l, 0, 0, 0))
        if states is not None:
            aliases[len(args)] = 1
            in_specs.append(pl.BlockSpec(memory_space=pl.ANY))
            args.append(states)
    else:
        st_shape = jax.ShapeDtypeStruct((nb, 2, B_DV, B_KW), F32)
        st_spec = pl.BlockSpec((nbb, 2, B_DV, B_KW), lambda b: (b, 0, 0, 0))
    return pl.pallas_call(
        functools.partial(_gla_body, seq=seq, nbb=nbb, has_s0=s0 is not None, has_alias=bool(aliases)),
        out_shape=[jax.ShapeDtypeStruct((nb * seq, B_VW), BF16), st_shape],
        grid=(nb // nbb,),
        in_specs=in_specs,
        out_specs=[tok(B_VW), st_spec],
        scratch_shapes=[pltpu.VMEM((nbb * seq, B_VW), F32), pltpu.VMEM((nbb * seq, B_VW), F32)],
        input_output_aliases=aliases,
        compiler_params=_cparams(("parallel",)),
        name="gla_ctx" if ctx else "gla_lat",
    )(*args)


def _attn_body(*refs, lam_init, ctx, nbb):
    it = iter(refs)
    x_ref, qa_ref, qc_ref, ob_ref = (next(it) for _ in range(4))
    kv_new = [next(it) for _ in range(4)]
    kv_old = None if ctx else [next(it) for _ in range(4)]
    wout_ref, gt_ref, lam_ref, gco_ref, seg_ref, o_ref, mix_scr = (next(it) for _ in range(7))

    def keys_values(bb):
        if ctx:
            return [r[bb] for r in kv_new]
        return [jnp.concatenate([rn[bb], ro[...]], axis=0) for rn, ro in zip(kv_new, kv_old)]

    b = pl.program_id(0)
    r = 0 if ctx else 1 + b
    tq = x_ref.shape[0] // nbb
    a_stack, c_stack = (A_STACK_CTX, C_STACK_CTX) if ctx else (A_STACK_LAT, C_STACK_LAT)
    lane = lax.broadcasted_iota(jnp.int32, (1, 128), 1)
    lane2 = lax.broadcasted_iota(jnp.int32, (1, 256), 1)
    hmask = [(lane >= hh * A_D) & (lane < (hh + 1) * A_D) for hh in range(A_KV)]
    lm = lam_ref[...]
    lam = (jnp.exp(jnp.sum(lm[0:1] * lm[1:2], axis=-1, keepdims=True))
           - jnp.exp(jnp.sum(lm[2:3] * lm[3:4], axis=-1, keepdims=True)) + lam_init)

    def softmax_terms(s):
        e = jnp.exp2(s - jnp.max(s, axis=-1, keepdims=True))
        return e, jnp.sum(e, axis=-1, keepdims=True)

    for bb in range(nbb):
        _attn_mix_rows(bb, tq, qa_ref, qc_ref, ob_ref, keys_values(bb), gco_ref, seg_ref, mix_scr,
                       a_stack, c_stack, hmask, lane2, lam, lam_init, softmax_terms)
    mixed = _dot(mix_scr[...], wout_ref[...])
    o_ref[...] = x_ref[...] + gt_ref[pl.ds(r, 1), :] * mixed


def _attn_mix_rows(bb, tq, qa_ref, qc_ref, ob_ref, kv, gco_ref, seg_ref, mix_scr,
                   a_stack, c_stack, hmask, lane2, lam, lam_init, softmax_terms):
    rows = slice(bb * tq, (bb + 1) * tq)
    zero16 = jnp.zeros((), BF16)
    ka, va, kc_all, vc_all = kv

    maps = [(g, hh) for g in range(A_G) for hh in range(A_KV)]
    acc = [jnp.zeros((tq, 128), F32) for _ in range(A_G)]
    for g0 in range(0, len(maps), a_stack):
        grp = maps[g0:g0 + a_stack]
        qs = jnp.concatenate(
            [jnp.where(hmask[hh], qa_ref[rows, g * 128:(g + 1) * 128], zero16) for g, hh in grp], axis=0)
        e, l = softmax_terms(_dot_nt(qs, ka))
        o = _dot(e.astype(BF16), va) * (1.0 / l)
        for k, (g, hh) in enumerate(grp):
            acc[g] = acc[g] + jnp.where(hmask[hh], o[k * tq:(k + 1) * tq], 0.0)
    for g in range(A_G):
        mix_scr[rows, g * 128:(g + 1) * 128] = acc[g].astype(BF16)

    mix_scr[rows, A_W:A_W + B_VW] = ob_ref[rows, :]

    outs = []
    for win in range(2):
        base = win * 128
        kc = kc_all[:, base:base + 256]
        vc = vc_all[:, base:base + 256]
        qc = qc_ref[rows, base:base + 256]
        ow = jnp.zeros((tq, 256), F32)
        cmaps = [(hh, mm) for hh in (2 * win, 2 * win + 1) for mm in range(2)]
        for g0 in range(0, len(cmaps), c_stack):
            grp = cmaps[g0:g0 + c_stack]
            qparts = []
            for hh, mm in grp:
                lo = (hh * 2 + mm) * C_DQ - base
                qparts.append(jnp.where((lane2 >= lo) & (lane2 < lo + C_DQ), qc, zero16))
            e, l = softmax_terms(_dot_nt(jnp.concatenate(qparts, axis=0), kc))
            ws, invs = [], []
            for k in range(0, len(grp), 2):
                l0 = l[k * tq:(k + 1) * tq]
                l1 = l[(k + 1) * tq:(k + 2) * tq]
                ws.append((e[k * tq:(k + 1) * tq] - (lam * l0 / l1) * e[(k + 1) * tq:(k + 2) * tq]).astype(BF16))
                invs.append(1.0 / l0)
            rr = _dot(jnp.concatenate(ws, axis=0) if len(ws) > 1 else ws[0], vc)
            for k in range(len(ws)):
                hh = grp[2 * k][0]
                vlo = hh * C_DV - base
                vm = (lane2 >= vlo) & (lane2 < vlo + C_DV)
                ow = ow + jnp.where(vm, rr[k * tq:(k + 1) * tq] * invs[k], 0.0)
        outs.append(ow)
    oc = jnp.concatenate([outs[0][:, :128], outs[0][:, 128:] + outs[1][:, :128], outs[1][:, 128:]], axis=1)
    ms = _split_dot(oc * oc, seg_ref[...]) * (1.0 / C_DV)
    oc = oc * lax.rsqrt(ms + EPS) * gco_ref[...] * (1.0 - lam_init)
    mix_scr[rows, A_W + B_VW:] = oc.astype(BF16)


def _attn(x, qa, qc, ob, kv_new, kv_old, prm, seg96, *, l, lam_init, ctx):
    nb, seq, nbb, tq = (NB_CTX, T_CTX, ATT_NBB_CTX, T_CTX) if ctx else (NB_LAT, T_LAT, 1, TQ_ATT_LAT)
    nq = seq // tq
    rb = nbb * tq
    off = 0 if ctx else N_CTX // rb
    qspec = lambda w: pl.BlockSpec((rb, w), lambda b, q: (b * nq + q, 0))
    one_buf = None if ctx else pl.Buffered(1)
    kspec = lambda a: pl.BlockSpec((nbb,) + a.shape[1:], lambda b, q: (b, 0, 0), pipeline_mode=one_buf)
    ospec = lambda a: pl.BlockSpec((None, None) + a.shape[2:], lambda b, q: (b, l, 0, 0), pipeline_mode=one_buf)
    lay = lambda shape: pl.BlockSpec((None,) + shape, lambda b, q: (l,) + (0,) * len(shape),
                                     pipeline_mode=pl.Buffered(1))
    xspec = pl.BlockSpec((rb, D_MODEL), lambda b, q: (off + b * nq + q, 0))
    in_specs = [xspec, qspec(384), qspec(384), qspec(B_VW)] + [kspec(a) for a in kv_new]
    args = [x, qa, qc, ob, *kv_new]
    if kv_old is not None:
        in_specs += [ospec(a) for a in kv_old]
        args += list(kv_old)
    in_specs += [lay((D_MODEL, D_MODEL)),
                 pl.BlockSpec((None, 8, D_MODEL), lambda b, q: (l, 0, 5)),
                 lay((4, C_DQ)), lay((1, 384)),
                 pl.BlockSpec((384, 384), lambda b, q: (0, 0))]
    args += [prm["w_o"], prm["mod"], prm["lam_c"], prm["gco"], seg96]
    return pl.pallas_call(
        functools.partial(_attn_body, lam_init=lam_init, ctx=ctx, nbb=nbb),
        out_shape=jax.ShapeDtypeStruct((N_TOK, D_MODEL), F32),
        grid=(nb // nbb, nq),
        in_specs=in_specs,
        out_specs=xspec,
        scratch_shapes=[pltpu.VMEM((rb, D_MODEL), BF16)],
        input_output_aliases={0: 0},
        compiler_params=_cparams(("parallel", "arbitrary")),
        name="attn_ctx" if ctx else "attn_lat",
    )(*args)


def _block_ones(width, seg):
    idx = np.arange(width) // seg
    return jnp.asarray((idx[:, None] == idx[None, :]).astype(np.float32), dtype=BF16)


def _rot_matrix(width, half):
    m = np.zeros((width, width), np.float32)
    for j in range(width):
        if (j % (2 * half)) < half:
            m[j + half, j] = -1.0
        else:
            m[j - half, j] = 1.0
    return jnp.asarray(m, dtype=BF16)


def _rope_tables():
    t = np.arange(T_LAT)
    row = (t // GRID_W).astype(np.float32)
    col = (t % GRID_W).astype(np.float32)

    def tab(head_dim, n_heads):
        m = head_dim // 4
        freqs = ROPE_BASE ** (-jnp.arange(m, dtype=F32) / m)
        ang_r = jnp.asarray(row)[:, None] * freqs[None, :]
        ang_c = jnp.asarray(col)[:, None] * freqs[None, :]
        cs = jnp.concatenate([jnp.cos(ang_r)] * 2 + [jnp.cos(ang_c)] * 2, axis=1)
        sn = jnp.concatenate([jnp.sin(ang_r)] * 2 + [jnp.sin(ang_c)] * 2, axis=1)
        return jnp.tile(cs, (1, n_heads)), jnp.tile(sn, (1, n_heads))

    cosa, sina = tab(A_D, 2)
    cosc, sinc = tab(C_DQ, 2 * C_H)
    return cosa, sina, cosc, sinc


def kernel(x_prompt, x_sample, c, cache_a_k, cache_a_v, cache_c_k, cache_c_v, state_gla, c_ctx, w_ada, b_ada,
           g_norm, w_ffn_gate, w_ffn_up, w_ffn_down, w_in, g_a_q, g_a_k, w_gla_up, b_gla, g_gla, g_c_q, g_c_k,
           lam_c, g_c_out, w_out):
    lam_inits = [0.8 - 0.6 * math.exp(-0.3 * l) for l in range(DEPTH)]

    consts = {
        "seg64_384": _block_ones(384, 64), "seg64_128": _block_ones(128, 64),
        "seg48_384": _block_ones(384, 48), "seg96_384": _block_ones(384, 96),
        "seg64_256": _block_ones(256, 64),
        "rot_a384": _rot_matrix(384, 16), "rot_a128": _rot_matrix(128, 16), "rot_c384": _rot_matrix(384, 12),
    }
    rope_tabs = _rope_tables()

    wq = w_in[:, :, 0:384].reshape(DEPTH, D_MODEL, A_KV, A_G, A_D).transpose(0, 1, 3, 2, 4).reshape(DEPTH, D_MODEL, 384)
    seg = lambda a, b: w_in[:, :, a:b]
    w_p = jnp.concatenate([
        wq, seg(384, 512), seg(512, 640),
        seg(640, 768), seg(768, 896), seg(896, 1152),
        seg(1184, 1440),
        seg(1440, 1824), seg(1824, 2208), seg(2208, 2592),
        seg(1152, 1184), jnp.zeros((DEPTH, D_MODEL, 96), F32),
    ], axis=2).astype(BF16)
    wo_a = w_out[:, 0:384].reshape(DEPTH, A_KV, A_G, A_D, D_MODEL).transpose(0, 2, 1, 3, 4).reshape(DEPTH, 384, D_MODEL)
    wup = jnp.zeros((DEPTH, 128, 2 * B_KW), F32)
    wup = wup.at[:, 0:B_RANK, 0:B_KW].set(w_gla_up[:, 0]).at[:, B_RANK:2 * B_RANK, B_KW:].set(w_gla_up[:, 1])
    cond8 = jnp.zeros((8, D_MODEL), F32).at[0].set(c_ctx).at[1:3].set(c)
    prm = {
        "mod": _adaln(cond8, w_ada, b_ada),
        "gn4": g_norm.reshape(DEPTH, 3, 1, D_MODEL),
        "w_p": w_p,
        "w_o": jnp.concatenate([wo_a, w_out[:, 384:]], axis=1).astype(BF16),
        "wup": wup.astype(BF16),
        "bgla": b_gla.reshape(DEPTH, 1, 2 * B_KW),
        "gaq": jnp.tile(g_a_q, (1, 6)).reshape(DEPTH, 1, 384),
        "gak": jnp.tile(g_a_k, (1, 2)).reshape(DEPTH, 1, 128),
        "gcq": jnp.tile(g_c_q.reshape(DEPTH, 96), (1, 4)).reshape(DEPTH, 1, 384),
        "gck": jnp.tile(g_c_k.reshape(DEPTH, 96), (1, 4)).reshape(DEPTH, 1, 384),
        "gco": jnp.tile(g_c_out, (1, 4)).reshape(DEPTH, 1, 384),
        "ggl": jnp.tile(g_gla, (1, 4)).reshape(DEPTH, 1, 256),
        "lam_c": lam_c,
    }

    caches, states = None, None
    s0_lat = jnp.swapaxes(state_gla.reshape(NB_LAT, DEPTH, 2, B_KW, B_DV), 3, 4)
    kv_old = [a.reshape(NB_LAT, DEPTH, PAST_LEN, a.shape[-2] * a.shape[-1]).astype(BF16)
              for a in (cache_a_k, cache_a_v, cache_c_k, cache_c_v)]
    ffn = functools.partial(_ffn, mod=prm["mod"], gn4=prm["gn4"], wg=w_ffn_gate, wu=w_ffn_up, wd=w_ffn_down)
    x = None
    for l in range(DEPTH):
        if l == 0:
            x = ffn((x_prompt.reshape(N_CTX, D_MODEL), x_sample.reshape(N_LAT, D_MODEL)), l=0, s=0, first=True)
        else:
            x = ffn((x,), l=l, s=0)

        outs = _proj(x, prm, consts, None, caches, l=l, ctx=True)
        qa, ka, va, qc, kc, vc, gq, gk, gv, gla, gr = outs[:11]
        caches = list(outs[11:])
        ob, states = _gla(gq, gk, gv, gla, gr, None, prm, consts["seg64_256"], states, l=l, ctx=True)
        r3 = lambda a, n, t: a.reshape(n, t, a.shape[-1])
        x = _attn(x, qa, qc, ob, [r3(a, NB_CTX, T_CTX) for a in (ka, va, kc, vc)], None, prm, consts["seg96_384"],
                  l=l, lam_init=lam_inits[l], ctx=True)

        qa, ka, va, qc, kc, vc, gq, gk, gv, gla, gr = _proj(x, prm, consts, rope_tabs, None, l=l, ctx=False)
        ob, _ = _gla(gq, gk, gv, gla, gr, s0_lat[:, l], prm, consts["seg64_256"], None, l=l, ctx=False)
        x = _attn(x, qa, qc, ob, [r3(a, NB_LAT, T_LAT) for a in (ka, va, kc, vc)], kv_old, prm, consts["seg96_384"],
                  l=l, lam_init=lam_inits[l], ctx=False)

        if l == DEPTH - 1:
            y_prompt, y_sample = ffn((x,), l=l, s=1, last=True)
        else:
            x = ffn((x,), l=l, s=1)

    return (y_prompt.reshape(NB_CTX, T_CTX, D_MODEL), y_sample.reshape(NB_LAT, T_LAT, D_MODEL),
            caches[0].reshape(NB_CTX, DEPTH, T_CTX, A_KV, A_D), caches[1].reshape(NB_CTX, DEPTH, T_CTX, A_KV, A_D),
            caches[2].reshape(NB_CTX, DEPTH, T_CTX, C_H, 2 * C_DQ), caches[3].reshape(NB_CTX, DEPTH, T_CTX, C_H, C_DV),
            jnp.swapaxes(states, 3, 4).reshape(NB_CTX, DEPTH, 2, B_H, B_DK, B_DV))
```

```python
import functools
import math

import numpy as np
import jax
import jax.numpy as jnp
from jax import lax
from jax.experimental import pallas as pl
from jax.experimental.pallas import tpu as pltpu

F32 = jnp.float32
BF16 = jnp.bfloat16

D_MODEL = 1024
D_FF = 2816
DEPTH = 4
N_MOD = 9
EPS = 1e-6
ROPE_BASE = 10000.0
GRID_W = 64
LOG2E = math.log2(math.e)

NB_CTX, T_CTX = 32, 256
NB_LAT, T_LAT = 2, 2048
PAST_LEN = 512
N_CTX = NB_CTX * T_CTX
N_LAT = NB_LAT * T_LAT
N_TOK = N_CTX + N_LAT

A_KV, A_G, A_D = 2, 3, 64
A_W = A_KV * A_G * A_D
B_H, B_DK, B_DV = 4, 32, 64
B_KW = B_H * B_DK
B_VW = B_H * B_DV
B_RANK = 16
B_TAU = 16.0
B_CHUNK = 64
C_H, C_DQ, C_DV = 4, 48, 96
C_W = C_H * C_DV

P_AQ, P_AK, P_AV = 0, 384, 512
P_BQ, P_BK, P_BV, P_BR = 640, 768, 896, 1152
P_CQ, P_CK, P_CV = 1408, 1792, 2176
P_BG = 2560
P_W = 2688

TM_FFN = 1024
TF_FFN = 256
TM_PROJ = 256
TQ_ATT_LAT = 512
ATT_NBB_CTX = 4
TN_ADA = 1152
GLA_NBB_CTX = 8
GLA_NBB_LAT = 2
A_STACK_CTX, A_STACK_LAT = 6, 2
C_STACK_CTX, C_STACK_LAT = 4, 2
VMEM_LIMIT = 56 * 1024 * 1024


def _cparams(sem):
    return pltpu.CompilerParams(dimension_semantics=sem, vmem_limit_bytes=VMEM_LIMIT)


def _dot(a, b):
    return jnp.dot(a, b, preferred_element_type=F32)


def _dot_nt(a, b):
    return lax.dot_general(a, b, (((1,), (1,)), ((), ())), preferred_element_type=F32)


def _dot_tn(a, b):
    return lax.dot_general(a, b, (((0,), (0,)), ((), ())), preferred_element_type=F32)


def _split_dot(x, m):
    hi = x.astype(BF16)
    lo = (x - hi.astype(F32)).astype(BF16)
    return _dot(hi, m) + _dot(lo, m)


def _split_dot_l(m, x):
    hi = x.astype(BF16)
    lo = (x - hi.astype(F32)).astype(BF16)
    return _dot(m, hi) + _dot(m, lo)


def _seg_rms(x, seg, n):
    ms = _dot((x * x).astype(BF16), seg) * (1.0 / n)
    return x * lax.rsqrt(ms + EPS)


def _silu(x):
    return x * jax.nn.sigmoid(x)


def _norm_mod(x, gn, sc, sh):
    ms = jnp.mean(x * x, axis=-1, keepdims=True)
    return ((x * lax.rsqrt(ms + EPS)) * (gn * (1.0 + sc)) + sh).astype(BF16)


def _adaln_body(cond_ref, w_ref, b_ref, o_ref):
    c = cond_ref[...]
    sc = _silu(c).astype(BF16)
    o_ref[0] = _dot(sc, w_ref[0].astype(BF16)) + b_ref[0]


def _adaln(cond8, w_ada, b_ada):
    nj = (N_MOD * D_MODEL) // TN_ADA
    return pl.pallas_call(
        _adaln_body,
        out_shape=jax.ShapeDtypeStruct((DEPTH, 8, N_MOD * D_MODEL), F32),
        grid=(DEPTH, nj),
        in_specs=[
            pl.BlockSpec((8, D_MODEL), lambda l, j: (0, 0)),
            pl.BlockSpec((1, D_MODEL, TN_ADA), lambda l, j: (l, 0, j)),
            pl.BlockSpec((1, 1, TN_ADA), lambda l, j: (l, 0, j)),
        ],
        out_specs=pl.BlockSpec((1, 8, TN_ADA), lambda l, j: (l, 0, j)),
        compiler_params=_cparams(("parallel", "parallel")),
        name="adaln",
    )(cond8, w_ada, b_ada.reshape(DEPTH, 1, N_MOD * D_MODEL))


FFN_TILES_CTX = N_CTX // TM_FFN
FFN_TILES_PER_LAT = T_LAT // TM_FFN


def _ffn_body(*refs, first, last):
    it = iter(refs)
    x_refs = (next(it), next(it)) if first else (next(it),)
    sh_ref, sc_ref, gt_ref, gn_ref, wg_ref, wu_ref, wd_ref = (next(it) for _ in range(7))
    o_refs = (next(it), next(it)) if last else (next(it),)
    h_scr, acc_scr = next(it), next(it)

    i = pl.program_id(0)
    j = pl.program_id(1)
    is_ctx = i < FFN_TILES_CTX
    r = jnp.where(is_ctx, 0, 1 + jnp.maximum(i - FFN_TILES_CTX, 0) // FFN_TILES_PER_LAT)

    def on_tile(pred, n_variants, fn):
        if n_variants == 1:
            pl.when(pred)(lambda: fn(0))
        else:
            pl.when(pred & is_ctx)(lambda: fn(0))
            pl.when(pred & jnp.logical_not(is_ctx))(lambda: fn(1))

    def prologue(k):
        h_scr[...] = _norm_mod(x_refs[k][...], gn_ref[...], sc_ref[pl.ds(r, 1), :], sh_ref[pl.ds(r, 1), :])
        acc_scr[...] = jnp.zeros_like(acc_scr)

    on_tile(j == 0, len(x_refs), prologue)

    h = h_scr[...]
    g = _dot(h, wg_ref[...].astype(BF16))
    u = _dot(h, wu_ref[...].astype(BF16))
    a = (_silu(g) * u).astype(BF16)
    acc_scr[...] += _dot(a, wd_ref[...].astype(BF16))

    def epilogue(k):
        x_ref = x_refs[k if first else 0]
        o_ref = o_refs[k if last else 0]
        o_ref[...] = x_ref[...] + 0.5 * gt_ref[pl.ds(r, 1), :] * acc_scr[...]

    on_tile(j == pl.num_programs(1) - 1, max(len(x_refs), len(o_refs)), epilogue)


def _ffn(xs, mod, gn4, wg, wu, wd, *, l, s, first=False, last=False):
    ni = N_TOK // TM_FFN
    nj = D_FF // TF_FFN
    k0 = 6 * s
    gi = 2 * s
    tc = FFN_TILES_CTX
    split_specs = [pl.BlockSpec((TM_FFN, D_MODEL), lambda i, j: (jnp.minimum(i, tc - 1), 0)),
                   pl.BlockSpec((TM_FFN, D_MODEL), lambda i, j: (jnp.maximum(i - tc, 0), 0))]
    one_spec = [pl.BlockSpec((TM_FFN, D_MODEL), lambda i, j: (i, 0))]
    mspec = lambda k: pl.BlockSpec((None, 8, D_MODEL), lambda i, j: (l, 0, k))
    in_specs = (split_specs if first else one_spec) + [
        mspec(k0), mspec(k0 + 1), mspec(k0 + 2),
        pl.BlockSpec((None, None, 1, D_MODEL), lambda i, j: (l, gi, 0, 0)),
        pl.BlockSpec((None, None, D_MODEL, TF_FFN), lambda i, j: (l, s, 0, j)),
        pl.BlockSpec((None, None, D_MODEL, TF_FFN), lambda i, j: (l, s, 0, j)),
        pl.BlockSpec((None, None, TF_FFN, D_MODEL), lambda i, j: (l, s, j, 0)),
    ]
    if last:
        out_shape = [jax.ShapeDtypeStruct((N_CTX, D_MODEL), F32), jax.ShapeDtypeStruct((N_LAT, D_MODEL), F32)]
        out_specs = split_specs
    else:
        out_shape = jax.ShapeDtypeStruct((N_TOK, D_MODEL), F32)
        out_specs = one_spec[0]
    return pl.pallas_call(
        functools.partial(_ffn_body, first=first, last=last),
        out_shape=out_shape,
        grid=(ni, nj),
        in_specs=in_specs,
        out_specs=out_specs,
        scratch_shapes=[pltpu.VMEM((TM_FFN, D_MODEL), BF16), pltpu.VMEM((TM_FFN, D_MODEL), F32)],
        compiler_params=_cparams(("arbitrary", "arbitrary")),
        name="ffn",
    )(*xs, mod, mod, mod, gn4, wg, wu, wd)


def _proj_body(*refs, rope, ctx, tiles_per_batch, n_alias):
    it = iter(refs)
    x_ref, sh_ref, sc_ref, gn_ref, w_ref, wup_ref, bgla_ref = (next(it) for _ in range(7))
    gaq_ref, gak_ref, gcq_ref, gck_ref = (next(it) for _ in range(4))
    s64a_ref, s64k_ref, s48_ref = (next(it) for _ in range(3))
    if rope:
        cosa_ref, sina_ref, cosc_ref, sinc_ref, ra_ref, rak_ref, rc_ref = (next(it) for _ in range(7))
    for _ in range(n_alias):
        next(it)
    qa_o, ka_o, va_o, qc_o, kc_o, vc_o, gq_o, gk_o, gv_o, gla_o, gr_o = (next(it) for _ in range(11))
    if ctx:
        ka32_o, va32_o, kc32_o, vc32_o = (next(it) for _ in range(4))

    i = pl.program_id(0)
    r = 0 if ctx else 1 + i // tiles_per_batch

    h = _norm_mod(x_ref[...], gn_ref[...], sc_ref[pl.ds(r, 1), :], sh_ref[pl.ds(r, 1), :])
    p = _dot(h, w_ref[...])

    aq = _seg_rms(p[:, P_AQ:P_AQ + A_W], s64a_ref[...], A_D) * gaq_ref[...]
    ak = _seg_rms(p[:, P_AK:P_AK + 128], s64k_ref[...], A_D) * gak_ref[...]
    av = p[:, P_AV:P_AV + 128]
    cq = _seg_rms(p[:, P_CQ:P_CQ + 384], s48_ref[...], C_DQ) * gcq_ref[...]
    ck = _seg_rms(p[:, P_CK:P_CK + 384], s48_ref[...], C_DQ) * gck_ref[...]
    cv = p[:, P_CV:P_CV + 384]
    if ctx:
        ka32_o[...] = ak
        va32_o[...] = av
        kc32_o[...] = ck
        vc32_o[...] = cv
    if rope:
        cosa = cosa_ref[...]
        sina = sina_ref[...]
        cosa3 = jnp.concatenate([cosa, cosa, cosa], axis=1)
        sina3 = jnp.concatenate([sina, sina, sina], axis=1)
        aq = aq * cosa3 + _dot(aq.astype(BF16), ra_ref[...]) * sina3
        ak = ak * cosa + _dot(ak.astype(BF16), rak_ref[...]) * sina
        cosc = cosc_ref[...]
        sinc = sinc_ref[...]
        cq = cq * cosc + _dot(cq.astype(BF16), rc_ref[...]) * sinc
        ck = ck * cosc + _dot(ck.astype(BF16), rc_ref[...]) * sinc
    qa_o[...] = (aq * (A_D ** -0.5 * LOG2E)).astype(BF16)
    ka_o[...] = ak.astype(BF16)
    va_o[...] = av.astype(BF16)
    qc_o[...] = (cq * (C_DQ ** -0.5 * LOG2E)).astype(BF16)
    kc_o[...] = ck.astype(BF16)
    vc_o[...] = cv.astype(BF16)

    gq_o[...] = p[:, P_BQ:P_BQ + B_KW] * (B_DK ** -0.5)
    gk_o[...] = p[:, P_BK:P_BK + B_KW]
    gv_o[...] = p[:, P_BV:P_BV + B_VW]
    gr_o[...] = _silu(p[:, P_BR:P_BR + B_VW])
    z = _dot(p[:, P_BG:P_BG + 128].astype(BF16), wup_ref[...]) + bgla_ref[...]
    log_sig = jnp.minimum(z, 0.0) - jnp.log1p(jnp.exp(-jnp.abs(z)))
    gla_o[...] = log_sig * (1.0 / B_TAU)


def _proj(x, prm, consts, rope_tabs, caches, *, l, ctx):
    n = N_CTX if ctx else N_LAT
    off = 0 if ctx else N_CTX // TM_PROJ
    tiles_per_batch = (T_CTX if ctx else T_LAT) // TM_PROJ
    rope = not ctx
    full = lambda shape: pl.BlockSpec(shape, lambda i: (0,) * len(shape))
    lay = lambda shape: pl.BlockSpec((None,) + shape, lambda i: (l,) + (0,) * len(shape))
    in_specs = [
        pl.BlockSpec((TM_PROJ, D_MODEL), lambda i: (i + off, 0)),
        pl.BlockSpec((None, 8, D_MODEL), lambda i: (l, 0, 3)),
        pl.BlockSpec((None, 8, D_MODEL), lambda i: (l, 0, 4)),
        pl.BlockSpec((None, None, 1, D_MODEL), lambda i: (l, 1, 0, 0)),
        lay((D_MODEL, P_W)),
        lay((128, 2 * B_KW)),
        lay((1, 2 * B_KW)),
        lay((1, 384)), lay((1, 128)), lay((1, 384)), lay((1, 384)),
        full((384, 384)), full((128, 128)), full((384, 384)),
    ]
    args = [x, prm["mod"], prm["mod"], prm["gn4"], prm["w_p"], prm["wup"], prm["bgla"],
            prm["gaq"], prm["gak"], prm["gcq"], prm["gck"],
            consts["seg64_384"], consts["seg64_128"], consts["seg48_384"]]
    if rope:
        tpb = tiles_per_batch
        in_specs += [
            pl.BlockSpec((TM_PROJ, 128), lambda i: (i % tpb, 0)),
            pl.BlockSpec((TM_PROJ, 128), lambda i: (i % tpb, 0)),
            pl.BlockSpec((TM_PROJ, 384), lambda i: (i % tpb, 0)),
            pl.BlockSpec((TM_PROJ, 384), lambda i: (i % tpb, 0)),
            full((384, 384)), full((128, 128)), full((384, 384)),
        ]
        args += [*rope_tabs, consts["rot_a384"], consts["rot_a128"], consts["rot_c384"]]
    widths = [(384, BF16), (128, BF16), (128, BF16), (384, BF16), (384, BF16), (384, BF16),
              (B_KW, F32), (B_KW, F32), (B_VW, F32), (2 * B_KW, F32), (B_VW, F32)]
    out_shape = [jax.ShapeDtypeStruct((n, w), dt) for w, dt in widths]
    out_specs = [pl.BlockSpec((TM_PROJ, w), lambda i: (i, 0)) for w, _ in widths]
    aliases = {}
    if ctx:
        assert TM_PROJ == T_CTX
        for k, w in enumerate((128, 128, 384, 384)):
            if caches is not None:
                aliases[len(args)] = len(out_shape)
                in_specs.append(pl.BlockSpec(memory_space=pl.ANY))
                args.append(caches[k])
            out_shape.append(jax.ShapeDtypeStruct((NB_CTX, DEPTH, T_CTX, w), F32))
            out_specs.append(pl.BlockSpec((None, None, T_CTX, w), lambda i: (i, l, 0, 0)))
    return pl.pallas_call(
        functools.partial(_proj_body, rope=rope, ctx=ctx, tiles_per_batch=tiles_per_batch, n_alias=len(aliases)),
        out_shape=out_shape,
        grid=(n // TM_PROJ,),
        in_specs=in_specs,
        out_specs=out_specs,
        input_output_aliases=aliases,
        compiler_params=_cparams(("parallel",)),
        name="proj_ctx" if ctx else "proj_lat",
    )(*args)


def _gla_tools(gq_ref, gk_ref, gv_ref, gla_ref):
    C = B_CHUNK
    row_i = lax.broadcasted_iota(jnp.int32, (C, B_KW), 0)
    rk = lax.broadcasted_iota(jnp.int32, (B_H * C, B_KW), 0) >> 6
    ck = lax.broadcasted_iota(jnp.int32, (B_H * C, B_KW), 1) >> 5
    hm_k = rk == ck
    rv = lax.broadcasted_iota(jnp.int32, (B_H * C, B_VW), 0) >> 6
    cv = lax.broadcasted_iota(jnp.int32, (B_H * C, B_VW), 1) >> 6
    hm_v = rv == cv
    ra = lax.broadcasted_iota(jnp.int32, (C, B_H * C), 0)
    ca = lax.broadcasted_iota(jnp.int32, (C, B_H * C), 1) & (C - 1)
    tri = (ca <= ra, ca >= ra)
    zero16 = jnp.zeros((), BF16)

    def expand_state(st):
        return jnp.where(hm_k, jnp.concatenate([st] * B_H, axis=0), 0.0)

    def chunk(d, start, S):
        rows = pl.ds(start if isinstance(start, int) else pl.multiple_of(start, C), C)
        q = gq_ref[rows, :]
        k = gk_ref[rows, :]
        v = gv_ref[rows, :].astype(BF16)
        la = gla_ref[rows, d * B_KW:(d + 1) * B_KW]
        p = la
        for s in (1, 2, 4, 8, 16, 32):
            p = p + jnp.where(row_i >= s, pltpu.roll(p, s, axis=0), 0.0)
        tot = p[C - 1:C, :]
        b = p if d == 0 else tot - p + la
        e = b - b[C // 2:C // 2 + 1, :]
        qt = (q * jnp.exp(e)).astype(BF16)
        kt = (k * jnp.exp(-e)).astype(BF16)
        qd = (q * jnp.exp(b)).astype(BF16)
        kd = (k * jnp.exp(tot - b)).astype(BF16)
        kbd = jnp.where(hm_k, jnp.concatenate([kt] * B_H, axis=0), zero16)
        a = _dot_nt(qt, kbd)
        a = jnp.where(tri[d], a, 0.0).astype(BF16)
        vbd = jnp.where(hm_v, jnp.concatenate([v] * B_H, axis=0), zero16)
        o = _dot(a, vbd) + _dot_nt(qd, S.astype(BF16))
        kv = _dot_tn(v, kd)
        S_new = S * jnp.exp(tot) + jnp.where(hm_k, kv, 0.0)
        return rows, o, S_new

    def collapse_state(S):
        Sm = jnp.where(hm_k, S, 0.0)
        acc = Sm[0:C, :]
        for hh in range(1, B_H):
            acc = acc + Sm[hh * C:(hh + 1) * C, :]
        return acc

    return expand_state, chunk, collapse_state


def _gla_gate(o, seg, gg, gr):
    ms = _split_dot(o * o, seg) * (1.0 / B_DV)
    return (o * lax.rsqrt(ms + EPS) * gg * gr).astype(BF16)


def _gla_body(*refs, seq, nbb, has_s0, has_alias):
    it = iter(refs)
    gq_ref, gk_ref, gv_ref, gla_ref, gr_ref = (next(it) for _ in range(5))
    s0_ref = next(it) if has_s0 else None
    gg_ref, seg_ref = next(it), next(it)
    if has_alias:
        next(it)
    ob_ref, sfin_ref, of_scr, or_scr = (next(it) for _ in range(4))
    expand_state, chunk, collapse_state = _gla_tools(gq_ref, gk_ref, gv_ref, gla_ref)
    n = seq // B_CHUNK

    def step(i, carry):
        new = []
        for bb in range(nbb):
            rows, o, Sf = chunk(0, bb * seq + i * B_CHUNK, carry[2 * bb])
            of_scr[rows, :] = o
            rows, o, Sb = chunk(1, bb * seq + (n - 1 - i) * B_CHUNK, carry[2 * bb + 1])
            or_scr[rows, :] = o
            new += [Sf, Sb]
        return tuple(new)

    if has_s0:
        carry = tuple(expand_state(s0_ref[bb, d]) for bb in range(nbb) for d in range(2))
    else:
        carry = tuple(jnp.zeros((B_VW, B_KW), F32) for _ in range(2 * nbb))
    if n <= 4:
        for i in range(n):
            carry = step(i, carry)
    else:
        carry = lax.fori_loop(0, n, step, carry, unroll=2)
    for bb in range(nbb):
        sfin_ref[bb, 0] = collapse_state(carry[2 * bb])
        sfin_ref[bb, 1] = collapse_state(carry[2 * bb + 1])

    RT = 256

    def fin(t, _):
        rows = pl.ds(pl.multiple_of(t * RT, RT), RT)
        ob_ref[rows, :] = _gla_gate(of_scr[rows, :] + or_scr[rows, :], seg_ref[...], gg_ref[...], gr_ref[rows, :])
        return 0

    lax.fori_loop(0, nbb * seq // RT, fin, 0)


def _gla(gq, gk, gv, gla, gr, s0, prm, seg, states, *, l, ctx):
    nb, seq, nbb = (NB_CTX, T_CTX, GLA_NBB_CTX) if ctx else (NB_LAT, T_LAT, GLA_NBB_LAT)
    tok = lambda w: pl.BlockSpec((nbb * seq, w), lambda b: (b, 0))
    in_specs = [tok(B_KW), tok(B_KW), tok(B_VW), tok(2 * B_KW), tok(B_VW)]
    args = [gq, gk, gv, gla, gr]
    if s0 is not None:
        in_specs.append(pl.BlockSpec((nbb, 2, B_DV, B_KW), lambda b: (b, 0, 0, 0)))
        args.append(s0)
    in_specs += [pl.BlockSpec((None, 1, B_VW), lambda b: (l, 0, 0)), pl.BlockSpec((B_VW, B_VW), lambda b: (0, 0))]
    args += [prm["ggl"], seg]
    aliases = {}
    if ctx:
        st_shape = jax.ShapeDtypeStruct((NB_CTX, DEPTH, 2, B_DV, B_KW), F32)
        st_spec = pl.BlockSpec((nbb, None, 2, B_DV, B_KW), lambda b: (b, l, 0, 0, 0))
        if states is not None:
            aliases[len(args)] = 1
            in_specs.append(pl.BlockSpec(memory_space=pl.ANY))
            args.append(states)
    else:
        st_shape = jax.ShapeDtypeStruct((nb, 2, B_DV, B_KW), F32)
        st_spec = pl.BlockSpec((nbb, 2, B_DV, B_KW), lambda b: (b, 0, 0, 0))
    return pl.pallas_call(
        functools.partial(_gla_body, seq=seq, nbb=nbb, has_s0=s0 is not None, has_alias=bool(aliases)),
        out_shape=[jax.ShapeDtypeStruct((nb * seq, B_VW), BF16), st_shape],
        grid=(nb // nbb,),
        in_specs=in_specs,
        out_specs=[tok(B_VW), st_spec],
        scratch_shapes=[pltpu.VMEM((nbb * seq, B_VW), F32), pltpu.VMEM((nbb * seq, B_VW), F32)],
        input_output_aliases=aliases,
        compiler_params=_cparams(("parallel",)),
        name="gla_ctx" if ctx else "gla_lat",
    )(*args)


def _attn_body(*refs, lam_init, ctx, nbb):
    it = iter(refs)
    x_ref, qa_ref, qc_ref, ob_ref = (next(it) for _ in range(4))
    kv_new = [next(it) for _ in range(4)]
    kv_old = None if ctx else [next(it) for _ in range(4)]
    wout_ref, gt_ref, lam_ref, gco_ref, seg_ref, o_ref, mix_scr = (next(it) for _ in range(7))

    def keys_values(bb):
        if ctx:
            return [r[bb] for r in kv_new]
        return [jnp.concatenate([rn[bb], ro[...]], axis=0) for rn, ro in zip(kv_new, kv_old)]

    b = pl.program_id(0)
    r = 0 if ctx else 1 + b
    tq = x_ref.shape[0] // nbb
    a_stack, c_stack = (A_STACK_CTX, C_STACK_CTX) if ctx else (A_STACK_LAT, C_STACK_LAT)
    lane = lax.broadcasted_iota(jnp.int32, (1, 128), 1)
    lane2 = lax.broadcasted_iota(jnp.int32, (1, 256), 1)
    hmask = [(lane >= hh * A_D) & (lane < (hh + 1) * A_D) for hh in range(A_KV)]
    lm = lam_ref[...]
    lam = (jnp.exp(jnp.sum(lm[0:1] * lm[1:2], axis=-1, keepdims=True))
           - jnp.exp(jnp.sum(lm[2:3] * lm[3:4], axis=-1, keepdims=True)) + lam_init)

    def softmax_terms(s):
        e = jnp.exp2(s - jnp.max(s, axis=-1, keepdims=True))
        return e, jnp.sum(e, axis=-1, keepdims=True)

    for bb in range(nbb):
        _attn_mix_rows(bb, tq, qa_ref, qc_ref, ob_ref, keys_values(bb), gco_ref, seg_ref, mix_scr,
                       a_stack, c_stack, hmask, lane2, lam, lam_init, softmax_terms)
    mixed = _dot(mix_scr[...], wout_ref[...])
    o_ref[...] = x_ref[...] + gt_ref[pl.ds(r, 1), :] * mixed


def _attn_mix_rows(bb, tq, qa_ref, qc_ref, ob_ref, kv, gco_ref, seg_ref, mix_scr,
                   a_stack, c_stack, hmask, lane2, lam, lam_init, softmax_terms):
    rows = slice(bb * tq, (bb + 1) * tq)
    zero16 = jnp.zeros((), BF16)
    ka, va, kc_all, vc_all = kv

    maps = [(g, hh) for g in range(A_G) for hh in range(A_KV)]
    acc = [jnp.zeros((tq, 128), F32) for _ in range(A_G)]
    for g0 in range(0, len(maps), a_stack):
        grp = maps[g0:g0 + a_stack]
        qs = jnp.concatenate(
            [jnp.where(hmask[hh], qa_ref[rows, g * 128:(g + 1) * 128], zero16) for g, hh in grp], axis=0)
        e, l = softmax_terms(_dot_nt(qs, ka))
        o = _dot(e.astype(BF16), va) * (1.0 / l)
        for k, (g, hh) in enumerate(grp):
            acc[g] = acc[g] + jnp.where(hmask[hh], o[k * tq:(k + 1) * tq], 0.0)
    for g in range(A_G):
        mix_scr[rows, g * 128:(g + 1) * 128] = acc[g].astype(BF16)

    mix_scr[rows, A_W:A_W + B_VW] = ob_ref[rows, :]

    outs = []
    for win in range(2):
        base = win * 128
        kc = kc_all[:, base:base + 256]
        vc = vc_all[:, base:base + 256]
        qc = qc_ref[rows, base:base + 256]
        ow = jnp.zeros((tq, 256), F32)
        cmaps = [(hh, mm) for hh in (2 * win, 2 * win + 1) for mm in range(2)]
        for g0 in range(0, len(cmaps), c_stack):
            grp = cmaps[g0:g0 + c_stack]
            qparts = []
            for hh, mm in grp:
                lo = (hh * 2 + mm) * C_DQ - base
                qparts.append(jnp.where((lane2 >= lo) & (lane2 < lo + C_DQ), qc, zero16))
            e, l = softmax_terms(_dot_nt(jnp.concatenate(qparts, axis=0), kc))
            ws, invs = [], []
            for k in range(0, len(grp), 2):
                l0 = l[k * tq:(k + 1) * tq]
                l1 = l[(k + 1) * tq:(k + 2) * tq]
                ws.append((e[k * tq:(k + 1) * tq] - (lam * l0 / l1) * e[(k + 1) * tq:(k + 2) * tq]).astype(BF16))
                invs.append(1.0 / l0)
            rr = _dot(jnp.concatenate(ws, axis=0) if len(ws) > 1 else ws[0], vc)
            for k in range(len(ws)):
                hh = grp[2 * k][0]
                vlo = hh * C_DV - base
                vm = (lane2 >= vlo) & (lane2 < vlo + C_DV)
                ow = ow + jnp.where(vm, rr[k * tq:(k + 1) * tq] * invs[k], 0.0)
        outs.append(ow)
    oc = jnp.concatenate([outs[0][:, :128], outs[0][:, 128:] + outs[1][:, :128], outs[1][:, 128:]], axis=1)
    ms = _split_dot(oc * oc, seg_ref[...]) * (1.0 / C_DV)
    oc = oc * lax.rsqrt(ms + EPS) * gco_ref[...] * (1.0 - lam_init)
    mix_scr[rows, A_W + B_VW:] = oc.astype(BF16)


def _attn(x, qa, qc, ob, kv_new, kv_old, prm, seg96, *, l, lam_init, ctx):
    nb, seq, nbb, tq = (NB_CTX, T_CTX, ATT_NBB_CTX, T_CTX) if ctx else (NB_LAT, T_LAT, 1, TQ_ATT_LAT)
    nq = seq // tq
    rb = nbb * tq
    off = 0 if ctx else N_CTX // rb
    qspec = lambda w: pl.BlockSpec((rb, w), lambda b, q: (b * nq + q, 0))
    one_buf = None if ctx else pl.Buffered(1)
    kspec = lambda a: pl.BlockSpec((nbb,) + a.shape[1:], lambda b, q: (b, 0, 0), pipeline_mode=one_buf)
    ospec = lambda a: pl.BlockSpec((None, None) + a.shape[2:], lambda b, q: (b, l, 0, 0), pipeline_mode=one_buf)
    lay = lambda shape: pl.BlockSpec((None,) + shape, lambda b, q: (l,) + (0,) * len(shape),
                                     pipeline_mode=pl.Buffered(1))
    xspec = pl.BlockSpec((rb, D_MODEL), lambda b, q: (off + b * nq + q, 0))
    in_specs = [xspec, qspec(384), qspec(384), qspec(B_VW)] + [kspec(a) for a in kv_new]
    args = [x, qa, qc, ob, *kv_new]
    if kv_old is not None:
        in_specs += [ospec(a) for a in kv_old]
        args += list(kv_old)
    in_specs += [lay((D_MODEL, D_MODEL)),
                 pl.BlockSpec((None, 8, D_MODEL), lambda b, q: (l, 0, 5)),
                 lay((4, C_DQ)), lay((1, 384)),
                 pl.BlockSpec((384, 384), lambda b, q: (0, 0))]
    args += [prm["w_o"], prm["mod"], prm["lam_c"], prm["gco"], seg96]
    return pl.pallas_call(
        functools.partial(_attn_body, lam_init=lam_init, ctx=ctx, nbb=nbb),
        out_shape=jax.ShapeDtypeStruct((N_TOK, D_MODEL), F32),
        grid=(nb // nbb, nq),
        in_specs=in_specs,
        out_specs=xspec,
        scratch_shapes=[pltpu.VMEM((rb, D_MODEL), BF16)],
        input_output_aliases={0: 0},
        compiler_params=_cparams(("parallel", "arbitrary")),
        name="attn_ctx" if ctx else "attn_lat",
    )(*args)


def _block_ones(width, seg):
    idx = np.arange(width) // seg
    return jnp.asarray((idx[:, None] == idx[None, :]).astype(np.float32), dtype=BF16)


def _rot_matrix(width, half):
    m = np.zeros((width, width), np.float32)
    for j in range(width):
        if (j % (2 * half)) < half:
            m[j + half, j] = -1.0
        else:
            m[j - half, j] = 1.0
    return jnp.asarray(m, dtype=BF16)


def _rope_tables():
    t = np.arange(T_LAT)
    row = (t // GRID_W).astype(np.float32)
    col = (t % GRID_W).astype(np.float32)

    def tab(head_dim, n_heads):
        m = head_dim // 4
        freqs = ROPE_BASE ** (-jnp.arange(m, dtype=F32) / m)
        ang_r = jnp.asarray(row)[:, None] * freqs[None, :]
        ang_c = jnp.asarray(col)[:, None] * freqs[None, :]
        cs = jnp.concatenate([jnp.cos(ang_r)] * 2 + [jnp.cos(ang_c)] * 2, axis=1)
        sn = jnp.concatenate([jnp.sin(ang_r)] * 2 + [jnp.sin(ang_c)] * 2, axis=1)
        return jnp.tile(cs, (1, n_heads)), jnp.tile(sn, (1, n_heads))

    cosa, sina = tab(A_D, 2)
    cosc, sinc = tab(C_DQ, 2 * C_H)
    return cosa, sina, cosc, sinc


def kernel(x_prompt, x_sample, c, cache_a_k, cache_a_v, cache_c_k, cache_c_v, state_gla, c_ctx, w_ada, b_ada,
           g_norm, w_ffn_gate, w_ffn_up, w_ffn_down, w_in, g_a_q, g_a_k, w_gla_up, b_gla, g_gla, g_c_q, g_c_k,
           lam_c, g_c_out, w_out):
    lam_inits = [0.8 - 0.6 * math.exp(-0.3 * l) for l in range(DEPTH)]

    consts = {
        "seg64_384": _block_ones(384, 64), "seg64_128": _block_ones(128, 64),
        "seg48_384": _block_ones(384, 48), "seg96_384": _block_ones(384, 96),
        "seg64_256": _block_ones(256, 64),
        "rot_a384": _rot_matrix(384, 16), "rot_a128": _rot_matrix(128, 16), "rot_c384": _rot_matrix(384, 12),
    }
    rope_tabs = _rope_tables()

    wq = w_in[:, :, 0:384].reshape(DEPTH, D_MODEL, A_KV, A_G, A_D).transpose(0, 1, 3, 2, 4).reshape(DEPTH, D_MODEL, 384)
    seg = lambda a, b: w_in[:, :, a:b]
    w_p = jnp.concatenate([
        wq, seg(384, 512), seg(512, 640),
        seg(640, 768), seg(768, 896), seg(896, 1152),
        seg(1184, 1440),
        seg(1440, 1824), seg(1824, 2208), seg(2208, 2592),
        seg(1152, 1184), jnp.zeros((DEPTH, D_MODEL, 96), F32),
    ], axis=2).astype(BF16)
    wo_a = w_out[:, 0:384].reshape(DEPTH, A_KV, A_G, A_D, D_MODEL).transpose(0, 2, 1, 3, 4).reshape(DEPTH, 384, D_MODEL)
    wup = jnp.zeros((DEPTH, 128, 2 * B_KW), F32)
    wup = wup.at[:, 0:B_RANK, 0:B_KW].set(w_gla_up[:, 0]).at[:, B_RANK:2 * B_RANK, B_KW:].set(w_gla_up[:, 1])
    cond8 = jnp.zeros((8, D_MODEL), F32).at[0].set(c_ctx).at[1:3].set(c)
    prm = {
        "mod": _adaln(cond8, w_ada, b_ada),
        "gn4": g_norm.reshape(DEPTH, 3, 1, D_MODEL),
        "w_p": w_p,
        "w_o": jnp.concatenate([wo_a, w_out[:, 384:]], axis=1).astype(BF16),
        "wup": wup.astype(BF16),
        "bgla": b_gla.reshape(DEPTH, 1, 2 * B_KW),
        "gaq": jnp.tile(g_a_q, (1, 6)).reshape(DEPTH, 1, 384),
        "gak": jnp.tile(g_a_k, (1, 2)).reshape(DEPTH, 1, 128),
        "gcq": jnp.tile(g_c_q.reshape(DEPTH, 96), (1, 4)).reshape(DEPTH, 1, 384),
        "gck": jnp.tile(g_c_k.reshape(DEPTH, 96), (1, 4)).reshape(DEPTH, 1, 384),
        "gco": jnp.tile(g_c_out, (1, 4)).reshape(DEPTH, 1, 384),
        "ggl": jnp.tile(g_gla, (1, 4)).reshape(DEPTH, 1, 256),
        "lam_c": lam_c,
    }

    caches, states = None, None
    s0_lat = jnp.swapaxes(state_gla.reshape(NB_LAT, DEPTH, 2, B_KW, B_DV), 3, 4)
    kv_old = [a.reshape(NB_LAT, DEPTH, PAST_LEN, a.shape[-2] * a.shape[-1]).astype(BF16)
              for a in (cache_a_k, cache_a_v, cache_c_k, cache_c_v)]
    ffn = functools.partial(_ffn, mod=prm["mod"], gn4=prm["gn4"], wg=w_ffn_gate, wu=w_ffn_up, wd=w_ffn_down)
    x = None
    for l in range(DEPTH):
        if l == 0:
            x = ffn((x_prompt.reshape(N_CTX, D_MODEL), x_sample.reshape(N_LAT, D_MODEL)), l=0, s=0, first=True)
        else:
            x = ffn((x,), l=l, s=0)

        outs = _proj(x, prm, consts, None, caches, l=l, ctx=True)
        qa, ka, va, qc, kc, vc, gq, gk, gv, gla, gr = outs[:11]
        caches = list(outs[11:])
        ob, states = _gla(gq, gk, gv, gla, gr, None, prm, consts["seg64_256"], states, l=l, ctx=True)
        r3 = lambda a, n, t: a.reshape(n, t, a.shape[-1])
        x = _attn(x, qa, qc, ob, [r3(a, NB_CTX, T_CTX) for a in (ka, va, kc, vc)], None, prm, consts["seg96_384"],
                  l=l, lam_init=lam_inits[l], ctx=True)

        qa, ka, va, qc, kc, vc, gq, gk, gv, gla, gr = _proj(x, prm, consts, rope_tabs, None, l=l, ctx=False)
        ob, _ = _gla(gq, gk, gv, gla, gr, s0_lat[:, l], prm, consts["seg64_256"], None, l=l, ctx=False)
        x = _attn(x, qa, qc, ob, [r3(a, NB_LAT, T_LAT) for a in (ka, va, kc, vc)], kv_old, prm, consts["seg96_384"],
                  l=l, lam_init=lam_inits[l], ctx=False)

        if l == DEPTH - 1:
            y_prompt, y_sample = ffn((x,), l=l, s=1, last=True)
        else:
            x = ffn((x,), l=l, s=1)

    return (y_prompt.reshape(NB_CTX, T_CTX, D_MODEL), y_sample.reshape(NB_LAT, T_LAT, D_MODEL),
            caches[0].reshape(NB_CTX, DEPTH, T_CTX, A_KV, A_D), caches[1].reshape(NB_CTX, DEPTH, T_CTX, A_KV, A_D),
            caches[2].reshape(NB_CTX, DEPTH, T_CTX, C_H, 2 * C_DQ), caches[3].reshape(NB_CTX, DEPTH, T_CTX, C_H, C_DV),
            jnp.swapaxes(states, 3, 4).reshape(NB_CTX, DEPTH, 2, B_H, B_DK, B_DV))
```

```python
import functools
import math

import numpy as np
import jax
import jax.numpy as jnp
from jax import lax
from jax.experimental import pallas as pl
from jax.experimental.pallas import tpu as pltpu

F32 = jnp.float32
BF16 = jnp.bfloat16

D_MODEL = 1024
D_FF = 2816
DEPTH = 4
N_MOD = 9
EPS = 1e-6
ROPE_BASE = 10000.0
GRID_W = 64
LOG2E = math.log2(math.e)

NB_CTX, T_CTX = 32, 256
NB_LAT, T_LAT = 2, 2048
PAST_LEN = 512
N_CTX = NB_CTX * T_CTX
N_LAT = NB_LAT * T_LAT
N_TOK = N_CTX + N_LAT

A_KV, A_G, A_D = 2, 3, 64
A_W = A_KV * A_G * A_D
B_H, B_DK, B_DV = 4, 32, 64
B_KW = B_H * B_DK
B_VW = B_H * B_DV
B_RANK = 16
B_TAU = 16.0
B_CHUNK = 64
C_H, C_DQ, C_DV = 4, 48, 96
C_W = C_H * C_DV

P_AQ, P_AK, P_AV = 0, 384, 512
P_BQ, P_BK, P_BV, P_BR = 640, 768, 896, 1152
P_CQ, P_CK, P_CV = 1408, 1792, 2176
P_BG = 2560
P_W = 2688

TM_FFN = 1024
TF_FFN = 256
TM_PROJ = 256
TQ_ATT_LAT = 512
ATT_NBB_CTX = 4
TN_ADA = 1152
GLA_NBB_CTX = 8
GLA_NBB_LAT = 2
A_STACK_CTX, A_STACK_LAT = 6, 2
C_STACK_CTX, C_STACK_LAT = 4, 2
VMEM_LIMIT = 56 * 1024 * 1024


def _cparams(sem):
    return pltpu.CompilerParams(dimension_semantics=sem, vmem_limit_bytes=VMEM_LIMIT)


def _dot(a, b):
    return jnp.dot(a, b, preferred_element_type=F32)


def _dot_nt(a, b):
    return lax.dot_general(a, b, (((1,), (1,)), ((), ())), preferred_element_type=F32)


def _dot_tn(a, b):
    return lax.dot_general(a, b, (((0,), (0,)), ((), ())), preferred_element_type=F32)


def _split_dot(x, m):
    hi = x.astype(BF16)
    lo = (x - hi.astype(F32)).astype(BF16)
    return _dot(hi, m) + _dot(lo, m)


def _split_dot_l(m, x):
    hi = x.astype(BF16)
    lo = (x - hi.astype(F32)).astype(BF16)
    return _dot(m, hi) + _dot(m, lo)


def _seg_rms(x, seg, n):
    ms = _dot((x * x).astype(BF16), seg) * (1.0 / n)
    return x * lax.rsqrt(ms + EPS)


def _silu(x):
    return x * jax.nn.sigmoid(x)


def _norm_mod(x, gn, sc, sh):
    ms = jnp.mean(x * x, axis=-1, keepdims=True)
    return ((x * lax.rsqrt(ms + EPS)) * (gn * (1.0 + sc)) + sh).astype(BF16)


def _adaln_body(cond_ref, w_ref, b_ref, o_ref):
    c = cond_ref[...]
    sc = _silu(c).astype(BF16)
    o_ref[0] = _dot(sc, w_ref[0].astype(BF16)) + b_ref[0]


def _adaln(cond8, w_ada, b_ada):
    nj = (N_MOD * D_MODEL) // TN_ADA
    return pl.pallas_call(
        _adaln_body,
        out_shape=jax.ShapeDtypeStruct((DEPTH, 8, N_MOD * D_MODEL), F32),
        grid=(DEPTH, nj),
        in_specs=[
            pl.BlockSpec((8, D_MODEL), lambda l, j: (0, 0)),
            pl.BlockSpec((1, D_MODEL, TN_ADA), lambda l, j: (l, 0, j)),
            pl.BlockSpec((1, 1, TN_ADA), lambda l, j: (l, 0, j)),
        ],
        out_specs=pl.BlockSpec((1, 8, TN_ADA), lambda l, j: (l, 0, j)),
        compiler_params=_cparams(("parallel", "parallel")),
        name="adaln",
    )(cond8, w_ada, b_ada.reshape(DEPTH, 1, N_MOD * D_MODEL))


FFN_TILES_CTX = N_CTX // TM_FFN
FFN_TILES_PER_LAT = T_LAT // TM_FFN


def _ffn_body(*refs, first, last):
    it = iter(refs)
    x_refs = (next(it), next(it)) if first else (next(it),)
    sh_ref, sc_ref, gt_ref, gn_ref, wg_ref, wu_ref, wd_ref = (next(it) for _ in range(7))
    o_refs = (next(it), next(it)) if last else (next(it),)
    h_scr, acc_scr = next(it), next(it)

    i = pl.program_id(0)
    j = pl.program_id(1)
    is_ctx = i < FFN_TILES_CTX
    r = jnp.where(is_ctx, 0, 1 + jnp.maximum(i - FFN_TILES_CTX, 0) // FFN_TILES_PER_LAT)

    def on_tile(pred, n_variants, fn):
        if n_variants == 1:
            pl.when(pred)(lambda: fn(0))
        else:
            pl.when(pred & is_ctx)(lambda: fn(0))
            pl.when(pred & jnp.logical_not(is_ctx))(lambda: fn(1))

    def prologue(k):
        h_scr[...] = _norm_mod(x_refs[k][...], gn_ref[...], sc_ref[pl.ds(r, 1), :], sh_ref[pl.ds(r, 1), :])
        acc_scr[...] = jnp.zeros_like(acc_scr)

    on_tile(j == 0, len(x_refs), prologue)

    h = h_scr[...]
    g = _dot(h, wg_ref[...].astype(BF16))
    u = _dot(h, wu_ref[...].astype(BF16))
    a = (_silu(g) * u).astype(BF16)
    acc_scr[...] += _dot(a, wd_ref[...].astype(BF16))

    def epilogue(k):
        x_ref = x_refs[k if first else 0]
        o_ref = o_refs[k if last else 0]
        o_ref[...] = x_ref[...] + 0.5 * gt_ref[pl.ds(r, 1), :] * acc_scr[...]

    on_tile(j == pl.num_programs(1) - 1, max(len(x_refs), len(o_refs)), epilogue)


def _ffn(xs, mod, gn4, wg, wu, wd, *, l, s, first=False, last=False):
    ni = N_TOK // TM_FFN
    nj = D_FF // TF_FFN
    k0 = 6 * s
    gi = 2 * s
    tc = FFN_TILES_CTX
    split_specs = [pl.BlockSpec((TM_FFN, D_MODEL), lambda i, j: (jnp.minimum(i, tc - 1), 0)),
                   pl.BlockSpec((TM_FFN, D_MODEL), lambda i, j: (jnp.maximum(i - tc, 0), 0))]
    one_spec = [pl.BlockSpec((TM_FFN, D_MODEL), lambda i, j: (i, 0))]
    mspec = lambda k: pl.BlockSpec((None, 8, D_MODEL), lambda i, j: (l, 0, k))
    in_specs = (split_specs if first else one_spec) + [
        mspec(k0), mspec(k0 + 1), mspec(k0 + 2),
        pl.BlockSpec((None, None, 1, D_MODEL), lambda i, j: (l, gi, 0, 0)),
        pl.BlockSpec((None, None, D_MODEL, TF_FFN), lambda i, j: (l, s, 0, j)),
        pl.BlockSpec((None, None, D_MODEL, TF_FFN), lambda i, j: (l, s, 0, j)),
        pl.BlockSpec((None, None, TF_FFN, D_MODEL), lambda i, j: (l, s, j, 0)),
    ]
    if last:
        out_shape = [jax.ShapeDtypeStruct((N_CTX, D_MODEL), F32), jax.ShapeDtypeStruct((N_LAT, D_MODEL), F32)]
        out_specs = split_specs
    else:
        out_shape = jax.ShapeDtypeStruct((N_TOK, D_MODEL), F32)
        out_specs = one_spec[0]
    return pl.pallas_call(
        functools.partial(_ffn_body, first=first, last=last),
        out_shape=out_shape,
        grid=(ni, nj),
        in_specs=in_specs,
        out_specs=out_specs,
        scratch_shapes=[pltpu.VMEM((TM_FFN, D_MODEL), BF16), pltpu.VMEM((TM_FFN, D_MODEL), F32)],
        compiler_params=_cparams(("arbitrary", "arbitrary")),
        name="ffn",
    )(*xs, mod, mod, mod, gn4, wg, wu, wd)


def _proj_body(*refs, rope, ctx, tiles_per_batch, n_alias):
    it = iter(refs)
    x_ref, sh_ref, sc_ref, gn_ref, w_ref, wup_ref, bgla_ref = (next(it) for _ in range(7))
    gaq_ref, gak_ref, gcq_ref, gck_ref = (next(it) for _ in range(4))
    s64a_ref, s64k_ref, s48_ref = (next(it) for _ in range(3))
    if rope:
        cosa_ref, sina_ref, cosc_ref, sinc_ref, ra_ref, rak_ref, rc_ref = (next(it) for _ in range(7))
    for _ in range(n_alias):
        next(it)
    qa_o, ka_o, va_o, qc_o, kc_o, vc_o, gq_o, gk_o, gv_o, gla_o, gr_o = (next(it) for _ in range(11))
    if ctx:
        ka32_o, va32_o, kc32_o, vc32_o = (next(it) for _ in range(4))

    i = pl.program_id(0)
    r = 0 if ctx else 1 + i // tiles_per_batch

    h = _norm_mod(x_ref[...], gn_ref[...], sc_ref[pl.ds(r, 1), :], sh_ref[pl.ds(r, 1), :])
    p = _dot(h, w_ref[...])

    aq = _seg_rms(p[:, P_AQ:P_AQ + A_W], s64a_ref[...], A_D) * gaq_ref[...]
    ak = _seg_rms(p[:, P_AK:P_AK + 128], s64k_ref[...], A_D) * gak_ref[...]
    av = p[:, P_AV:P_AV + 128]
    cq = _seg_rms(p[:, P_CQ:P_CQ + 384], s48_ref[...], C_DQ) * gcq_ref[...]
    ck = _seg_rms(p[:, P_CK:P_CK + 384], s48_ref[...], C_DQ) * gck_ref[...]
    cv = p[:, P_CV:P_CV + 384]
    if ctx:
        for o, val in ((ka32_o, ak), (va32_o, av), (kc32_o, ck), (vc32_o, cv)):
            if n_alias:
                o[...] = val
            else:
                o[0] = val
                o[1:] = jnp.zeros((DEPTH - 1,) + val.shape, F32)
    if rope:
        cosa = cosa_ref[...]
        sina = sina_ref[...]
        cosa3 = jnp.concatenate([cosa, cosa, cosa], axis=1)
        sina3 = jnp.concatenate([sina, sina, sina], axis=1)
        aq = aq * cosa3 + _dot(aq.astype(BF16), ra_ref[...]) * sina3
        ak = ak * cosa + _dot(ak.astype(BF16), rak_ref[...]) * sina
        cosc = cosc_ref[...]
        sinc = sinc_ref[...]
        cq = cq * cosc + _dot(cq.astype(BF16), rc_ref[...]) * sinc
        ck = ck * cosc + _dot(ck.astype(BF16), rc_ref[...]) * sinc
    qa_o[...] = (aq * (A_D ** -0.5 * LOG2E)).astype(BF16)
    ka_o[...] = ak.astype(BF16)
    va_o[...] = av.astype(BF16)
    qc_o[...] = (cq * (C_DQ ** -0.5 * LOG2E)).astype(BF16)
    kc_o[...] = ck.astype(BF16)
    vc_o[...] = cv.astype(BF16)

    gq_o[...] = p[:, P_BQ:P_BQ + B_KW] * (B_DK ** -0.5)
    gk_o[...] = p[:, P_BK:P_BK + B_KW]
    gv_o[...] = p[:, P_BV:P_BV + B_VW]
    gr_o[...] = _silu(p[:, P_BR:P_BR + B_VW])
    z = _dot(p[:, P_BG:P_BG + 128].astype(BF16), wup_ref[...]) + bgla_ref[...]
    log_sig = jnp.minimum(z, 0.0) - jnp.log1p(jnp.exp(-jnp.abs(z)))
    gla_o[...] = log_sig * (1.0 / B_TAU)


def _proj(x, prm, consts, rope_tabs, caches, *, l, ctx):
    n = N_CTX if ctx else N_LAT
    off = 0 if ctx else N_CTX // TM_PROJ
    tiles_per_batch = (T_CTX if ctx else T_LAT) // TM_PROJ
    rope = not ctx
    full = lambda shape: pl.BlockSpec(shape, lambda i: (0,) * len(shape))
    lay = lambda shape: pl.BlockSpec((None,) + shape, lambda i: (l,) + (0,) * len(shape))
    in_specs = [
        pl.BlockSpec((TM_PROJ, D_MODEL), lambda i: (i + off, 0)),
        pl.BlockSpec((None, 8, D_MODEL), lambda i: (l, 0, 3)),
        pl.BlockSpec((None, 8, D_MODEL), lambda i: (l, 0, 4)),
        pl.BlockSpec((None, None, 1, D_MODEL), lambda i: (l, 1, 0, 0)),
        lay((D_MODEL, P_W)),
        lay((128, 2 * B_KW)),
        lay((1, 2 * B_KW)),
        lay((1, 384)), lay((1, 128)), lay((1, 384)), lay((1, 384)),
        full((384, 384)), full((128, 128)), full((384, 384)),
    ]
    args = [x, prm["mod"], prm["mod"], prm["gn4"], prm["w_p"], prm["wup"], prm["bgla"],
            prm["gaq"], prm["gak"], prm["gcq"], prm["gck"],
            consts["seg64_384"], consts["seg64_128"], consts["seg48_384"]]
    if rope:
        tpb = tiles_per_batch
        in_specs += [
            pl.BlockSpec((TM_PROJ, 128), lambda i: (i % tpb, 0)),
            pl.BlockSpec((TM_PROJ, 128), lambda i: (i % tpb, 0)),
            pl.BlockSpec((TM_PROJ, 384), lambda i: (i % tpb, 0)),
            pl.BlockSpec((TM_PROJ, 384), lambda i: (i % tpb, 0)),
            full((384, 384)), full((128, 128)), full((384, 384)),
        ]
        args += [*rope_tabs, consts["rot_a384"], consts["rot_a128"], consts["rot_c384"]]
    widths = [(384, BF16), (128, BF16), (128, BF16), (384, BF16), (384, BF16), (384, BF16),
              (B_KW, F32), (B_KW, F32), (B_VW, F32), (2 * B_KW, F32), (B_VW, F32)]
    out_shape = [jax.ShapeDtypeStruct((n, w), dt) for w, dt in widths]
    out_specs = [pl.BlockSpec((TM_PROJ, w), lambda i: (i, 0)) for w, _ in widths]
    aliases = {}
    if ctx:
        assert TM_PROJ == T_CTX
        for k, w in enumerate((128, 128, 384, 384)):
            if caches is not None:
                aliases[len(args)] = len(out_shape)
                in_specs.append(pl.BlockSpec(memory_space=pl.ANY))
                args.append(caches[k])
            out_shape.append(jax.ShapeDtypeStruct((NB_CTX, DEPTH, T_CTX, w), F32))
            if caches is not None:
                out_specs.append(pl.BlockSpec((None, None, T_CTX, w), lambda i: (i, l, 0, 0)))
            else:
                assert l == 0
                out_specs.append(pl.BlockSpec((None, DEPTH, T_CTX, w), lambda i: (i, 0, 0, 0)))
    return pl.pallas_call(
        functools.partial(_proj_body, rope=rope, ctx=ctx, tiles_per_batch=tiles_per_batch, n_alias=len(aliases)),
        out_shape=out_shape,
        grid=(n // TM_PROJ,),
        in_specs=in_specs,
        out_specs=out_specs,
        input_output_aliases=aliases,
        compiler_params=_cparams(("parallel",)),
        name="proj_ctx" if ctx else "proj_lat",
    )(*args)


def _gla_tools(gq_ref, gk_ref, gv_ref, gla_ref):
    C = B_CHUNK
    row_i = lax.broadcasted_iota(jnp.int32, (C, B_KW), 0)
    rk = lax.broadcasted_iota(jnp.int32, (B_H * C, B_KW), 0) >> 6
    ck = lax.broadcasted_iota(jnp.int32, (B_H * C, B_KW), 1) >> 5
    hm_k = rk == ck
    rv = lax.broadcasted_iota(jnp.int32, (B_H * C, B_VW), 0) >> 6
    cv = lax.broadcasted_iota(jnp.int32, (B_H * C, B_VW), 1) >> 6
    hm_v = rv == cv
    ra = lax.broadcasted_iota(jnp.int32, (C, B_H * C), 0)
    ca = lax.broadcasted_iota(jnp.int32, (C, B_H * C), 1) & (C - 1)
    tri = (ca <= ra, ca >= ra)
    zero16 = jnp.zeros((), BF16)

    def expand_state(st):
        return jnp.where(hm_k, jnp.concatenate([st] * B_H, axis=0), 0.0)

    def chunk(d, start, S):
        rows = pl.ds(start if isinstance(start, int) else pl.multiple_of(start, C), C)
        q = gq_ref[rows, :]
        k = gk_ref[rows, :]
        v = gv_ref[rows, :].astype(BF16)
        la = gla_ref[rows, d * B_KW:(d + 1) * B_KW]
        p = la
        for s in (1, 2, 4, 8, 16, 32):
            p = p + jnp.where(row_i >= s, pltpu.roll(p, s, axis=0), 0.0)
        tot = p[C - 1:C, :]
        b = p if d == 0 else tot - p + la
        e = b - b[C // 2:C // 2 + 1, :]
        qt = (q * jnp.exp(e)).astype(BF16)
        kt = (k * jnp.exp(-e)).astype(BF16)
        qd = (q * jnp.exp(b)).astype(BF16)
        kd = (k * jnp.exp(tot - b)).astype(BF16)
        kbd = jnp.where(hm_k, jnp.concatenate([kt] * B_H, axis=0), zero16)
        a = _dot_nt(qt, kbd)
        a = jnp.where(tri[d], a, 0.0).astype(BF16)
        vbd = jnp.where(hm_v, jnp.concatenate([v] * B_H, axis=0), zero16)
        o = _dot(a, vbd) + _dot_nt(qd, S.astype(BF16))
        kv = _dot_tn(v, kd)
        S_new = S * jnp.exp(tot) + jnp.where(hm_k, kv, 0.0)
        return rows, o, S_new

    def collapse_state(S):
        Sm = jnp.where(hm_k, S, 0.0)
        acc = Sm[0:C, :]
        for hh in range(1, B_H):
            acc = acc + Sm[hh * C:(hh + 1) * C, :]
        return acc

    return expand_state, chunk, collapse_state


def _gla_gate(o, seg, gg, gr):
    ms = _split_dot(o * o, seg) * (1.0 / B_DV)
    return (o * lax.rsqrt(ms + EPS) * gg * gr).astype(BF16)


def _gla_body(*refs, seq, nbb, has_s0, has_alias, zero_fill):
    it = iter(refs)
    gq_ref, gk_ref, gv_ref, gla_ref, gr_ref = (next(it) for _ in range(5))
    s0_ref = next(it) if has_s0 else None
    gg_ref, seg_ref = next(it), next(it)
    if has_alias:
        next(it)
    ob_ref, sfin_ref, of_scr, or_scr = (next(it) for _ in range(4))
    expand_state, chunk, collapse_state = _gla_tools(gq_ref, gk_ref, gv_ref, gla_ref)
    n = seq // B_CHUNK

    def step(i, carry):
        new = []
        for bb in range(nbb):
            rows, o, Sf = chunk(0, bb * seq + i * B_CHUNK, carry[2 * bb])
            of_scr[rows, :] = o
            rows, o, Sb = chunk(1, bb * seq + (n - 1 - i) * B_CHUNK, carry[2 * bb + 1])
            or_scr[rows, :] = o
            new += [Sf, Sb]
        return tuple(new)

    if has_s0:
        carry = tuple(expand_state(s0_ref[bb, d]) for bb in range(nbb) for d in range(2))
    else:
        carry = tuple(jnp.zeros((B_VW, B_KW), F32) for _ in range(2 * nbb))
    if n <= 4:
        for i in range(n):
            carry = step(i, carry)
    else:
        carry = lax.fori_loop(0, n, step, carry, unroll=2)
    for bb in range(nbb):
        if zero_fill:
            sfin_ref[bb, 0, 0] = collapse_state(carry[2 * bb])
            sfin_ref[bb, 0, 1] = collapse_state(carry[2 * bb + 1])
            sfin_ref[bb, 1:] = jnp.zeros((DEPTH - 1, 2, B_DV, B_KW), F32)
        else:
            sfin_ref[bb, 0] = collapse_state(carry[2 * bb])
            sfin_ref[bb, 1] = collapse_state(carry[2 * bb + 1])

    RT = 256

    def fin(t, _):
        rows = pl.ds(pl.multiple_of(t * RT, RT), RT)
        ob_ref[rows, :] = _gla_gate(of_scr[rows, :] + or_scr[rows, :], seg_ref[...], gg_ref[...], gr_ref[rows, :])
        return 0

    lax.fori_loop(0, nbb * seq // RT, fin, 0)


def _gla(gq, gk, gv, gla, gr, s0, prm, seg, states, *, l, ctx):
    nb, seq, nbb = (NB_CTX, T_CTX, GLA_NBB_CTX) if ctx else (NB_LAT, T_LAT, GLA_NBB_LAT)
    tok = lambda w: pl.BlockSpec((nbb * seq, w), lambda b: (b, 0))
    in_specs = [tok(B_KW), tok(B_KW), tok(B_VW), tok(2 * B_KW), tok(B_VW)]
    args = [gq, gk, gv, gla, gr]
    if s0 is not None:
        in_specs.append(pl.BlockSpec((nbb, 2, B_DV, B_KW), lambda b: (b, 0, 0, 0)))
        args.append(s0)
    in_specs += [pl.BlockSpec((None, 1, B_VW), lambda b: (l, 0, 0)), pl.BlockSpec((B_VW, B_VW), lambda b: (0, 0))]
    args += [prm["ggl"], seg]
    aliases = {}
    if ctx:
        st_shape = jax.ShapeDtypeStruct((NB_CTX, DEPTH, 2, B_DV, B_KW), F32)
        st_spec = pl.BlockSpec((nbb, None, 2, B_DV, B_KW), lambda b: (b, l, 0, 0, 0))
        if states is None:
            assert l == 0
            st_spec = pl.BlockSpec((nbb, DEPTH, 2, B_DV, B_KW), lambda b: (b, 0, 0, 0, 0))
        if states is not None:
            aliases[len(args)] = 1
            in_specs.append(pl.BlockSpec(memory_space=pl.ANY))
            args.append(states)
    else:
        st_shape = jax.ShapeDtypeStruct((nb, 2, B_DV, B_KW), F32)
        st_spec = pl.BlockSpec((nbb, 2, B_DV, B_KW), lambda b: (b, 0, 0, 0))
    return pl.pallas_call(
        functools.partial(_gla_body, seq=seq, nbb=nbb, has_s0=s0 is not None, has_alias=bool(aliases),
                          zero_fill=ctx and states is None),
        out_shape=[jax.ShapeDtypeStruct((nb * seq, B_VW), BF16), st_shape],
        grid=(nb // nbb,),
        in_specs=in_specs,
        out_specs=[tok(B_VW), st_spec],
        scratch_shapes=[pltpu.VMEM((nbb * seq, B_VW), F32), pltpu.VMEM((nbb * seq, B_VW), F32)],
        input_output_aliases=aliases,
        compiler_params=_cparams(("parallel",)),
        name="gla_ctx" if ctx else "gla_lat",
    )(*args)


def _attn_body(*refs, lam_init, ctx, nbb):
    it = iter(refs)
    x_ref, qa_ref, qc_ref, ob_ref = (next(it) for _ in range(4))
    kv_new = [next(it) for _ in range(4)]
    kv_old = None if ctx else [next(it) for _ in range(4)]
    wout_ref, gt_ref, lam_ref, gco_ref, seg_ref, o_ref, mix_scr = (next(it) for _ in range(7))

    def keys_values(bb):
        if ctx:
            return [r[bb] for r in kv_new]
        return [jnp.concatenate([rn[bb], ro[...]], axis=0) for rn, ro in zip(kv_new, kv_old)]

    b = pl.program_id(0)
    r = 0 if ctx else 1 + b
    tq = x_ref.shape[0] // nbb
    a_stack, c_stack = (A_STACK_CTX, C_STACK_CTX) if ctx else (A_STACK_LAT, C_STACK_LAT)
    lane = lax.broadcasted_iota(jnp.int32, (1, 128), 1)
    lane2 = lax.broadcasted_iota(jnp.int32, (1, 256), 1)
    hmask = [(lane >= hh * A_D) & (lane < (hh + 1) * A_D) for hh in range(A_KV)]
    lm = lam_ref[...]
    lam = (jnp.exp(jnp.sum(lm[0:1] * lm[1:2], axis=-1, keepdims=True))
           - jnp.exp(jnp.sum(lm[2:3] * lm[3:4], axis=-1, keepdims=True)) + lam_init)

    def softmax_terms(s):
        e = jnp.exp2(s - jnp.max(s, axis=-1, keepdims=True))
        return e, jnp.sum(e, axis=-1, keepdims=True)

    for bb in range(nbb):
        _attn_mix_rows(bb, tq, qa_ref, qc_ref, ob_ref, keys_values(bb), gco_ref, seg_ref, mix_scr,
                       a_stack, c_stack, hmask, lane2, lam, lam_init, softmax_terms)
    mixed = _dot(mix_scr[...], wout_ref[...])
    o_ref[...] = x_ref[...] + gt_ref[pl.ds(r, 1), :] * mixed


def _attn_mix_rows(bb, tq, qa_ref, qc_ref, ob_ref, kv, gco_ref, seg_ref, mix_scr,
                   a_stack, c_stack, hmask, lane2, lam, lam_init, softmax_terms):
    rows = slice(bb * tq, (bb + 1) * tq)
    zero16 = jnp.zeros((), BF16)
    ka, va, kc_all, vc_all = kv

    maps = [(g, hh) for g in range(A_G) for hh in range(A_KV)]
    acc = [jnp.zeros((tq, 128), F32) for _ in range(A_G)]
    for g0 in range(0, len(maps), a_stack):
        grp = maps[g0:g0 + a_stack]
        qs = jnp.concatenate(
            [jnp.where(hmask[hh], qa_ref[rows, g * 128:(g + 1) * 128], zero16) for g, hh in grp], axis=0)
        e, l = softmax_terms(_dot_nt(qs, ka))
        o = _dot(e.astype(BF16), va) * (1.0 / l)
        for k, (g, hh) in enumerate(grp):
            acc[g] = acc[g] + jnp.where(hmask[hh], o[k * tq:(k + 1) * tq], 0.0)
    for g in range(A_G):
        mix_scr[rows, g * 128:(g + 1) * 128] = acc[g].astype(BF16)

    mix_scr[rows, A_W:A_W + B_VW] = ob_ref[rows, :]

    outs = []
    for win in range(2):
        base = win * 128
        kc = kc_all[:, base:base + 256]
        vc = vc_all[:, base:base + 256]
        qc = qc_ref[rows, base:base + 256]
        ow = jnp.zeros((tq, 256), F32)
        cmaps = [(hh, mm) for hh in (2 * win, 2 * win + 1) for mm in range(2)]
        for g0 in range(0, len(cmaps), c_stack):
            grp = cmaps[g0:g0 + c_stack]
            qparts = []
            for hh, mm in grp:
                lo = (hh * 2 + mm) * C_DQ - base
                qparts.append(jnp.where((lane2 >= lo) & (lane2 < lo + C_DQ), qc, zero16))
            e, l = softmax_terms(_dot_nt(jnp.concatenate(qparts, axis=0), kc))
            ws, invs = [], []
            for k in range(0, len(grp), 2):
                l0 = l[k * tq:(k + 1) * tq]
                l1 = l[(k + 1) * tq:(k + 2) * tq]
                ws.append((e[k * tq:(k + 1) * tq] - (lam * l0 / l1) * e[(k + 1) * tq:(k + 2) * tq]).astype(BF16))
                invs.append(1.0 / l0)
            rr = _dot(jnp.concatenate(ws, axis=0) if len(ws) > 1 else ws[0], vc)
            for k in range(len(ws)):
                hh = grp[2 * k][0]
                vlo = hh * C_DV - base
                vm = (lane2 >= vlo) & (lane2 < vlo + C_DV)
                ow = ow + jnp.where(vm, rr[k * tq:(k + 1) * tq] * invs[k], 0.0)
        outs.append(ow)
    oc = jnp.concatenate([outs[0][:, :128], outs[0][:, 128:] + outs[1][:, :128], outs[1][:, 128:]], axis=1)
    ms = _dot((oc * oc).astype(BF16), seg_ref[...]) * (1.0 / C_DV)
    oc = oc * lax.rsqrt(ms + EPS) * gco_ref[...] * (1.0 - lam_init)
    mix_scr[rows, A_W + B_VW:] = oc.astype(BF16)


def _attn(x, qa, qc, ob, kv_new, kv_old, prm, seg96, *, l, lam_init, ctx):
    nb, seq, nbb, tq = (NB_CTX, T_CTX, ATT_NBB_CTX, T_CTX) if ctx else (NB_LAT, T_LAT, 1, TQ_ATT_LAT)
    nq = seq // tq
    rb = nbb * tq
    off = 0 if ctx else N_CTX // rb
    qspec = lambda w: pl.BlockSpec((rb, w), lambda b, q: (b * nq + q, 0))
    one_buf = None if ctx else pl.Buffered(1)
    kspec = lambda a: pl.BlockSpec((nbb,) + a.shape[1:], lambda b, q: (b, 0, 0), pipeline_mode=one_buf)
    ospec = lambda a: pl.BlockSpec((None, None) + a.shape[2:], lambda b, q: (b, l, 0, 0), pipeline_mode=one_buf)
    lay = lambda shape: pl.BlockSpec((None,) + shape, lambda b, q: (l,) + (0,) * len(shape),
                                     pipeline_mode=pl.Buffered(1))
    xspec = pl.BlockSpec((rb, D_MODEL), lambda b, q: (off + b * nq + q, 0))
    in_specs = [xspec, qspec(384), qspec(384), qspec(B_VW)] + [kspec(a) for a in kv_new]
    args = [x, qa, qc, ob, *kv_new]
    if kv_old is not None:
        in_specs += [ospec(a) for a in kv_old]
        args += list(kv_old)
    in_specs += [lay((D_MODEL, D_MODEL)),
                 pl.BlockSpec((None, 8, D_MODEL), lambda b, q: (l, 0, 5)),
                 lay((4, C_DQ)), lay((1, 384)),
                 pl.BlockSpec((384, 384), lambda b, q: (0, 0))]
    args += [prm["w_o"], prm["mod"], prm["lam_c"], prm["gco"], seg96]
    return pl.pallas_call(
        functools.partial(_attn_body, lam_init=lam_init, ctx=ctx, nbb=nbb),
        out_shape=jax.ShapeDtypeStruct((N_TOK, D_MODEL), F32),
        grid=(nb // nbb, nq),
        in_specs=in_specs,
        out_specs=xspec,
        scratch_shapes=[pltpu.VMEM((rb, D_MODEL), BF16)],
        input_output_aliases={0: 0},
        compiler_params=_cparams(("parallel", "arbitrary")),
        name="attn_ctx" if ctx else "attn_lat",
    )(*args)


def _block_ones(width, seg):
    idx = np.arange(width) // seg
    return jnp.asarray((idx[:, None] == idx[None, :]).astype(np.float32), dtype=BF16)


def _rot_matrix(width, half):
    m = np.zeros((width, width), np.float32)
    for j in range(width):
        if (j % (2 * half)) < half:
            m[j + half, j] = -1.0
        else:
            m[j - half, j] = 1.0
    return jnp.asarray(m, dtype=BF16)


def _rope_tables():
    t = np.arange(T_LAT)
    row = (t // GRID_W).astype(np.float32)
    col = (t % GRID_W).astype(np.float32)

    def tab(head_dim, n_heads):
        m = head_dim // 4
        freqs = ROPE_BASE ** (-jnp.arange(m, dtype=F32) / m)
        ang_r = jnp.asarray(row)[:, None] * freqs[None, :]
        ang_c = jnp.asarray(col)[:, None] * freqs[None, :]
        cs = jnp.concatenate([jnp.cos(ang_r)] * 2 + [jnp.cos(ang_c)] * 2, axis=1)
        sn = jnp.concatenate([jnp.sin(ang_r)] * 2 + [jnp.sin(ang_c)] * 2, axis=1)
        return jnp.tile(cs, (1, n_heads)), jnp.tile(sn, (1, n_heads))

    cosa, sina = tab(A_D, 2)
    cosc, sinc = tab(C_DQ, 2 * C_H)
    return cosa, sina, cosc, sinc


def kernel(x_prompt, x_sample, c, cache_a_k, cache_a_v, cache_c_k, cache_c_v, state_gla, c_ctx, w_ada, b_ada,
           g_norm, w_ffn_gate, w_ffn_up, w_ffn_down, w_in, g_a_q, g_a_k, w_gla_up, b_gla, g_gla, g_c_q, g_c_k,
           lam_c, g_c_out, w_out):
    lam_inits = [0.8 - 0.6 * math.exp(-0.3 * l) for l in range(DEPTH)]

    consts = {
        "seg64_384": _block_ones(384, 64), "seg64_128": _block_ones(128, 64),
        "seg48_384": _block_ones(384, 48), "seg96_384": _block_ones(384, 96),
        "seg64_256": _block_ones(256, 64),
        "rot_a384": _rot_matrix(384, 16), "rot_a128": _rot_matrix(128, 16), "rot_c384": _rot_matrix(384, 12),
    }
    rope_tabs = _rope_tables()

    wq = w_in[:, :, 0:384].reshape(DEPTH, D_MODEL, A_KV, A_G, A_D).transpose(0, 1, 3, 2, 4).reshape(DEPTH, D_MODEL, 384)
    seg = lambda a, b: w_in[:, :, a:b]
    w_p = jnp.concatenate([
        wq, seg(384, 512), seg(512, 640),
        seg(640, 768), seg(768, 896), seg(896, 1152),
        seg(1184, 1440),
        seg(1440, 1824), seg(1824, 2208), seg(2208, 2592),
        seg(1152, 1184), jnp.zeros((DEPTH, D_MODEL, 96), F32),
    ], axis=2).astype(BF16)
    wo_a = w_out[:, 0:384].reshape(DEPTH, A_KV, A_G, A_D, D_MODEL).transpose(0, 2, 1, 3, 4).reshape(DEPTH, 384, D_MODEL)
    wup = jnp.zeros((DEPTH, 128, 2 * B_KW), F32)
    wup = wup.at[:, 0:B_RANK, 0:B_KW].set(w_gla_up[:, 0]).at[:, B_RANK:2 * B_RANK, B_KW:].set(w_gla_up[:, 1])
    cond8 = jnp.zeros((8, D_MODEL), F32).at[0].set(c_ctx).at[1:3].set(c)
    prm = {
        "mod": _adaln(cond8, w_ada, b_ada),
        "gn4": g_norm.reshape(DEPTH, 3, 1, D_MODEL),
        "w_p": w_p,
        "w_o": jnp.concatenate([wo_a, w_out[:, 384:]], axis=1).astype(BF16),
        "wup": wup.astype(BF16),
        "bgla": b_gla.reshape(DEPTH, 1, 2 * B_KW),
        "gaq": jnp.tile(g_a_q, (1, 6)).reshape(DEPTH, 1, 384),
        "gak": jnp.tile(g_a_k, (1, 2)).reshape(DEPTH, 1, 128),
        "gcq": jnp.tile(g_c_q.reshape(DEPTH, 96), (1, 4)).reshape(DEPTH, 1, 384),
        "gck": jnp.tile(g_c_k.reshape(DEPTH, 96), (1, 4)).reshape(DEPTH, 1, 384),
        "gco": jnp.tile(g_c_out, (1, 4)).reshape(DEPTH, 1, 384),
        "ggl": jnp.tile(g_gla, (1, 4)).reshape(DEPTH, 1, 256),
        "lam_c": lam_c,
    }

    caches, states = None, None
    s0_lat = jnp.swapaxes(state_gla.reshape(NB_LAT, DEPTH, 2, B_KW, B_DV), 3, 4)
    kv_old = [a.reshape(NB_LAT, DEPTH, PAST_LEN, a.shape[-2] * a.shape[-1]).astype(BF16)
              for a in (cache_a_k, cache_a_v, cache_c_k, cache_c_v)]
    ffn = functools.partial(_ffn, mod=prm["mod"], gn4=prm["gn4"], wg=w_ffn_gate, wu=w_ffn_up, wd=w_ffn_down)
    x = None
    for l in range(DEPTH):
        if l == 0:
            x = ffn((x_prompt.reshape(N_CTX, D_MODEL), x_sample.reshape(N_LAT, D_MODEL)), l=0, s=0, first=True)
        else:
            x = ffn((x,), l=l, s=0)

        outs = _proj(x, prm, consts, None, caches, l=l, ctx=True)
        qa, ka, va, qc, kc, vc, gq, gk, gv, gla, gr = outs[:11]
        caches = list(outs[11:])
        ob, states = _gla(gq, gk, gv, gla, gr, None, prm, consts["seg64_256"], states, l=l, ctx=True)
        r3 = lambda a, n, t: a.reshape(n, t, a.shape[-1])
        x = _attn(x, qa, qc, ob, [r3(a, NB_CTX, T_CTX) for a in (ka, va, kc, vc)], None, prm, consts["seg96_384"],
                  l=l, lam_init=lam_inits[l], ctx=True)

        qa, ka, va, qc, kc, vc, gq, gk, gv, gla, gr = _proj(x, prm, consts, rope_tabs, None, l=l, ctx=False)
        ob, _ = _gla(gq, gk, gv, gla, gr, s0_lat[:, l], prm, consts["seg64_256"], None, l=l, ctx=False)
        x = _attn(x, qa, qc, ob, [r3(a, NB_LAT, T_LAT) for a in (ka, va, kc, vc)], kv_old, prm, consts["seg96_384"],
                  l=l, lam_init=lam_inits[l], ctx=False)

        if l == DEPTH - 1:
            y_prompt, y_sample = ffn((x,), l=l, s=1, last=True)
        else:
            x = ffn((x,), l=l, s=1)

    return (y_prompt.reshape(NB_CTX, T_CTX, D_MODEL), y_sample.reshape(NB_LAT, T_LAT, D_MODEL),
            caches[0].reshape(NB_CTX, DEPTH, T_CTX, A_KV, A_D), caches[1].reshape(NB_CTX, DEPTH, T_CTX, A_KV, A_D),
            caches[2].reshape(NB_CTX, DEPTH, T_CTX, C_H, 2 * C_DQ), caches[3].reshape(NB_CTX, DEPTH, T_CTX, C_H, C_DV),
            jnp.swapaxes(states, 3, 4).reshape(NB_CTX, DEPTH, 2, B_H, B_DK, B_DV))
```

```python
import functools
import math

import numpy as np
import jax
import jax.numpy as jnp
from jax import lax
from jax.experimental import pallas as pl
from jax.experimental.pallas import tpu as pltpu

F32 = jnp.float32
BF16 = jnp.bfloat16

D_MODEL = 1024
D_FF = 2816
DEPTH = 4
N_MOD = 9
EPS = 1e-6
ROPE_BASE = 10000.0
GRID_W = 64
LOG2E = math.log2(math.e)

NB_CTX, T_CTX = 32, 256
NB_LAT, T_LAT = 2, 2048
PAST_LEN = 512
N_CTX = NB_CTX * T_CTX
N_LAT = NB_LAT * T_LAT
N_TOK = N_CTX + N_LAT

A_KV, A_G, A_D = 2, 3, 64
A_W = A_KV * A_G * A_D
B_H, B_DK, B_DV = 4, 32, 64
B_KW = B_H * B_DK
B_VW = B_H * B_DV
B_RANK = 16
B_TAU = 16.0
B_CHUNK = 64
C_H, C_DQ, C_DV = 4, 48, 96
C_W = C_H * C_DV

P_AQ, P_AK, P_AV = 0, 384, 512
P_BQ, P_BK, P_BV, P_BR = 640, 768, 896, 1152
P_CQ, P_CK, P_CV = 1408, 1792, 2176
P_BG = 2560
P_W = 2688

TM_FFN = 1024
TF_FFN = 256
TM_PROJ_CTX = 256
TM_PROJ_LAT = 512
TQ_ATT_LAT = 512
ATT_NBB_CTX = 4
TN_ADA = 1152
GLA_NBB_CTX = 8
GLA_NBB_LAT = 2
A_STACK_CTX, A_STACK_LAT = 6, 2
C_STACK_CTX, C_STACK_LAT = 4, 2
VMEM_LIMIT = 56 * 1024 * 1024


def _cparams(sem):
    return pltpu.CompilerParams(dimension_semantics=sem, vmem_limit_bytes=VMEM_LIMIT)


def _dot(a, b):
    return jnp.dot(a, b, preferred_element_type=F32)


def _dot_nt(a, b):
    return lax.dot_general(a, b, (((1,), (1,)), ((), ())), preferred_element_type=F32)


def _dot_tn(a, b):
    return lax.dot_general(a, b, (((0,), (0,)), ((), ())), preferred_element_type=F32)


def _split_dot(x, m):
    hi = x.astype(BF16)
    lo = (x - hi.astype(F32)).astype(BF16)
    return _dot(hi, m) + _dot(lo, m)


def _split_dot_l(m, x):
    hi = x.astype(BF16)
    lo = (x - hi.astype(F32)).astype(BF16)
    return _dot(m, hi) + _dot(m, lo)


def _seg_rms(x, seg, n):
    ms = _dot((x * x).astype(BF16), seg) * (1.0 / n)
    return x * lax.rsqrt(ms + EPS)


def _silu(x):
    return x * jax.nn.sigmoid(x)


def _norm_mod(x, gn, sc, sh):
    ms = jnp.mean(x * x, axis=-1, keepdims=True)
    return ((x * lax.rsqrt(ms + EPS)) * (gn * (1.0 + sc)) + sh).astype(BF16)


def _adaln_body(cond_ref, w_ref, b_ref, o_ref):
    c = cond_ref[...]
    sc = _silu(c).astype(BF16)
    o_ref[0] = _dot(sc, w_ref[0].astype(BF16)) + b_ref[0]


def _adaln(cond8, w_ada, b_ada):
    nj = (N_MOD * D_MODEL) // TN_ADA
    return pl.pallas_call(
        _adaln_body,
        out_shape=jax.ShapeDtypeStruct((DEPTH, 8, N_MOD * D_MODEL), F32),
        grid=(DEPTH, nj),
        in_specs=[
            pl.BlockSpec((8, D_MODEL), lambda l, j: (0, 0)),
            pl.BlockSpec((1, D_MODEL, TN_ADA), lambda l, j: (l, 0, j)),
            pl.BlockSpec((1, 1, TN_ADA), lambda l, j: (l, 0, j)),
        ],
        out_specs=pl.BlockSpec((1, 8, TN_ADA), lambda l, j: (l, 0, j)),
        compiler_params=_cparams(("parallel", "parallel")),
        name="adaln",
    )(cond8, w_ada, b_ada.reshape(DEPTH, 1, N_MOD * D_MODEL))


FFN_TILES_CTX = N_CTX // TM_FFN
FFN_TILES_PER_LAT = T_LAT // TM_FFN


def _ffn_body(*refs, first, last):
    it = iter(refs)
    x_refs = (next(it), next(it)) if first else (next(it),)
    sh_ref, sc_ref, gt_ref, gn_ref, wg_ref, wu_ref, wd_ref = (next(it) for _ in range(7))
    o_refs = (next(it), next(it)) if last else (next(it),)
    h_scr, acc_scr = next(it), next(it)

    i = pl.program_id(0)
    j = pl.program_id(1)
    is_ctx = i < FFN_TILES_CTX
    r = jnp.where(is_ctx, 0, 1 + jnp.maximum(i - FFN_TILES_CTX, 0) // FFN_TILES_PER_LAT)

    def on_tile(pred, n_variants, fn):
        if n_variants == 1:
            pl.when(pred)(lambda: fn(0))
        else:
            pl.when(pred & is_ctx)(lambda: fn(0))
            pl.when(pred & jnp.logical_not(is_ctx))(lambda: fn(1))

    def prologue(k):
        h_scr[...] = _norm_mod(x_refs[k][...], gn_ref[...], sc_ref[pl.ds(r, 1), :], sh_ref[pl.ds(r, 1), :])
        acc_scr[...] = jnp.zeros_like(acc_scr)

    on_tile(j == 0, len(x_refs), prologue)

    h = h_scr[...]
    g = _dot(h, wg_ref[...].astype(BF16))
    u = _dot(h, wu_ref[...].astype(BF16))
    a = (_silu(g) * u).astype(BF16)
    acc_scr[...] += _dot(a, wd_ref[...].astype(BF16))

    def epilogue(k):
        x_ref = x_refs[k if first else 0]
        o_ref = o_refs[k if last else 0]
        o_ref[...] = x_ref[...] + 0.5 * gt_ref[pl.ds(r, 1), :] * acc_scr[...]

    on_tile(j == pl.num_programs(1) - 1, max(len(x_refs), len(o_refs)), epilogue)


def _ffn(xs, mod, gn4, wg, wu, wd, *, l, s, first=False, last=False):
    ni = N_TOK // TM_FFN
    nj = D_FF // TF_FFN
    k0 = 6 * s
    gi = 2 * s
    tc = FFN_TILES_CTX
    split_specs = [pl.BlockSpec((TM_FFN, D_MODEL), lambda i, j: (jnp.minimum(i, tc - 1), 0)),
                   pl.BlockSpec((TM_FFN, D_MODEL), lambda i, j: (jnp.maximum(i - tc, 0), 0))]
    one_spec = [pl.BlockSpec((TM_FFN, D_MODEL), lambda i, j: (i, 0))]
    mspec = lambda k: pl.BlockSpec((None, 8, D_MODEL), lambda i, j: (l, 0, k))
    in_specs = (split_specs if first else one_spec) + [
        mspec(k0), mspec(k0 + 1), mspec(k0 + 2),
        pl.BlockSpec((None, None, 1, D_MODEL), lambda i, j: (l, gi, 0, 0)),
        pl.BlockSpec((None, None, D_MODEL, TF_FFN), lambda i, j: (l, s, 0, j)),
        pl.BlockSpec((None, None, D_MODEL, TF_FFN), lambda i, j: (l, s, 0, j)),
        pl.BlockSpec((None, None, TF_FFN, D_MODEL), lambda i, j: (l, s, j, 0)),
    ]
    if last:
        out_shape = [jax.ShapeDtypeStruct((N_CTX, D_MODEL), F32), jax.ShapeDtypeStruct((N_LAT, D_MODEL), F32)]
        out_specs = split_specs
    else:
        out_shape = jax.ShapeDtypeStruct((N_TOK, D_MODEL), F32)
        out_specs = one_spec[0]
    return pl.pallas_call(
        functools.partial(_ffn_body, first=first, last=last),
        out_shape=out_shape,
        grid=(ni, nj),
        in_specs=in_specs,
        out_specs=out_specs,
        scratch_shapes=[pltpu.VMEM((TM_FFN, D_MODEL), BF16), pltpu.VMEM((TM_FFN, D_MODEL), F32)],
        compiler_params=_cparams(("arbitrary", "arbitrary")),
        name="ffn",
    )(*xs, mod, mod, mod, gn4, wg, wu, wd)


def _proj_body(*refs, rope, ctx, tiles_per_batch, n_alias):
    it = iter(refs)
    x_ref, sh_ref, sc_ref, gn_ref, w_ref, wup_ref, bgla_ref = (next(it) for _ in range(7))
    gaq_ref, gak_ref, gcq_ref, gck_ref = (next(it) for _ in range(4))
    s64a_ref, s64k_ref, s48_ref = (next(it) for _ in range(3))
    if rope:
        cosa_ref, sina_ref, cosc_ref, sinc_ref, ra_ref, rak_ref, rc_ref = (next(it) for _ in range(7))
    for _ in range(n_alias):
        next(it)
    qa_o, ka_o, va_o, qc_o, kc_o, vc_o, gq_o, gk_o, gv_o, gla_o, gr_o = (next(it) for _ in range(11))
    if ctx:
        ka32_o, va32_o, kc32_o, vc32_o = (next(it) for _ in range(4))

    i = pl.program_id(0)
    r = 0 if ctx else 1 + i // tiles_per_batch

    h = _norm_mod(x_ref[...], gn_ref[...], sc_ref[pl.ds(r, 1), :], sh_ref[pl.ds(r, 1), :])
    p = _dot(h, w_ref[...])

    aq = _seg_rms(p[:, P_AQ:P_AQ + A_W], s64a_ref[...], A_D) * gaq_ref[...]
    ak = _seg_rms(p[:, P_AK:P_AK + 128], s64k_ref[...], A_D) * gak_ref[...]
    av = p[:, P_AV:P_AV + 128]
    cq = _seg_rms(p[:, P_CQ:P_CQ + 384], s48_ref[...], C_DQ) * gcq_ref[...]
    ck = _seg_rms(p[:, P_CK:P_CK + 384], s48_ref[...], C_DQ) * gck_ref[...]
    cv = p[:, P_CV:P_CV + 384]
    if ctx:
        for o, val in ((ka32_o, ak), (va32_o, av), (kc32_o, ck), (vc32_o, cv)):
            if n_alias:
                o[...] = val
            else:
                o[0] = val
                o[1:] = jnp.zeros((DEPTH - 1,) + val.shape, F32)
    if rope:
        cosa = cosa_ref[...]
        sina = sina_ref[...]
        cosa3 = jnp.concatenate([cosa, cosa, cosa], axis=1)
        sina3 = jnp.concatenate([sina, sina, sina], axis=1)
        aq = aq * cosa3 + _dot(aq.astype(BF16), ra_ref[...]) * sina3
        ak = ak * cosa + _dot(ak.astype(BF16), rak_ref[...]) * sina
        cosc = cosc_ref[...]
        sinc = sinc_ref[...]
        cq = cq * cosc + _dot(cq.astype(BF16), rc_ref[...]) * sinc
        ck = ck * cosc + _dot(ck.astype(BF16), rc_ref[...]) * sinc
    qa_o[...] = (aq * (A_D ** -0.5 * LOG2E)).astype(BF16)
    ka_o[...] = ak.astype(BF16)
    va_o[...] = av.astype(BF16)
    qc_o[...] = (cq * (C_DQ ** -0.5 * LOG2E)).astype(BF16)
    kc_o[...] = ck.astype(BF16)
    vc_o[...] = cv.astype(BF16)

    gq_o[...] = p[:, P_BQ:P_BQ + B_KW] * (B_DK ** -0.5)
    gk_o[...] = p[:, P_BK:P_BK + B_KW]
    gv_o[...] = p[:, P_BV:P_BV + B_VW]
    gr_o[...] = _silu(p[:, P_BR:P_BR + B_VW])
    z = _dot(p[:, P_BG:P_BG + 128].astype(BF16), wup_ref[...]) + bgla_ref[...]
    log_sig = jnp.minimum(z, 0.0) - jnp.log1p(jnp.exp(-jnp.abs(z)))
    gla_o[...] = log_sig * (1.0 / B_TAU)


def _proj(x, prm, consts, rope_tabs, caches, *, l, ctx):
    n = N_CTX if ctx else N_LAT
    tm = TM_PROJ_CTX if ctx else TM_PROJ_LAT
    off = 0 if ctx else N_CTX // tm
    tiles_per_batch = (T_CTX if ctx else T_LAT) // tm
    rope = not ctx
    full = lambda shape: pl.BlockSpec(shape, lambda i: (0,) * len(shape))
    lay = lambda shape: pl.BlockSpec((None,) + shape, lambda i: (l,) + (0,) * len(shape))
    in_specs = [
        pl.BlockSpec((tm, D_MODEL), lambda i: (i + off, 0)),
        pl.BlockSpec((None, 8, D_MODEL), lambda i: (l, 0, 3)),
        pl.BlockSpec((None, 8, D_MODEL), lambda i: (l, 0, 4)),
        pl.BlockSpec((None, None, 1, D_MODEL), lambda i: (l, 1, 0, 0)),
        lay((D_MODEL, P_W)),
        lay((128, 2 * B_KW)),
        lay((1, 2 * B_KW)),
        lay((1, 384)), lay((1, 128)), lay((1, 384)), lay((1, 384)),
        full((384, 384)), full((128, 128)), full((384, 384)),
    ]
    args = [x, prm["mod"], prm["mod"], prm["gn4"], prm["w_p"], prm["wup"], prm["bgla"],
            prm["gaq"], prm["gak"], prm["gcq"], prm["gck"],
            consts["seg64_384"], consts["seg64_128"], consts["seg48_384"]]
    if rope:
        tpb = tiles_per_batch
        in_specs += [
            pl.BlockSpec((tm, 128), lambda i: (i % tpb, 0)),
            pl.BlockSpec((tm, 128), lambda i: (i % tpb, 0)),
            pl.BlockSpec((tm, 384), lambda i: (i % tpb, 0)),
            pl.BlockSpec((tm, 384), lambda i: (i % tpb, 0)),
            full((384, 384)), full((128, 128)), full((384, 384)),
        ]
        args += [*rope_tabs, consts["rot_a384"], consts["rot_a128"], consts["rot_c384"]]
    widths = [(384, BF16), (128, BF16), (128, BF16), (384, BF16), (384, BF16), (384, BF16),
              (B_KW, F32), (B_KW, F32), (B_VW, F32), (2 * B_KW, F32), (B_VW, F32)]
    out_shape = [jax.ShapeDtypeStruct((n, w), dt) for w, dt in widths]
    out_specs = [pl.BlockSpec((tm, w), lambda i: (i, 0)) for w, _ in widths]
    aliases = {}
    if ctx:
        assert tm == T_CTX
        for k, w in enumerate((128, 128, 384, 384)):
            if caches is not None:
                aliases[len(args)] = len(out_shape)
                in_specs.append(pl.BlockSpec(memory_space=pl.ANY))
                args.append(caches[k])
            out_shape.append(jax.ShapeDtypeStruct((NB_CTX, DEPTH, T_CTX, w), F32))
            if caches is not None:
                out_specs.append(pl.BlockSpec((None, None, T_CTX, w), lambda i: (i, l, 0, 0)))
            else:
                assert l == 0
                out_specs.append(pl.BlockSpec((None, DEPTH, T_CTX, w), lambda i: (i, 0, 0, 0)))
    return pl.pallas_call(
        functools.partial(_proj_body, rope=rope, ctx=ctx, tiles_per_batch=tiles_per_batch, n_alias=len(aliases)),
        out_shape=out_shape,
        grid=(n // tm,),
        in_specs=in_specs,
        out_specs=out_specs,
        input_output_aliases=aliases,
        compiler_params=_cparams(("parallel",)),
        name="proj_ctx" if ctx else "proj_lat",
    )(*args)


def _gla_tools(gq_ref, gk_ref, gv_ref, gla_ref):
    C = B_CHUNK
    row_i = lax.broadcasted_iota(jnp.int32, (C, B_KW), 0)
    rk = lax.broadcasted_iota(jnp.int32, (B_H * C, B_KW), 0) >> 6
    ck = lax.broadcasted_iota(jnp.int32, (B_H * C, B_KW), 1) >> 5
    hm_k = rk == ck
    rv = lax.broadcasted_iota(jnp.int32, (B_H * C, B_VW), 0) >> 6
    cv = lax.broadcasted_iota(jnp.int32, (B_H * C, B_VW), 1) >> 6
    hm_v = rv == cv
    ra = lax.broadcasted_iota(jnp.int32, (C, B_H * C), 0)
    ca = lax.broadcasted_iota(jnp.int32, (C, B_H * C), 1) & (C - 1)
    tri = (ca <= ra, ca >= ra)
    zero16 = jnp.zeros((), BF16)

    def expand_state(st):
        return jnp.where(hm_k, jnp.concatenate([st] * B_H, axis=0), 0.0)

    def chunk(d, start, S):
        rows = pl.ds(start if isinstance(start, int) else pl.multiple_of(start, C), C)
        q = gq_ref[rows, :]
        k = gk_ref[rows, :]
        v = gv_ref[rows, :].astype(BF16)
        la = gla_ref[rows, d * B_KW:(d + 1) * B_KW]
        p = la
        for s in (1, 2, 4, 8, 16, 32):
            p = p + jnp.where(row_i >= s, pltpu.roll(p, s, axis=0), 0.0)
        tot = p[C - 1:C, :]
        b = p if d == 0 else tot - p + la
        e = b - b[C // 2:C // 2 + 1, :]
        qt = (q * jnp.exp(e)).astype(BF16)
        kt = (k * jnp.exp(-e)).astype(BF16)
        qd = (q * jnp.exp(b)).astype(BF16)
        kd = (k * jnp.exp(tot - b)).astype(BF16)
        kbd = jnp.where(hm_k, jnp.concatenate([kt] * B_H, axis=0), zero16)
        a = _dot_nt(qt, kbd)
        a = jnp.where(tri[d], a, 0.0).astype(BF16)
        vbd = jnp.where(hm_v, jnp.concatenate([v] * B_H, axis=0), zero16)
        o = _dot(a, vbd) + _dot_nt(qd, S.astype(BF16))
        kv = _dot_tn(v, kd)
        S_new = S * jnp.exp(tot) + jnp.where(hm_k, kv, 0.0)
        return rows, o, S_new

    def collapse_state(S):
        Sm = jnp.where(hm_k, S, 0.0)
        acc = Sm[0:C, :]
        for hh in range(1, B_H):
            acc = acc + Sm[hh * C:(hh + 1) * C, :]
        return acc

    return expand_state, chunk, collapse_state


def _gla_gate(o, seg, gg, gr):
    ms = _split_dot(o * o, seg) * (1.0 / B_DV)
    return (o * lax.rsqrt(ms + EPS) * gg * gr).astype(BF16)


def _gla_body(*refs, seq, nbb, has_s0, has_alias, zero_fill):
    it = iter(refs)
    gq_ref, gk_ref, gv_ref, gla_ref, gr_ref = (next(it) for _ in range(5))
    s0_ref = next(it) if has_s0 else None
    gg_ref, seg_ref = next(it), next(it)
    if has_alias:
        next(it)
    ob_ref, sfin_ref, of_scr, or_scr = (next(it) for _ in range(4))
    expand_state, chunk, collapse_state = _gla_tools(gq_ref, gk_ref, gv_ref, gla_ref)
    n = seq // B_CHUNK

    def step(i, carry):
        new = []
        for bb in range(nbb):
            rows, o, Sf = chunk(0, bb * seq + i * B_CHUNK, carry[2 * bb])
            of_scr[rows, :] = o
            rows, o, Sb = chunk(1, bb * seq + (n - 1 - i) * B_CHUNK, carry[2 * bb + 1])
            or_scr[rows, :] = o
            new += [Sf, Sb]
        return tuple(new)

    if has_s0:
        carry = tuple(expand_state(s0_ref[bb, d]) for bb in range(nbb) for d in range(2))
    else:
        carry = tuple(jnp.zeros((B_VW, B_KW), F32) for _ in range(2 * nbb))
    if n <= 4:
        for i in range(n):
            carry = step(i, carry)
    else:
        carry = lax.fori_loop(0, n, step, carry, unroll=2)
    for bb in range(nbb):
        if zero_fill:
            sfin_ref[bb, 0, 0] = collapse_state(carry[2 * bb])
            sfin_ref[bb, 0, 1] = collapse_state(carry[2 * bb + 1])
            sfin_ref[bb, 1:] = jnp.zeros((DEPTH - 1, 2, B_DV, B_KW), F32)
        else:
            sfin_ref[bb, 0] = collapse_state(carry[2 * bb])
            sfin_ref[bb, 1] = collapse_state(carry[2 * bb + 1])

    RT = 256

    def fin(t, _):
        rows = pl.ds(pl.multiple_of(t * RT, RT), RT)
        ob_ref[rows, :] = _gla_gate(of_scr[rows, :] + or_scr[rows, :], seg_ref[...], gg_ref[...], gr_ref[rows, :])
        return 0

    lax.fori_loop(0, nbb * seq // RT, fin, 0)


def _gla(gq, gk, gv, gla, gr, s0, prm, seg, states, *, l, ctx):
    nb, seq, nbb = (NB_CTX, T_CTX, GLA_NBB_CTX) if ctx else (NB_LAT, T_LAT, GLA_NBB_LAT)
    tok = lambda w: pl.BlockSpec((nbb * seq, w), lambda b: (b, 0))
    in_specs = [tok(B_KW), tok(B_KW), tok(B_VW), tok(2 * B_KW), tok(B_VW)]
    args = [gq, gk, gv, gla, gr]
    if s0 is not None:
        in_specs.append(pl.BlockSpec((nbb, 2, B_DV, B_KW), lambda b: (b, 0, 0, 0)))
        args.append(s0)
    in_specs += [pl.BlockSpec((None, 1, B_VW), lambda b: (l, 0, 0)), pl.BlockSpec((B_VW, B_VW), lambda b: (0, 0))]
    args += [prm["ggl"], seg]
    aliases = {}
    if ctx:
        st_shape = jax.ShapeDtypeStruct((NB_CTX, DEPTH, 2, B_DV, B_KW), F32)
        st_spec = pl.BlockSpec((nbb, None, 2, B_DV, B_KW), lambda b: (b, l, 0, 0, 0))
        if states is None:
            assert l == 0
            st_spec = pl.BlockSpec((nbb, DEPTH, 2, B_DV, B_KW), lambda b: (b, 0, 0, 0, 0))
        if states is not None:
            aliases[len(args)] = 1
            in_specs.append(pl.BlockSpec(memory_space=pl.ANY))
            args.append(states)
    else:
        st_shape = jax.ShapeDtypeStruct((nb, 2, B_DV, B_KW), F32)
        st_spec = pl.BlockSpec((nbb, 2, B_DV, B_KW), lambda b: (b, 0, 0, 0))
    return pl.pallas_call(
        functools.partial(_gla_body, seq=seq, nbb=nbb, has_s0=s0 is not None, has_alias=bool(aliases),
                          zero_fill=ctx and states is None),
        out_shape=[jax.ShapeDtypeStruct((nb * seq, B_VW), BF16), st_shape],
        grid=(nb // nbb,),
        in_specs=in_specs,
        out_specs=[tok(B_VW), st_spec],
        scratch_shapes=[pltpu.VMEM((nbb * seq, B_VW), F32), pltpu.VMEM((nbb * seq, B_VW), F32)],
        input_output_aliases=aliases,
        compiler_params=_cparams(("parallel",)),
        name="gla_ctx" if ctx else "gla_lat",
    )(*args)


def _attn_body(*refs, lam_init, ctx, nbb):
    it = iter(refs)
    x_ref, qa_ref, qc_ref, ob_ref = (next(it) for _ in range(4))
    kv_new = [next(it) for _ in range(4)]
    kv_old = None if ctx else [next(it) for _ in range(4)]
    wout_ref, gt_ref, lam_ref, gco_ref, seg_ref, o_ref, mix_scr = (next(it) for _ in range(7))

    def keys_values(bb):
        if ctx:
            return [r[bb] for r in kv_new]
        return [jnp.concatenate([rn[bb], ro[...]], axis=0) for rn, ro in zip(kv_new, kv_old)]

    b = pl.program_id(0)
    r = 0 if ctx else 1 + b
    tq = x_ref.shape[0] // nbb
    a_stack, c_stack = (A_STACK_CTX, C_STACK_CTX) if ctx else (A_STACK_LAT, C_STACK_LAT)
    lane = lax.broadcasted_iota(jnp.int32, (1, 128), 1)
    lane2 = lax.broadcasted_iota(jnp.int32, (1, 256), 1)
    hmask = [(lane >= hh * A_D) & (lane < (hh + 1) * A_D) for hh in range(A_KV)]
    lm = lam_ref[...]
    lam = (jnp.exp(jnp.sum(lm[0:1] * lm[1:2], axis=-1, keepdims=True))
           - jnp.exp(jnp.sum(lm[2:3] * lm[3:4], axis=-1, keepdims=True)) + lam_init)

    def softmax_terms(s):
        e = jnp.exp2(s - jnp.max(s, axis=-1, keepdims=True))
        return e, jnp.sum(e, axis=-1, keepdims=True)

    for bb in range(nbb):
        _attn_mix_rows(bb, tq, qa_ref, qc_ref, ob_ref, keys_values(bb), gco_ref, seg_ref, mix_scr,
                       a_stack, c_stack, hmask, lane2, lam, lam_init, softmax_terms)
    mixed = _dot(mix_scr[...], wout_ref[...])
    o_ref[...] = x_ref[...] + gt_ref[pl.ds(r, 1), :] * mixed


def _attn_mix_rows(bb, tq, qa_ref, qc_ref, ob_ref, kv, gco_ref, seg_ref, mix_scr,
                   a_stack, c_stack, hmask, lane2, lam, lam_init, softmax_terms):
    rows = slice(bb * tq, (bb + 1) * tq)
    zero16 = jnp.zeros((), BF16)
    ka, va, kc_all, vc_all = kv
    va_ones = jnp.concatenate([va, jnp.ones_like(va)], axis=1)

    maps = [(g, hh) for g in range(A_G) for hh in range(A_KV)]
    acc = [jnp.zeros((tq, 128), F32) for _ in range(A_G)]
    for g0 in range(0, len(maps), a_stack):
        grp = maps[g0:g0 + a_stack]
        qs = jnp.concatenate(
            [jnp.where(hmask[hh], qa_ref[rows, g * 128:(g + 1) * 128], zero16) for g, hh in grp], axis=0)
        s = _dot_nt(qs, ka)
        e = jnp.exp2(s - jnp.max(s, axis=-1, keepdims=True)).astype(BF16)
        oe = _dot(e, va_ones)
        o = oe[:, :128] * (1.0 / oe[:, 128:])
        for k, (g, hh) in enumerate(grp):
            acc[g] = acc[g] + jnp.where(hmask[hh], o[k * tq:(k + 1) * tq], 0.0)
    for g in range(A_G):
        mix_scr[rows, g * 128:(g + 1) * 128] = acc[g].astype(BF16)

    mix_scr[rows, A_W:A_W + B_VW] = ob_ref[rows, :]

    outs = []
    for win in range(2):
        base = win * 128
        kc = kc_all[:, base:base + 256]
        vc = vc_all[:, base:base + 256]
        qc = qc_ref[rows, base:base + 256]
        ow = jnp.zeros((tq, 256), F32)
        cmaps = [(hh, mm) for hh in (2 * win, 2 * win + 1) for mm in range(2)]
        for g0 in range(0, len(cmaps), c_stack):
            grp = cmaps[g0:g0 + c_stack]
            qparts = []
            for hh, mm in grp:
                lo = (hh * 2 + mm) * C_DQ - base
                qparts.append(jnp.where((lane2 >= lo) & (lane2 < lo + C_DQ), qc, zero16))
            e, l = softmax_terms(_dot_nt(jnp.concatenate(qparts, axis=0), kc))
            ws, invs = [], []
            for k in range(0, len(grp), 2):
                l0 = l[k * tq:(k + 1) * tq]
                l1 = l[(k + 1) * tq:(k + 2) * tq]
                ws.append((e[k * tq:(k + 1) * tq] - (lam * l0 / l1) * e[(k + 1) * tq:(k + 2) * tq]).astype(BF16))
                invs.append(1.0 / l0)
            rr = _dot(jnp.concatenate(ws, axis=0) if len(ws) > 1 else ws[0], vc)
            for k in range(len(ws)):
                hh = grp[2 * k][0]
                vlo = hh * C_DV - base
                vm = (lane2 >= vlo) & (lane2 < vlo + C_DV)
                ow = ow + jnp.where(vm, rr[k * tq:(k + 1) * tq] * invs[k], 0.0)
        outs.append(ow)
    oc = jnp.concatenate([outs[0][:, :128], outs[0][:, 128:] + outs[1][:, :128], outs[1][:, 128:]], axis=1)
    ms = _dot((oc * oc).astype(BF16), seg_ref[...]) * (1.0 / C_DV)
    oc = oc * lax.rsqrt(ms + EPS) * gco_ref[...] * (1.0 - lam_init)
    mix_scr[rows, A_W + B_VW:] = oc.astype(BF16)


def _attn(x, qa, qc, ob, kv_new, kv_old, prm, seg96, *, l, lam_init, ctx):
    nb, seq, nbb, tq = (NB_CTX, T_CTX, ATT_NBB_CTX, T_CTX) if ctx else (NB_LAT, T_LAT, 1, TQ_ATT_LAT)
    nq = seq // tq
    rb = nbb * tq
    off = 0 if ctx else N_CTX // rb
    qspec = lambda w: pl.BlockSpec((rb, w), lambda b, q: (b * nq + q, 0))
    one_buf = None if ctx else pl.Buffered(1)
    kspec = lambda a: pl.BlockSpec((nbb,) + a.shape[1:], lambda b, q: (b, 0, 0), pipeline_mode=one_buf)
    ospec = lambda a: pl.BlockSpec((None, None) + a.shape[2:], lambda b, q: (b, l, 0, 0), pipeline_mode=one_buf)
    lay = lambda shape: pl.BlockSpec((None,) + shape, lambda b, q: (l,) + (0,) * len(shape),
                                     pipeline_mode=pl.Buffered(1))
    xspec = pl.BlockSpec((rb, D_MODEL), lambda b, q: (off + b * nq + q, 0))
    in_specs = [xspec, qspec(384), qspec(384), qspec(B_VW)] + [kspec(a) for a in kv_new]
    args = [x, qa, qc, ob, *kv_new]
    if kv_old is not None:
        in_specs += [ospec(a) for a in kv_old]
        args += list(kv_old)
    in_specs += [lay((D_MODEL, D_MODEL)),
                 pl.BlockSpec((None, 8, D_MODEL), lambda b, q: (l, 0, 5)),
                 lay((4, C_DQ)), lay((1, 384)),
                 pl.BlockSpec((384, 384), lambda b, q: (0, 0))]
    args += [prm["w_o"], prm["mod"], prm["lam_c"], prm["gco"], seg96]
    return pl.pallas_call(
        functools.partial(_attn_body, lam_init=lam_init, ctx=ctx, nbb=nbb),
        out_shape=jax.ShapeDtypeStruct((N_TOK, D_MODEL), F32),
        grid=(nb // nbb, nq),
        in_specs=in_specs,
        out_specs=xspec,
        scratch_shapes=[pltpu.VMEM((rb, D_MODEL), BF16)],
        input_output_aliases={0: 0},
        compiler_params=_cparams(("parallel", "arbitrary")),
        name="attn_ctx" if ctx else "attn_lat",
    )(*args)


def _block_ones(width, seg):
    idx = np.arange(width) // seg
    return jnp.asarray((idx[:, None] == idx[None, :]).astype(np.float32), dtype=BF16)


def _rot_matrix(width, half):
    m = np.zeros((width, width), np.float32)
    for j in range(width):
        if (j % (2 * half)) < half:
            m[j + half, j] = -1.0
        else:
            m[j - half, j] = 1.0
    return jnp.asarray(m, dtype=BF16)


def _rope_tables():
    t = np.arange(T_LAT)
    row = (t // GRID_W).astype(np.float32)
    col = (t % GRID_W).astype(np.float32)

    def tab(head_dim, n_heads):
        m = head_dim // 4
        freqs = ROPE_BASE ** (-jnp.arange(m, dtype=F32) / m)
        ang_r = jnp.asarray(row)[:, None] * freqs[None, :]
        ang_c = jnp.asarray(col)[:, None] * freqs[None, :]
        cs = jnp.concatenate([jnp.cos(ang_r)] * 2 + [jnp.cos(ang_c)] * 2, axis=1)
        sn = jnp.concatenate([jnp.sin(ang_r)] * 2 + [jnp.sin(ang_c)] * 2, axis=1)
        return jnp.tile(cs, (1, n_heads)), jnp.tile(sn, (1, n_heads))

    cosa, sina = tab(A_D, 2)
    cosc, sinc = tab(C_DQ, 2 * C_H)
    return cosa, sina, cosc, sinc


def kernel(x_prompt, x_sample, c, cache_a_k, cache_a_v, cache_c_k, cache_c_v, state_gla, c_ctx, w_ada, b_ada,
           g_norm, w_ffn_gate, w_ffn_up, w_ffn_down, w_in, g_a_q, g_a_k, w_gla_up, b_gla, g_gla, g_c_q, g_c_k,
           lam_c, g_c_out, w_out):
    lam_inits = [0.8 - 0.6 * math.exp(-0.3 * l) for l in range(DEPTH)]

    consts = {
        "seg64_384": _block_ones(384, 64), "seg64_128": _block_ones(128, 64),
        "seg48_384": _block_ones(384, 48), "seg96_384": _block_ones(384, 96),
        "seg64_256": _block_ones(256, 64),
        "rot_a384": _rot_matrix(384, 16), "rot_a128": _rot_matrix(128, 16), "rot_c384": _rot_matrix(384, 12),
    }
    rope_tabs = _rope_tables()

    wq = w_in[:, :, 0:384].reshape(DEPTH, D_MODEL, A_KV, A_G, A_D).transpose(0, 1, 3, 2, 4).reshape(DEPTH, D_MODEL, 384)
    seg = lambda a, b: w_in[:, :, a:b]
    w_p = jnp.concatenate([
        wq, seg(384, 512), seg(512, 640),
        seg(640, 768), seg(768, 896), seg(896, 1152),
        seg(1184, 1440),
        seg(1440, 1824), seg(1824, 2208), seg(2208, 2592),
        seg(1152, 1184), jnp.zeros((DEPTH, D_MODEL, 96), F32),
    ], axis=2).astype(BF16)
    wo_a = w_out[:, 0:384].reshape(DEPTH, A_KV, A_G, A_D, D_MODEL).transpose(0, 2, 1, 3, 4).reshape(DEPTH, 384, D_MODEL)
    wup = jnp.zeros((DEPTH, 128, 2 * B_KW), F32)
    wup = wup.at[:, 0:B_RANK, 0:B_KW].set(w_gla_up[:, 0]).at[:, B_RANK:2 * B_RANK, B_KW:].set(w_gla_up[:, 1])
    cond8 = jnp.zeros((8, D_MODEL), F32).at[0].set(c_ctx).at[1:3].set(c)
    prm = {
        "mod": _adaln(cond8, w_ada, b_ada),
        "gn4": g_norm.reshape(DEPTH, 3, 1, D_MODEL),
        "w_p": w_p,
        "w_o": jnp.concatenate([wo_a, w_out[:, 384:]], axis=1).astype(BF16),
        "wup": wup.astype(BF16),
        "bgla": b_gla.reshape(DEPTH, 1, 2 * B_KW),
        "gaq": jnp.tile(g_a_q, (1, 6)).reshape(DEPTH, 1, 384),
        "gak": jnp.tile(g_a_k, (1, 2)).reshape(DEPTH, 1, 128),
        "gcq": jnp.tile(g_c_q.reshape(DEPTH, 96), (1, 4)).reshape(DEPTH, 1, 384),
        "gck": jnp.tile(g_c_k.reshape(DEPTH, 96), (1, 4)).reshape(DEPTH, 1, 384),
        "gco": jnp.tile(g_c_out, (1, 4)).reshape(DEPTH, 1, 384),
        "ggl": jnp.tile(g_gla, (1, 4)).reshape(DEPTH, 1, 256),
        "lam_c": lam_c,
    }

    caches, states = None, None
    s0_lat = jnp.swapaxes(state_gla.reshape(NB_LAT, DEPTH, 2, B_KW, B_DV), 3, 4)
    kv_old = [a.reshape(NB_LAT, DEPTH, PAST_LEN, a.shape[-2] * a.shape[-1]).astype(BF16)
              for a in (cache_a_k, cache_a_v, cache_c_k, cache_c_v)]
    ffn = functools.partial(_ffn, mod=prm["mod"], gn4=prm["gn4"], wg=w_ffn_gate, wu=w_ffn_up, wd=w_ffn_down)
    x = None
    for l in range(DEPTH):
        if l == 0:
            x = ffn((x_prompt.reshape(N_CTX, D_MODEL), x_sample.reshape(N_LAT, D_MODEL)), l=0, s=0, first=True)
        else:
            x = ffn((x,), l=l, s=0)

        outs = _proj(x, prm, consts, None, caches, l=l, ctx=True)
        qa, ka, va, qc, kc, vc, gq, gk, gv, gla, gr = outs[:11]
        caches = list(outs[11:])
        ob, states = _gla(gq, gk, gv, gla, gr, None, prm, consts["seg64_256"], states, l=l, ctx=True)
        r3 = lambda a, n, t: a.reshape(n, t, a.shape[-1])
        x = _attn(x, qa, qc, ob, [r3(a, NB_CTX, T_CTX) for a in (ka, va, kc, vc)], None, prm, consts["seg96_384"],
                  l=l, lam_init=lam_inits[l], ctx=True)

        qa, ka, va, qc, kc, vc, gq, gk, gv, gla, gr = _proj(x, prm, consts, rope_tabs, None, l=l, ctx=False)
        ob, _ = _gla(gq, gk, gv, gla, gr, s0_lat[:, l], prm, consts["seg64_256"], None, l=l, ctx=False)
        x = _attn(x, qa, qc, ob, [r3(a, NB_LAT, T_LAT) for a in (ka, va, kc, vc)], kv_old, prm, consts["seg96_384"],
                  l=l, lam_init=lam_inits[l], ctx=False)

        if l == DEPTH - 1:
            y_prompt, y_sample = ffn((x,), l=l, s=1, last=True)
        else:
            x = ffn((x,), l=l, s=1)

    return (y_prompt.reshape(NB_CTX, T_CTX, D_MODEL), y_sample.reshape(NB_LAT, T_LAT, D_MODEL),
            caches[0].reshape(NB_CTX, DEPTH, T_CTX, A_KV, A_D), caches[1].reshape(NB_CTX, DEPTH, T_CTX, A_KV, A_D),
            caches[2].reshape(NB_CTX, DEPTH, T_CTX, C_H, 2 * C_DQ), caches[3].reshape(NB_CTX, DEPTH, T_CTX, C_H, C_DV),
            jnp.swapaxes(states, 3, 4).reshape(NB_CTX, DEPTH, 2, B_H, B_DK, B_DV))
```

```python
import functools
import math

import numpy as np
import jax
import jax.numpy as jnp
from jax import lax
from jax.experimental import pallas as pl
from jax.experimental.pallas import tpu as pltpu

F32 = jnp.float32
BF16 = jnp.bfloat16

D_MODEL = 1024
D_FF = 2816
DEPTH = 4
N_MOD = 9
EPS = 1e-6
ROPE_BASE = 10000.0
GRID_W = 64
LOG2E = math.log2(math.e)

NB_CTX, T_CTX = 32, 256
NB_LAT, T_LAT = 2, 2048
PAST_LEN = 512
N_CTX = NB_CTX * T_CTX
N_LAT = NB_LAT * T_LAT
N_TOK = N_CTX + N_LAT

A_KV, A_G, A_D = 2, 3, 64
A_W = A_KV * A_G * A_D
B_H, B_DK, B_DV = 4, 32, 64
B_KW = B_H * B_DK
B_VW = B_H * B_DV
B_RANK = 16
B_TAU = 16.0
B_CHUNK = 64
C_H, C_DQ, C_DV = 4, 48, 96
C_W = C_H * C_DV

P_AQ, P_AK, P_AV = 0, 384, 512
P_BQ, P_BK, P_BV, P_BR = 640, 768, 896, 1152
P_CQ, P_CK, P_CV = 1408, 1792, 2176
P_BG = 2560
P_W = 2688

TM_FFN = 1024
TF_FFN = 512
FFN_MAIN_STEPS = D_FF // TF_FFN
FFN_TAIL = D_FF - FFN_MAIN_STEPS * TF_FFN
assert FFN_TAIL > 0 and (FFN_MAIN_STEPS * TF_FFN) % FFN_TAIL == 0 and FFN_TAIL % 128 == 0
TM_PROJ_CTX = 256
TM_PROJ_LAT = 512
TQ_ATT_LAT = 512
ATT_NBB_CTX = 4
TN_ADA = 1152
GLA_NBB_CTX = 8
GLA_NBB_LAT = 2
A_STACK_CTX, A_STACK_LAT = 6, 2
C_STACK_CTX, C_STACK_LAT = 4, 2
VMEM_LIMIT = 56 * 1024 * 1024


def _cparams(sem):
    return pltpu.CompilerParams(dimension_semantics=sem, vmem_limit_bytes=VMEM_LIMIT)


def _dot(a, b):
    return jnp.dot(a, b, preferred_element_type=F32)


def _dot_nt(a, b):
    return lax.dot_general(a, b, (((1,), (1,)), ((), ())), preferred_element_type=F32)


def _dot_tn(a, b):
    return lax.dot_general(a, b, (((0,), (0,)), ((), ())), preferred_element_type=F32)


def _split_dot(x, m):
    hi = x.astype(BF16)
    lo = (x - hi.astype(F32)).astype(BF16)
    return _dot(hi, m) + _dot(lo, m)


def _split_dot_l(m, x):
    hi = x.astype(BF16)
    lo = (x - hi.astype(F32)).astype(BF16)
    return _dot(m, hi) + _dot(m, lo)


def _seg_rms(x, seg, n):
    ms = _dot((x * x).astype(BF16), seg) * (1.0 / n)
    return x * lax.rsqrt(ms + EPS)


def _silu(x):
    return x * jax.nn.sigmoid(x)


def _norm_mod(x, gn, sc, sh):
    ms = jnp.mean(x * x, axis=-1, keepdims=True)
    return ((x * lax.rsqrt(ms + EPS)) * (gn * (1.0 + sc)) + sh).astype(BF16)


def _adaln_body(cond_ref, w_ref, b_ref, o_ref):
    c = cond_ref[...]
    sc = _silu(c).astype(BF16)
    o_ref[0] = _dot(sc, w_ref[0].astype(BF16)) + b_ref[0]


def _adaln(cond8, w_ada, b_ada):
    nj = (N_MOD * D_MODEL) // TN_ADA
    return pl.pallas_call(
        _adaln_body,
        out_shape=jax.ShapeDtypeStruct((DEPTH, 8, N_MOD * D_MODEL), F32),
        grid=(DEPTH, nj),
        in_specs=[
            pl.BlockSpec((8, D_MODEL), lambda l, j: (0, 0)),
            pl.BlockSpec((1, D_MODEL, TN_ADA), lambda l, j: (l, 0, j)),
            pl.BlockSpec((1, 1, TN_ADA), lambda l, j: (l, 0, j)),
        ],
        out_specs=pl.BlockSpec((1, 8, TN_ADA), lambda l, j: (l, 0, j)),
        compiler_params=_cparams(("parallel", "parallel")),
        name="adaln",
    )(cond8, w_ada, b_ada.reshape(DEPTH, 1, N_MOD * D_MODEL))


FFN_TILES_CTX = N_CTX // TM_FFN
FFN_TILES_PER_LAT = T_LAT // TM_FFN


def _ffn_body(*refs, first, last):
    it = iter(refs)
    x_refs = (next(it), next(it)) if first else (next(it),)
    sh_ref, sc_ref, gt_ref, gn_ref = (next(it) for _ in range(4))
    w_main = [next(it) for _ in range(3)]
    w_tail = [next(it) for _ in range(3)]
    o_refs = (next(it), next(it)) if last else (next(it),)
    h_scr, acc_scr = next(it), next(it)

    i = pl.program_id(0)
    j = pl.program_id(1)
    is_ctx = i < FFN_TILES_CTX
    r = jnp.where(is_ctx, 0, 1 + jnp.maximum(i - FFN_TILES_CTX, 0) // FFN_TILES_PER_LAT)

    def on_tile(pred, n_variants, fn):
        if n_variants == 1:
            pl.when(pred)(lambda: fn(0))
        else:
            pl.when(pred & is_ctx)(lambda: fn(0))
            pl.when(pred & jnp.logical_not(is_ctx))(lambda: fn(1))

    def prologue(k):
        h_scr[...] = _norm_mod(x_refs[k][...], gn_ref[...], sc_ref[pl.ds(r, 1), :], sh_ref[pl.ds(r, 1), :])
        acc_scr[...] = jnp.zeros_like(acc_scr)

    on_tile(j == 0, len(x_refs), prologue)

    def hidden_tile(wg_ref, wu_ref, wd_ref):
        h = h_scr[...]
        g = _dot(h, wg_ref[...].astype(BF16))
        u = _dot(h, wu_ref[...].astype(BF16))
        a = (_silu(g) * u).astype(BF16)
        acc_scr[...] += _dot(a, wd_ref[...].astype(BF16))

    pl.when(j < FFN_MAIN_STEPS)(lambda: hidden_tile(*w_main))
    pl.when(j == FFN_MAIN_STEPS)(lambda: hidden_tile(*w_tail))

    def epilogue(k):
        x_ref = x_refs[k if first else 0]
        o_ref = o_refs[k if last else 0]
        o_ref[...] = x_ref[...] + 0.5 * gt_ref[pl.ds(r, 1), :] * acc_scr[...]

    on_tile(j == pl.num_programs(1) - 1, max(len(x_refs), len(o_refs)), epilogue)


def _ffn(xs, mod, gn4, wg, wu, wd, *, l, s, first=False, last=False):
    ni = N_TOK // TM_FFN
    nj = FFN_MAIN_STEPS + 1
    nm = FFN_MAIN_STEPS
    tail_blk = (FFN_MAIN_STEPS * TF_FFN) // FFN_TAIL
    k0 = 6 * s
    gi = 2 * s
    tc = FFN_TILES_CTX
    split_specs = [pl.BlockSpec((TM_FFN, D_MODEL), lambda i, j: (jnp.minimum(i, tc - 1), 0)),
                   pl.BlockSpec((TM_FFN, D_MODEL), lambda i, j: (jnp.maximum(i - tc, 0), 0))]
    one_spec = [pl.BlockSpec((TM_FFN, D_MODEL), lambda i, j: (i, 0))]
    mspec = lambda k: pl.BlockSpec((None, 8, D_MODEL), lambda i, j: (l, 0, k))
    in_specs = (split_specs if first else one_spec) + [
        mspec(k0), mspec(k0 + 1), mspec(k0 + 2),
        pl.BlockSpec((None, None, 1, D_MODEL), lambda i, j: (l, gi, 0, 0)),
        pl.BlockSpec((None, None, D_MODEL, TF_FFN), lambda i, j: (l, s, 0, jnp.minimum(j, nm - 1))),
        pl.BlockSpec((None, None, D_MODEL, TF_FFN), lambda i, j: (l, s, 0, jnp.minimum(j, nm - 1))),
        pl.BlockSpec((None, None, TF_FFN, D_MODEL), lambda i, j: (l, s, jnp.minimum(j, nm - 1), 0)),
        pl.BlockSpec((None, None, D_MODEL, FFN_TAIL), lambda i, j: (l, s, 0, tail_blk), pipeline_mode=pl.Buffered(1)),
        pl.BlockSpec((None, None, D_MODEL, FFN_TAIL), lambda i, j: (l, s, 0, tail_blk), pipeline_mode=pl.Buffered(1)),
        pl.BlockSpec((None, None, FFN_TAIL, D_MODEL), lambda i, j: (l, s, tail_blk, 0), pipeline_mode=pl.Buffered(1)),
    ]
    if last:
        out_shape = [jax.ShapeDtypeStruct((N_CTX, D_MODEL), F32), jax.ShapeDtypeStruct((N_LAT, D_MODEL), F32)]
        out_specs = split_specs
    else:
        out_shape = jax.ShapeDtypeStruct((N_TOK, D_MODEL), F32)
        out_specs = one_spec[0]
    return pl.pallas_call(
        functools.partial(_ffn_body, first=first, last=last),
        out_shape=out_shape,
        grid=(ni, nj),
        in_specs=in_specs,
        out_specs=out_specs,
        scratch_shapes=[pltpu.VMEM((TM_FFN, D_MODEL), BF16), pltpu.VMEM((TM_FFN, D_MODEL), F32)],
        compiler_params=_cparams(("arbitrary", "arbitrary")),
        name="ffn",
    )(*xs, mod, mod, mod, gn4, wg, wu, wd, wg, wu, wd)


def _proj_body(*refs, rope, ctx, tiles_per_batch, n_alias):
    it = iter(refs)
    x_ref, sh_ref, sc_ref, gn_ref, w_ref, wup_ref, bgla_ref = (next(it) for _ in range(7))
    gaq_ref, gak_ref, gcq_ref, gck_ref = (next(it) for _ in range(4))
    s64a_ref, s64k_ref, s48_ref = (next(it) for _ in range(3))
    if rope:
        cosa_ref, sina_ref, cosc_ref, sinc_ref, ra_ref, rak_ref, rc_ref = (next(it) for _ in range(7))
    for _ in range(n_alias):
        next(it)
    qa_o, ka_o, va_o, qc_o, kc_o, vc_o, gq_o, gk_o, gv_o, gla_o, gr_o = (next(it) for _ in range(11))
    if ctx:
        ka32_o, va32_o, kc32_o, vc32_o = (next(it) for _ in range(4))

    i = pl.program_id(0)
    r = 0 if ctx else 1 + i // tiles_per_batch

    h = _norm_mod(x_ref[...], gn_ref[...], sc_ref[pl.ds(r, 1), :], sh_ref[pl.ds(r, 1), :])
    p = _dot(h, w_ref[...])

    aq = _seg_rms(p[:, P_AQ:P_AQ + A_W], s64a_ref[...], A_D) * gaq_ref[...]
    ak = _seg_rms(p[:, P_AK:P_AK + 128], s64k_ref[...], A_D) * gak_ref[...]
    av = p[:, P_AV:P_AV + 128]
    cq = _seg_rms(p[:, P_CQ:P_CQ + 384], s48_ref[...], C_DQ) * gcq_ref[...]
    ck = _seg_rms(p[:, P_CK:P_CK + 384], s48_ref[...], C_DQ) * gck_ref[...]
    cv = p[:, P_CV:P_CV + 384]
    if ctx:
        for o, val in ((ka32_o, ak), (va32_o, av), (kc32_o, ck), (vc32_o, cv)):
            if n_alias:
                o[...] = val
            else:
                o[0] = val
                o[1:] = jnp.zeros((DEPTH - 1,) + val.shape, F32)
    if rope:
        cosa = cosa_ref[...]
        sina = sina_ref[...]
        cosa3 = jnp.concatenate([cosa, cosa, cosa], axis=1)
        sina3 = jnp.concatenate([sina, sina, sina], axis=1)
        aq = aq * cosa3 + _dot(aq.astype(BF16), ra_ref[...]) * sina3
        ak = ak * cosa + _dot(ak.astype(BF16), rak_ref[...]) * sina
        cosc = cosc_ref[...]
        sinc = sinc_ref[...]
        cq = cq * cosc + _dot(cq.astype(BF16), rc_ref[...]) * sinc
        ck = ck * cosc + _dot(ck.astype(BF16), rc_ref[...]) * sinc
    qa_o[...] = (aq * (A_D ** -0.5 * LOG2E)).astype(BF16)
    ka_o[...] = ak.astype(BF16)
    va_o[...] = av.astype(BF16)
    qc_o[...] = (cq * (C_DQ ** -0.5 * LOG2E)).astype(BF16)
    kc_o[...] = ck.astype(BF16)
    vc_o[...] = cv.astype(BF16)

    gq_o[...] = p[:, P_BQ:P_BQ + B_KW] * (B_DK ** -0.5)
    gk_o[...] = p[:, P_BK:P_BK + B_KW]
    gv_o[...] = p[:, P_BV:P_BV + B_VW]
    gr_o[...] = _silu(p[:, P_BR:P_BR + B_VW])
    z = _dot(p[:, P_BG:P_BG + 128].astype(BF16), wup_ref[...]) + bgla_ref[...]
    log_sig = jnp.minimum(z, 0.0) - jnp.log1p(jnp.exp(-jnp.abs(z)))
    gla_o[...] = log_sig * (1.0 / B_TAU)


def _proj(x, prm, consts, rope_tabs, caches, *, l, ctx):
    n = N_CTX if ctx else N_LAT
    tm = TM_PROJ_CTX if ctx else TM_PROJ_LAT
    off = 0 if ctx else N_CTX // tm
    tiles_per_batch = (T_CTX if ctx else T_LAT) // tm
    rope = not ctx
    full = lambda shape: pl.BlockSpec(shape, lambda i: (0,) * len(shape))
    lay = lambda shape: pl.BlockSpec((None,) + shape, lambda i: (l,) + (0,) * len(shape))
    in_specs = [
        pl.BlockSpec((tm, D_MODEL), lambda i: (i + off, 0)),
        pl.BlockSpec((None, 8, D_MODEL), lambda i: (l, 0, 3)),
        pl.BlockSpec((None, 8, D_MODEL), lambda i: (l, 0, 4)),
        pl.BlockSpec((None, None, 1, D_MODEL), lambda i: (l, 1, 0, 0)),
        lay((D_MODEL, P_W)),
        lay((128, 2 * B_KW)),
        lay((1, 2 * B_KW)),
        lay((1, 384)), lay((1, 128)), lay((1, 384)), lay((1, 384)),
        full((384, 384)), full((128, 128)), full((384, 384)),
    ]
    args = [x, prm["mod"], prm["mod"], prm["gn4"], prm["w_p"], prm["wup"], prm["bgla"],
            prm["gaq"], prm["gak"], prm["gcq"], prm["gck"],
            consts["seg64_384"], consts["seg64_128"], consts["seg48_384"]]
    if rope:
        tpb = tiles_per_batch
        in_specs += [
            pl.BlockSpec((tm, 128), lambda i: (i % tpb, 0)),
            pl.BlockSpec((tm, 128), lambda i: (i % tpb, 0)),
            pl.BlockSpec((tm, 384), lambda i: (i % tpb, 0)),
            pl.BlockSpec((tm, 384), lambda i: (i % tpb, 0)),
            full((384, 384)), full((128, 128)), full((384, 384)),
        ]
        args += [*rope_tabs, consts["rot_a384"], consts["rot_a128"], consts["rot_c384"]]
    widths = [(384, BF16), (128, BF16), (128, BF16), (384, BF16), (384, BF16), (384, BF16),
              (B_KW, F32), (B_KW, F32), (B_VW, F32), (2 * B_KW, F32), (B_VW, F32)]
    out_shape = [jax.ShapeDtypeStruct((n, w), dt) for w, dt in widths]
    out_specs = [pl.BlockSpec((tm, w), lambda i: (i, 0)) for w, _ in widths]
    aliases = {}
    if ctx:
        assert tm == T_CTX
        for k, w in enumerate((128, 128, 384, 384)):
            if caches is not None:
                aliases[len(args)] = len(out_shape)
                in_specs.append(pl.BlockSpec(memory_space=pl.ANY))
                args.append(caches[k])
            out_shape.append(jax.ShapeDtypeStruct((NB_CTX, DEPTH, T_CTX, w), F32))
            if caches is not None:
                out_specs.append(pl.BlockSpec((None, None, T_CTX, w), lambda i: (i, l, 0, 0)))
            else:
                assert l == 0
                out_specs.append(pl.BlockSpec((None, DEPTH, T_CTX, w), lambda i: (i, 0, 0, 0)))
    return pl.pallas_call(
        functools.partial(_proj_body, rope=rope, ctx=ctx, tiles_per_batch=tiles_per_batch, n_alias=len(aliases)),
        out_shape=out_shape,
        grid=(n // tm,),
        in_specs=in_specs,
        out_specs=out_specs,
        input_output_aliases=aliases,
        compiler_params=_cparams(("parallel",)),
        name="proj_ctx" if ctx else "proj_lat",
    )(*args)


def _gla_tools(gq_ref, gk_ref, gv_ref, gla_ref):
    C = B_CHUNK
    row_i = lax.broadcasted_iota(jnp.int32, (C, B_KW), 0)
    rk = lax.broadcasted_iota(jnp.int32, (B_H * C, B_KW), 0) >> 6
    ck = lax.broadcasted_iota(jnp.int32, (B_H * C, B_KW), 1) >> 5
    hm_k = rk == ck
    rv = lax.broadcasted_iota(jnp.int32, (B_H * C, B_VW), 0) >> 6
    cv = lax.broadcasted_iota(jnp.int32, (B_H * C, B_VW), 1) >> 6
    hm_v = rv == cv
    ra = lax.broadcasted_iota(jnp.int32, (C, B_H * C), 0)
    ca = lax.broadcasted_iota(jnp.int32, (C, B_H * C), 1) & (C - 1)
    tri = (ca <= ra, ca >= ra)
    zero16 = jnp.zeros((), BF16)

    def expand_state(st):
        return jnp.where(hm_k, jnp.concatenate([st] * B_H, axis=0), 0.0)

    def chunk(d, start, S):
        rows = pl.ds(start if isinstance(start, int) else pl.multiple_of(start, C), C)
        q = gq_ref[rows, :]
        k = gk_ref[rows, :]
        v = gv_ref[rows, :].astype(BF16)
        la = gla_ref[rows, d * B_KW:(d + 1) * B_KW]
        p = la
        for s in (1, 2, 4, 8, 16, 32):
            p = p + jnp.where(row_i >= s, pltpu.roll(p, s, axis=0), 0.0)
        tot = p[C - 1:C, :]
        b = p if d == 0 else tot - p + la
        e = b - b[C // 2:C // 2 + 1, :]
        qt = (q * jnp.exp(e)).astype(BF16)
        kt = (k * jnp.exp(-e)).astype(BF16)
        qd = (q * jnp.exp(b)).astype(BF16)
        kd = (k * jnp.exp(tot - b)).astype(BF16)
        kbd = jnp.where(hm_k, jnp.concatenate([kt] * B_H, axis=0), zero16)
        a = _dot_nt(qt, kbd)
        a = jnp.where(tri[d], a, 0.0).astype(BF16)
        vbd = jnp.where(hm_v, jnp.concatenate([v] * B_H, axis=0), zero16)
        o = _dot(a, vbd) + _dot_nt(qd, S.astype(BF16))
        kv = _dot_tn(v, kd)
        S_new = S * jnp.exp(tot) + jnp.where(hm_k, kv, 0.0)
        return rows, o, S_new

    def collapse_state(S):
        Sm = jnp.where(hm_k, S, 0.0)
        acc = Sm[0:C, :]
        for hh in range(1, B_H):
            acc = acc + Sm[hh * C:(hh + 1) * C, :]
        return acc

    return expand_state, chunk, collapse_state


def _gla_gate(o, seg, gg, gr):
    ms = _split_dot(o * o, seg) * (1.0 / B_DV)
    return (o * lax.rsqrt(ms + EPS) * gg * gr).astype(BF16)


def _gla_body(*refs, seq, nbb, has_s0, has_alias, zero_fill):
    it = iter(refs)
    gq_ref, gk_ref, gv_ref, gla_ref, gr_ref = (next(it) for _ in range(5))
    s0_ref = next(it) if has_s0 else None
    gg_ref, seg_ref = next(it), next(it)
    if has_alias:
        next(it)
    ob_ref, sfin_ref, of_scr, or_scr = (next(it) for _ in range(4))
    expand_state, chunk, collapse_state = _gla_tools(gq_ref, gk_ref, gv_ref, gla_ref)
    n = seq // B_CHUNK

    def step(i, carry):
        new = []
        for bb in range(nbb):
            rows, o, Sf = chunk(0, bb * seq + i * B_CHUNK, carry[2 * bb])
            of_scr[rows, :] = o
            rows, o, Sb = chunk(1, bb * seq + (n - 1 - i) * B_CHUNK, carry[2 * bb + 1])
            or_scr[rows, :] = o
            new += [Sf, Sb]
        return tuple(new)

    if has_s0:
        carry = tuple(expand_state(s0_ref[bb, d]) for bb in range(nbb) for d in range(2))
    else:
        carry = tuple(jnp.zeros((B_VW, B_KW), F32) for _ in range(2 * nbb))
    if n <= 4:
        for i in range(n):
            carry = step(i, carry)
    else:
        carry = lax.fori_loop(0, n, step, carry, unroll=2)
    for bb in range(nbb):
        if zero_fill:
            sfin_ref[bb, 0, 0] = collapse_state(carry[2 * bb])
            sfin_ref[bb, 0, 1] = collapse_state(carry[2 * bb + 1])
            sfin_ref[bb, 1:] = jnp.zeros((DEPTH - 1, 2, B_DV, B_KW), F32)
        else:
            sfin_ref[bb, 0] = collapse_state(carry[2 * bb])
            sfin_ref[bb, 1] = collapse_state(carry[2 * bb + 1])

    RT = 256

    def fin(t, _):
        rows = pl.ds(pl.multiple_of(t * RT, RT), RT)
        ob_ref[rows, :] = _gla_gate(of_scr[rows, :] + or_scr[rows, :], seg_ref[...], gg_ref[...], gr_ref[rows, :])
        return 0

    lax.fori_loop(0, nbb * seq // RT, fin, 0)


def _gla(gq, gk, gv, gla, gr, s0, prm, seg, states, *, l, ctx):
    nb, seq, nbb = (NB_CTX, T_CTX, GLA_NBB_CTX) if ctx else (NB_LAT, T_LAT, GLA_NBB_LAT)
    tok = lambda w: pl.BlockSpec((nbb * seq, w), lambda b: (b, 0))
    in_specs = [tok(B_KW), tok(B_KW), tok(B_VW), tok(2 * B_KW), tok(B_VW)]
    args = [gq, gk, gv, gla, gr]
    if s0 is not None:
        in_specs.append(pl.BlockSpec((nbb, 2, B_DV, B_KW), lambda b: (b, 0, 0, 0)))
        args.append(s0)
    in_specs += [pl.BlockSpec((None, 1, B_VW), lambda b: (l, 0, 0)), pl.BlockSpec((B_VW, B_VW), lambda b: (0, 0))]
    args += [prm["ggl"], seg]
    aliases = {}
    if ctx:
        st_shape = jax.ShapeDtypeStruct((NB_CTX, DEPTH, 2, B_DV, B_KW), F32)
        st_spec = pl.BlockSpec((nbb, None, 2, B_DV, B_KW), lambda b: (b, l, 0, 0, 0))
        if states is None:
            assert l == 0
            st_spec = pl.BlockSpec((nbb, DEPTH, 2, B_DV, B_KW), lambda b: (b, 0, 0, 0, 0))
        if states is not None:
            aliases[len(args)] = 1
            in_specs.append(pl.BlockSpec(memory_space=pl.ANY))
            args.append(states)
    else:
        st_shape = jax.ShapeDtypeStruct((nb, 2, B_DV, B_KW), F32)
        st_spec = pl.BlockSpec((nbb, 2, B_DV, B_KW), lambda b: (b, 0, 0, 0))
    return pl.pallas_call(
        functools.partial(_gla_body, seq=seq, nbb=nbb, has_s0=s0 is not None, has_alias=bool(aliases),
                          zero_fill=ctx and states is None),
        out_shape=[jax.ShapeDtypeStruct((nb * seq, B_VW), BF16), st_shape],
        grid=(nb // nbb,),
        in_specs=in_specs,
        out_specs=[tok(B_VW), st_spec],
        scratch_shapes=[pltpu.VMEM((nbb * seq, B_VW), F32), pltpu.VMEM((nbb * seq, B_VW), F32)],
        input_output_aliases=aliases,
        compiler_params=_cparams(("parallel",)),
        name="gla_ctx" if ctx else "gla_lat",
    )(*args)


def _attn_body(*refs, lam_init, ctx, nbb):
    it = iter(refs)
    x_ref, qa_ref, qc_ref, ob_ref = (next(it) for _ in range(4))
    kv_new = [next(it) for _ in range(4)]
    kv_old = None if ctx else [next(it) for _ in range(4)]
    wout_ref, gt_ref, lam_ref, gco_ref, seg_ref, o_ref, mix_scr = (next(it) for _ in range(7))

    def keys_values(bb):
        if ctx:
            return [r[bb] for r in kv_new]
        return [jnp.concatenate([rn[bb], ro[...]], axis=0) for rn, ro in zip(kv_new, kv_old)]

    b = pl.program_id(0)
    r = 0 if ctx else 1 + b
    tq = x_ref.shape[0] // nbb
    a_stack, c_stack = (A_STACK_CTX, C_STACK_CTX) if ctx else (A_STACK_LAT, C_STACK_LAT)
    lane = lax.broadcasted_iota(jnp.int32, (1, 128), 1)
    lane2 = lax.broadcasted_iota(jnp.int32, (1, 256), 1)
    hmask = [(lane >= hh * A_D) & (lane < (hh + 1) * A_D) for hh in range(A_KV)]
    lm = lam_ref[...]
    lam = (jnp.exp(jnp.sum(lm[0:1] * lm[1:2], axis=-1, keepdims=True))
           - jnp.exp(jnp.sum(lm[2:3] * lm[3:4], axis=-1, keepdims=True)) + lam_init)

    def softmax_terms(s):
        e = jnp.exp2(s - jnp.max(s, axis=-1, keepdims=True))
        return e, jnp.sum(e, axis=-1, keepdims=True)

    for bb in range(nbb):
        _attn_mix_rows(bb, tq, qa_ref, qc_ref, ob_ref, keys_values(bb), gco_ref, seg_ref, mix_scr,
                       a_stack, c_stack, hmask, lane2, lam, lam_init, softmax_terms)
    mixed = _dot(mix_scr[...], wout_ref[...])
    o_ref[...] = x_ref[...] + gt_ref[pl.ds(r, 1), :] * mixed


def _attn_mix_rows(bb, tq, qa_ref, qc_ref, ob_ref, kv, gco_ref, seg_ref, mix_scr,
                   a_stack, c_stack, hmask, lane2, lam, lam_init, softmax_terms):
    rows = slice(bb * tq, (bb + 1) * tq)
    zero16 = jnp.zeros((), BF16)
    ka, va, kc_all, vc_all = kv
    va_ones = jnp.concatenate([va, jnp.ones_like(va)], axis=1)

    maps = [(g, hh) for g in range(A_G) for hh in range(A_KV)]
    acc = [jnp.zeros((tq, 128), F32) for _ in range(A_G)]
    for g0 in range(0, len(maps), a_stack):
        grp = maps[g0:g0 + a_stack]
        qs = jnp.concatenate(
            [jnp.where(hmask[hh], qa_ref[rows, g * 128:(g + 1) * 128], zero16) for g, hh in grp], axis=0)
        s = _dot_nt(qs, ka)
        e = jnp.exp2(s - jnp.max(s, axis=-1, keepdims=True)).astype(BF16)
        oe = _dot(e, va_ones)
        o = oe[:, :128] * (1.0 / oe[:, 128:])
        for k, (g, hh) in enumerate(grp):
            acc[g] = acc[g] + jnp.where(hmask[hh], o[k * tq:(k + 1) * tq], 0.0)
    for g in range(A_G):
        mix_scr[rows, g * 128:(g + 1) * 128] = acc[g].astype(BF16)

    mix_scr[rows, A_W:A_W + B_VW] = ob_ref[rows, :]

    outs = []
    for win in range(2):
        base = win * 128
        kc = kc_all[:, base:base + 256]
        vc = vc_all[:, base:base + 256]
        qc = qc_ref[rows, base:base + 256]
        ow = jnp.zeros((tq, 256), F32)
        cmaps = [(hh, mm) for hh in (2 * win, 2 * win + 1) for mm in range(2)]
        for g0 in range(0, len(cmaps), c_stack):
            grp = cmaps[g0:g0 + c_stack]
            qparts = []
            for hh, mm in grp:
                lo = (hh * 2 + mm) * C_DQ - base
                qparts.append(jnp.where((lane2 >= lo) & (lane2 < lo + C_DQ), qc, zero16))
            e, l = softmax_terms(_dot_nt(jnp.concatenate(qparts, axis=0), kc))
            ws, invs = [], []
            for k in range(0, len(grp), 2):
                l0 = l[k * tq:(k + 1) * tq]
                l1 = l[(k + 1) * tq:(k + 2) * tq]
                ws.append((e[k * tq:(k + 1) * tq] - (lam * l0 / l1) * e[(k + 1) * tq:(k + 2) * tq]).astype(BF16))
                invs.append(1.0 / l0)
            rr = _dot(jnp.concatenate(ws, axis=0) if len(ws) > 1 else ws[0], vc)
            for k in range(len(ws)):
                hh = grp[2 * k][0]
                vlo = hh * C_DV - base
                vm = (lane2 >= vlo) & (lane2 < vlo + C_DV)
                ow = ow + jnp.where(vm, rr[k * tq:(k + 1) * tq] * invs[k], 0.0)
        outs.append(ow)
    oc = jnp.concatenate([outs[0][:, :128], outs[0][:, 128:] + outs[1][:, :128], outs[1][:, 128:]], axis=1)
    ms = _dot((oc * oc).astype(BF16), seg_ref[...]) * (1.0 / C_DV)
    oc = oc * lax.rsqrt(ms + EPS) * gco_ref[...] * (1.0 - lam_init)
    mix_scr[rows, A_W + B_VW:] = oc.astype(BF16)


def _attn(x, qa, qc, ob, kv_new, kv_old, prm, seg96, *, l, lam_init, ctx):
    nb, seq, nbb, tq = (NB_CTX, T_CTX, ATT_NBB_CTX, T_CTX) if ctx else (NB_LAT, T_LAT, 1, TQ_ATT_LAT)
    nq = seq // tq
    rb = nbb * tq
    off = 0 if ctx else N_CTX // rb
    qspec = lambda w: pl.BlockSpec((rb, w), lambda b, q: (b * nq + q, 0))
    one_buf = None if ctx else pl.Buffered(1)
    kspec = lambda a: pl.BlockSpec((nbb,) + a.shape[1:], lambda b, q: (b, 0, 0), pipeline_mode=one_buf)
    ospec = lambda a: pl.BlockSpec((None, None) + a.shape[2:], lambda b, q: (b, l, 0, 0), pipeline_mode=one_buf)
    lay = lambda shape: pl.BlockSpec((None,) + shape, lambda b, q: (l,) + (0,) * len(shape),
                                     pipeline_mode=pl.Buffered(1))
    xspec = pl.BlockSpec((rb, D_MODEL), lambda b, q: (off + b * nq + q, 0))
    in_specs = [xspec, qspec(384), qspec(384), qspec(B_VW)] + [kspec(a) for a in kv_new]
    args = [x, qa, qc, ob, *kv_new]
    if kv_old is not None:
        in_specs += [ospec(a) for a in kv_old]
        args += list(kv_old)
    in_specs += [lay((D_MODEL, D_MODEL)),
                 pl.BlockSpec((None, 8, D_MODEL), lambda b, q: (l, 0, 5)),
                 lay((4, C_DQ)), lay((1, 384)),
                 pl.BlockSpec((384, 384), lambda b, q: (0, 0))]
    args += [prm["w_o"], prm["mod"], prm["lam_c"], prm["gco"], seg96]
    return pl.pallas_call(
        functools.partial(_attn_body, lam_init=lam_init, ctx=ctx, nbb=nbb),
        out_shape=jax.ShapeDtypeStruct((N_TOK, D_MODEL), F32),
        grid=(nb // nbb, nq),
        in_specs=in_specs,
        out_specs=xspec,
        scratch_shapes=[pltpu.VMEM((rb, D_MODEL), BF16)],
        input_output_aliases={0: 0},
        compiler_params=_cparams(("parallel", "arbitrary")),
        name="attn_ctx" if ctx else "attn_lat",
    )(*args)


def _block_ones(width, seg):
    idx = np.arange(width) // seg
    return jnp.asarray((idx[:, None] == idx[None, :]).astype(np.float32), dtype=BF16)


def _rot_matrix(width, half):
    m = np.zeros((width, width), np.float32)
    for j in range(width):
        if (j % (2 * half)) < half:
            m[j + half, j] = -1.0
        else:
            m[j - half, j] = 1.0
    return jnp.asarray(m, dtype=BF16)


def _rope_tables():
    t = np.arange(T_LAT)
    row = (t // GRID_W).astype(np.float32)
    col = (t % GRID_W).astype(np.float32)

    def tab(head_dim, n_heads):
        m = head_dim // 4
        freqs = ROPE_BASE ** (-jnp.arange(m, dtype=F32) / m)
        ang_r = jnp.asarray(row)[:, None] * freqs[None, :]
        ang_c = jnp.asarray(col)[:, None] * freqs[None, :]
        cs = jnp.concatenate([jnp.cos(ang_r)] * 2 + [jnp.cos(ang_c)] * 2, axis=1)
        sn = jnp.concatenate([jnp.sin(ang_r)] * 2 + [jnp.sin(ang_c)] * 2, axis=1)
        return jnp.tile(cs, (1, n_heads)), jnp.tile(sn, (1, n_heads))

    cosa, sina = tab(A_D, 2)
    cosc, sinc = tab(C_DQ, 2 * C_H)
    return cosa, sina, cosc, sinc


def kernel(x_prompt, x_sample, c, cache_a_k, cache_a_v, cache_c_k, cache_c_v, state_gla, c_ctx, w_ada, b_ada,
           g_norm, w_ffn_gate, w_ffn_up, w_ffn_down, w_in, g_a_q, g_a_k, w_gla_up, b_gla, g_gla, g_c_q, g_c_k,
           lam_c, g_c_out, w_out):
    lam_inits = [0.8 - 0.6 * math.exp(-0.3 * l) for l in range(DEPTH)]

    consts = {
        "seg64_384": _block_ones(384, 64), "seg64_128": _block_ones(128, 64),
        "seg48_384": _block_ones(384, 48), "seg96_384": _block_ones(384, 96),
        "seg64_256": _block_ones(256, 64),
        "rot_a384": _rot_matrix(384, 16), "rot_a128": _rot_matrix(128, 16), "rot_c384": _rot_matrix(384, 12),
    }
    rope_tabs = _rope_tables()

    wq = w_in[:, :, 0:384].reshape(DEPTH, D_MODEL, A_KV, A_G, A_D).transpose(0, 1, 3, 2, 4).reshape(DEPTH, D_MODEL, 384)
    seg = lambda a, b: w_in[:, :, a:b]
    w_p = jnp.concatenate([
        wq, seg(384, 512), seg(512, 640),
        seg(640, 768), seg(768, 896), seg(896, 1152),
        seg(1184, 1440),
        seg(1440, 1824), seg(1824, 2208), seg(2208, 2592),
        seg(1152, 1184), jnp.zeros((DEPTH, D_MODEL, 96), F32),
    ], axis=2).astype(BF16)
    wo_a = w_out[:, 0:384].reshape(DEPTH, A_KV, A_G, A_D, D_MODEL).transpose(0, 2, 1, 3, 4).reshape(DEPTH, 384, D_MODEL)
    wup = jnp.zeros((DEPTH, 128, 2 * B_KW), F32)
    wup = wup.at[:, 0:B_RANK, 0:B_KW].set(w_gla_up[:, 0]).at[:, B_RANK:2 * B_RANK, B_KW:].set(w_gla_up[:, 1])
    cond8 = jnp.zeros((8, D_MODEL), F32).at[0].set(c_ctx).at[1:3].set(c)
    prm = {
        "mod": _adaln(cond8, w_ada, b_ada),
        "gn4": g_norm.reshape(DEPTH, 3, 1, D_MODEL),
        "w_p": w_p,
        "w_o": jnp.concatenate([wo_a, w_out[:, 384:]], axis=1).astype(BF16),
        "wup": wup.astype(BF16),
        "bgla": b_gla.reshape(DEPTH, 1, 2 * B_KW),
        "gaq": jnp.tile(g_a_q, (1, 6)).reshape(DEPTH, 1, 384),
        "gak": jnp.tile(g_a_k, (1, 2)).reshape(DEPTH, 1, 128),
        "gcq": jnp.tile(g_c_q.reshape(DEPTH, 96), (1, 4)).reshape(DEPTH, 1, 384),
        "gck": jnp.tile(g_c_k.reshape(DEPTH, 96), (1, 4)).reshape(DEPTH, 1, 384),
        "gco": jnp.tile(g_c_out, (1, 4)).reshape(DEPTH, 1, 384),
        "ggl": jnp.tile(g_gla, (1, 4)).reshape(DEPTH, 1, 256),
        "lam_c": lam_c,
    }

    caches, states = None, None
    s0_lat = jnp.swapaxes(state_gla.reshape(NB_LAT, DEPTH, 2, B_KW, B_DV), 3, 4)
    kv_old = [a.reshape(NB_LAT, DEPTH, PAST_LEN, a.shape[-2] * a.shape[-1]).astype(BF16)
              for a in (cache_a_k, cache_a_v, cache_c_k, cache_c_v)]
    ffn = functools.partial(_ffn, mod=prm["mod"], gn4=prm["gn4"], wg=w_ffn_gate, wu=w_ffn_up, wd=w_ffn_down)
    x = None
    for l in range(DEPTH):
        if l == 0:
            x = ffn((x_prompt.reshape(N_CTX, D_MODEL), x_sample.reshape(N_LAT, D_MODEL)), l=0, s=0, first=True)
        else:
            x = ffn((x,), l=l, s=0)

        outs = _proj(x, prm, consts, None, caches, l=l, ctx=True)
        qa, ka, va, qc, kc, vc, gq, gk, gv, gla, gr = outs[:11]
        caches = list(outs[11:])
        ob, states = _gla(gq, gk, gv, gla, gr, None, prm, consts["seg64_256"], states, l=l, ctx=True)
        r3 = lambda a, n, t: a.reshape(n, t, a.shape[-1])
        x = _attn(x, qa, qc, ob, [r3(a, NB_CTX, T_CTX) for a in (ka, va, kc, vc)], None, prm, consts["seg96_384"],
                  l=l, lam_init=lam_inits[l], ctx=True)

        qa, ka, va, qc, kc, vc, gq, gk, gv, gla, gr = _proj(x, prm, consts, rope_tabs, None, l=l, ctx=False)
        ob, _ = _gla(gq, gk, gv, gla, gr, s0_lat[:, l], prm, consts["seg64_256"], None, l=l, ctx=False)
        x = _attn(x, qa, qc, ob, [r3(a, NB_LAT, T_LAT) for a in (ka, va, kc, vc)], kv_old, prm, consts["seg96_384"],
                  l=l, lam_init=lam_inits[l], ctx=False)

        if l == DEPTH - 1:
            y_prompt, y_sample = ffn((x,), l=l, s=1, last=True)
        else:
            x = ffn((x,), l=l, s=1)

    return (y_prompt.reshape(NB_CTX, T_CTX, D_MODEL), y_sample.reshape(NB_LAT, T_LAT, D_MODEL),
            caches[0].reshape(NB_CTX, DEPTH, T_CTX, A_KV, A_D), caches[1].reshape(NB_CTX, DEPTH, T_CTX, A_KV, A_D),
            caches[2].reshape(NB_CTX, DEPTH, T_CTX, C_H, 2 * C_DQ), caches[3].reshape(NB_CTX, DEPTH, T_CTX, C_H, C_DV),
            jnp.swapaxes(states, 3, 4).reshape(NB_CTX, DEPTH, 2, B_H, B_DK, B_DV))
```

```python
import functools
import math

import numpy as np
import jax
import jax.numpy as jnp
from jax import lax
from jax.experimental import pallas as pl
from jax.experimental.pallas import tpu as pltpu

F32 = jnp.float32
BF16 = jnp.bfloat16

D_MODEL = 1024
D_FF = 2816
DEPTH = 4
N_MOD = 9
EPS = 1e-6
ROPE_BASE = 10000.0
GRID_W = 64
LOG2E = math.log2(math.e)

NB_CTX, T_CTX = 32, 256
NB_LAT, T_LAT = 2, 2048
PAST_LEN = 512
N_CTX = NB_CTX * T_CTX
N_LAT = NB_LAT * T_LAT
N_TOK = N_CTX + N_LAT

A_KV, A_G, A_D = 2, 3, 64
A_W = A_KV * A_G * A_D
B_H, B_DK, B_DV = 4, 32, 64
B_KW = B_H * B_DK
B_VW = B_H * B_DV
B_RANK = 16
B_TAU = 16.0
B_CHUNK = 64
C_H, C_DQ, C_DV = 4, 48, 96
C_W = C_H * C_DV

P_AQ, P_AK, P_AV = 0, 384, 512
P_BQ, P_BK, P_BV, P_BR = 640, 768, 896, 1152
P_CQ, P_CK, P_CV = 1408, 1792, 2176
P_BG = 2560
P_W = 2688

TM_FFN = 1024
TF_FFN = 256
TM_PROJ_CTX = 256
TM_PROJ_LAT = 512
TQ_ATT_LAT = 256
ATT_NBB_CTX = 4
TN_ADA = 1152
GLA_NBB_CTX = 8
GLA_NBB_LAT = 2
A_STACK_CTX, A_STACK_LAT = 6, 6
C_STACK_CTX, C_STACK_LAT = 4, 4
VMEM_LIMIT = 56 * 1024 * 1024


def _cparams(sem):
    return pltpu.CompilerParams(dimension_semantics=sem, vmem_limit_bytes=VMEM_LIMIT)


def _dot(a, b):
    return jnp.dot(a, b, preferred_element_type=F32)


def _dot_nt(a, b):
    return lax.dot_general(a, b, (((1,), (1,)), ((), ())), preferred_element_type=F32)


def _dot_tn(a, b):
    return lax.dot_general(a, b, (((0,), (0,)), ((), ())), preferred_element_type=F32)


def _split_dot(x, m):
    hi = x.astype(BF16)
    lo = (x - hi.astype(F32)).astype(BF16)
    return _dot(hi, m) + _dot(lo, m)


def _split_dot_l(m, x):
    hi = x.astype(BF16)
    lo = (x - hi.astype(F32)).astype(BF16)
    return _dot(m, hi) + _dot(m, lo)


def _seg_rms(x, seg, n):
    ms = _dot((x * x).astype(BF16), seg) * (1.0 / n)
    return x * lax.rsqrt(ms + EPS)


def _silu(x):
    return x * jax.nn.sigmoid(x)


def _norm_mod(x, gn, sc, sh):
    ms = jnp.mean(x * x, axis=-1, keepdims=True)
    return ((x * lax.rsqrt(ms + EPS)) * (gn * (1.0 + sc)) + sh).astype(BF16)


def _adaln_body(cond_ref, w_ref, b_ref, o_ref):
    c = cond_ref[...]
    sc = _silu(c).astype(BF16)
    o_ref[0] = _dot(sc, w_ref[0].astype(BF16)) + b_ref[0]


def _adaln(cond8, w_ada, b_ada):
    nj = (N_MOD * D_MODEL) // TN_ADA
    return pl.pallas_call(
        _adaln_body,
        out_shape=jax.ShapeDtypeStruct((DEPTH, 8, N_MOD * D_MODEL), F32),
        grid=(DEPTH, nj),
        in_specs=[
            pl.BlockSpec((8, D_MODEL), lambda l, j: (0, 0)),
            pl.BlockSpec((1, D_MODEL, TN_ADA), lambda l, j: (l, 0, j)),
            pl.BlockSpec((1, 1, TN_ADA), lambda l, j: (l, 0, j)),
        ],
        out_specs=pl.BlockSpec((1, 8, TN_ADA), lambda l, j: (l, 0, j)),
        compiler_params=_cparams(("parallel", "parallel")),
        name="adaln",
    )(cond8, w_ada, b_ada.reshape(DEPTH, 1, N_MOD * D_MODEL))


FFN_TILES_CTX = N_CTX // TM_FFN
FFN_TILES_PER_LAT = T_LAT // TM_FFN


def _ffn_body(*refs, first, last):
    it = iter(refs)
    x_refs = (next(it), next(it)) if first else (next(it),)
    sh_ref, sc_ref, gt_ref, gn_ref, wg_ref, wu_ref, wd_ref = (next(it) for _ in range(7))
    o_refs = (next(it), next(it)) if last else (next(it),)
    h_scr, acc_scr = next(it), next(it)

    i = pl.program_id(0)
    j = pl.program_id(1)
    is_ctx = i < FFN_TILES_CTX
    r = jnp.where(is_ctx, 0, 1 + jnp.maximum(i - FFN_TILES_CTX, 0) // FFN_TILES_PER_LAT)

    def on_tile(pred, n_variants, fn):
        if n_variants == 1:
            pl.when(pred)(lambda: fn(0))
        else:
            pl.when(pred & is_ctx)(lambda: fn(0))
            pl.when(pred & jnp.logical_not(is_ctx))(lambda: fn(1))

    def prologue(k):
        h_scr[...] = _norm_mod(x_refs[k][...], gn_ref[...], sc_ref[pl.ds(r, 1), :], sh_ref[pl.ds(r, 1), :])
        acc_scr[...] = jnp.zeros_like(acc_scr)

    on_tile(j == 0, len(x_refs), prologue)

    h = h_scr[...]
    g = _dot(h, wg_ref[...].astype(BF16))
    u = _dot(h, wu_ref[...].astype(BF16))
    a = (_silu(g) * u).astype(BF16)
    acc_scr[...] += _dot(a, wd_ref[...].astype(BF16))

    def epilogue(k):
        x_ref = x_refs[k if first else 0]
        o_ref = o_refs[k if last else 0]
        o_ref[...] = x_ref[...] + 0.5 * gt_ref[pl.ds(r, 1), :] * acc_scr[...]

    on_tile(j == pl.num_programs(1) - 1, max(len(x_refs), len(o_refs)), epilogue)


def _ffn(xs, mod, gn4, wg, wu, wd, *, l, s, first=False, last=False):
    ni = N_TOK // TM_FFN
    nj = D_FF // TF_FFN
    k0 = 6 * s
    gi = 2 * s
    tc = FFN_TILES_CTX
    split_specs = [pl.BlockSpec((TM_FFN, D_MODEL), lambda i, j: (jnp.minimum(i, tc - 1), 0)),
                   pl.BlockSpec((TM_FFN, D_MODEL), lambda i, j: (jnp.maximum(i - tc, 0), 0))]
    one_spec = [pl.BlockSpec((TM_FFN, D_MODEL), lambda i, j: (i, 0))]
    mspec = lambda k: pl.BlockSpec((None, 8, D_MODEL), lambda i, j: (l, 0, k))
    in_specs = (split_specs if first else one_spec) + [
        mspec(k0), mspec(k0 + 1), mspec(k0 + 2),
        pl.BlockSpec((None, None, 1, D_MODEL), lambda i, j: (l, gi, 0, 0)),
        pl.BlockSpec((None, None, D_MODEL, TF_FFN), lambda i, j: (l, s, 0, j)),
        pl.BlockSpec((None, None, D_MODEL, TF_FFN), lambda i, j: (l, s, 0, j)),
        pl.BlockSpec((None, None, TF_FFN, D_MODEL), lambda i, j: (l, s, j, 0)),
    ]
    if last:
        out_shape = [jax.ShapeDtypeStruct((N_CTX, D_MODEL), F32), jax.ShapeDtypeStruct((N_LAT, D_MODEL), F32)]
        out_specs = split_specs
    else:
        out_shape = jax.ShapeDtypeStruct((N_TOK, D_MODEL), F32)
        out_specs = one_spec[0]
    return pl.pallas_call(
        functools.partial(_ffn_body, first=first, last=last),
        out_shape=out_shape,
        grid=(ni, nj),
        in_specs=in_specs,
        out_specs=out_specs,
        scratch_shapes=[pltpu.VMEM((TM_FFN, D_MODEL), BF16), pltpu.VMEM((TM_FFN, D_MODEL), F32)],
        compiler_params=_cparams(("arbitrary", "arbitrary")),
        name="ffn",
    )(*xs, mod, mod, mod, gn4, wg, wu, wd)


def _proj_body(*refs, rope, ctx, tiles_per_batch, n_alias):
    it = iter(refs)
    x_ref, sh_ref, sc_ref, gn_ref, w_ref, wup_ref, bgla_ref = (next(it) for _ in range(7))
    gaq_ref, gak_ref, gcq_ref, gck_ref = (next(it) for _ in range(4))
    s64a_ref, s64k_ref, s48_ref = (next(it) for _ in range(3))
    if rope:
        cosa_ref, sina_ref, cosc_ref, sinc_ref, ra_ref, rak_ref, rc_ref = (next(it) for _ in range(7))
    for _ in range(n_alias):
        next(it)
    qa_o, ka_o, va_o, qc_o, kc_o, vc_o, gq_o, gk_o, gv_o, gla_o, gr_o = (next(it) for _ in range(11))
    if ctx:
        ka32_o, va32_o, kc32_o, vc32_o = (next(it) for _ in range(4))

    i = pl.program_id(0)
    r = 0 if ctx else 1 + i // tiles_per_batch

    h = _norm_mod(x_ref[...], gn_ref[...], sc_ref[pl.ds(r, 1), :], sh_ref[pl.ds(r, 1), :])
    p = _dot(h, w_ref[...])

    aq = _seg_rms(p[:, P_AQ:P_AQ + A_W], s64a_ref[...], A_D) * gaq_ref[...]
    ak = _seg_rms(p[:, P_AK:P_AK + 128], s64k_ref[...], A_D) * gak_ref[...]
    av = p[:, P_AV:P_AV + 128]
    cq = _seg_rms(p[:, P_CQ:P_CQ + 384], s48_ref[...], C_DQ) * gcq_ref[...]
    ck = _seg_rms(p[:, P_CK:P_CK + 384], s48_ref[...], C_DQ) * gck_ref[...]
    cv = p[:, P_CV:P_CV + 384]
    if ctx:
        for o, val in ((ka32_o, ak), (va32_o, av), (kc32_o, ck), (vc32_o, cv)):
            if n_alias:
                o[...] = val
            else:
                o[0] = val
                o[1:] = jnp.zeros((DEPTH - 1,) + val.shape, F32)
    if rope:
        cosa = cosa_ref[...]
        sina = sina_ref[...]
        cosa3 = jnp.concatenate([cosa, cosa, cosa], axis=1)
        sina3 = jnp.concatenate([sina, sina, sina], axis=1)
        aq = aq * cosa3 + _dot(aq.astype(BF16), ra_ref[...]) * sina3
        ak = ak * cosa + _dot(ak.astype(BF16), rak_ref[...]) * sina
        cosc = cosc_ref[...]
        sinc = sinc_ref[...]
        cq = cq * cosc + _dot(cq.astype(BF16), rc_ref[...]) * sinc
        ck = ck * cosc + _dot(ck.astype(BF16), rc_ref[...]) * sinc
    qa_o[...] = (aq * (A_D ** -0.5 * LOG2E)).astype(BF16)
    ka_o[...] = ak.astype(BF16)
    va_o[...] = av.astype(BF16)
    qc_o[...] = (cq * (C_DQ ** -0.5 * LOG2E)).astype(BF16)
    kc_o[...] = ck.astype(BF16)
    vc_o[...] = cv.astype(BF16)

    gq_o[...] = p[:, P_BQ:P_BQ + B_KW] * (B_DK ** -0.5)
    gk_o[...] = p[:, P_BK:P_BK + B_KW]
    gv_o[...] = p[:, P_BV:P_BV + B_VW]
    gr_o[...] = _silu(p[:, P_BR:P_BR + B_VW])
    z = _dot(p[:, P_BG:P_BG + 128].astype(BF16), wup_ref[...]) + bgla_ref[...]
    log_sig = jnp.minimum(z, 0.0) - jnp.log1p(jnp.exp(-jnp.abs(z)))
    gla_o[...] = log_sig * (1.0 / B_TAU)


def _proj(x, prm, consts, rope_tabs, caches, *, l, ctx):
    n = N_CTX if ctx else N_LAT
    tm = TM_PROJ_CTX if ctx else TM_PROJ_LAT
    off = 0 if ctx else N_CTX // tm
    tiles_per_batch = (T_CTX if ctx else T_LAT) // tm
    rope = not ctx
    full = lambda shape: pl.BlockSpec(shape, lambda i: (0,) * len(shape))
    lay = lambda shape: pl.BlockSpec((None,) + shape, lambda i: (l,) + (0,) * len(shape))
    in_specs = [
        pl.BlockSpec((tm, D_MODEL), lambda i: (i + off, 0)),
        pl.BlockSpec((None, 8, D_MODEL), lambda i: (l, 0, 3)),
        pl.BlockSpec((None, 8, D_MODEL), lambda i: (l, 0, 4)),
        pl.BlockSpec((None, None, 1, D_MODEL), lambda i: (l, 1, 0, 0)),
        lay((D_MODEL, P_W)),
        lay((128, 2 * B_KW)),
        lay((1, 2 * B_KW)),
        lay((1, 384)), lay((1, 128)), lay((1, 384)), lay((1, 384)),
        full((384, 384)), full((128, 128)), full((384, 384)),
    ]
    args = [x, prm["mod"], prm["mod"], prm["gn4"], prm["w_p"], prm["wup"], prm["bgla"],
            prm["gaq"], prm["gak"], prm["gcq"], prm["gck"],
            consts["seg64_384"], consts["seg64_128"], consts["seg48_384"]]
    if rope:
        tpb = tiles_per_batch
        in_specs += [
            pl.BlockSpec((tm, 128), lambda i: (i % tpb, 0)),
            pl.BlockSpec((tm, 128), lambda i: (i % tpb, 0)),
            pl.BlockSpec((tm, 384), lambda i: (i % tpb, 0)),
            pl.BlockSpec((tm, 384), lambda i: (i % tpb, 0)),
            full((384, 384)), full((128, 128)), full((384, 384)),
        ]
        args += [*rope_tabs, consts["rot_a384"], consts["rot_a128"], consts["rot_c384"]]
    widths = [(384, BF16), (128, BF16), (128, BF16), (384, BF16), (384, BF16), (384, BF16),
              (B_KW, F32), (B_KW, F32), (B_VW, F32), (2 * B_KW, F32), (B_VW, F32)]
    out_shape = [jax.ShapeDtypeStruct((n, w), dt) for w, dt in widths]
    out_specs = [pl.BlockSpec((tm, w), lambda i: (i, 0)) for w, _ in widths]
    aliases = {}
    if ctx:
        assert tm == T_CTX
        for k, w in enumerate((128, 128, 384, 384)):
            if caches is not None:
                aliases[len(args)] = len(out_shape)
                in_specs.append(pl.BlockSpec(memory_space=pl.ANY))
                args.append(caches[k])
            out_shape.append(jax.ShapeDtypeStruct((NB_CTX, DEPTH, T_CTX, w), F32))
            if caches is not None:
                out_specs.append(pl.BlockSpec((None, None, T_CTX, w), lambda i: (i, l, 0, 0)))
            else:
                assert l == 0
                out_specs.append(pl.BlockSpec((None, DEPTH, T_CTX, w), lambda i: (i, 0, 0, 0)))
    return pl.pallas_call(
        functools.partial(_proj_body, rope=rope, ctx=ctx, tiles_per_batch=tiles_per_batch, n_alias=len(aliases)),
        out_shape=out_shape,
        grid=(n // tm,),
        in_specs=in_specs,
        out_specs=out_specs,
        input_output_aliases=aliases,
        compiler_params=_cparams(("parallel",)),
        name="proj_ctx" if ctx else "proj_lat",
    )(*args)


def _gla_tools(gq_ref, gk_ref, gv_ref, gla_ref):
    C = B_CHUNK
    row_i = lax.broadcasted_iota(jnp.int32, (C, B_KW), 0)
    rk = lax.broadcasted_iota(jnp.int32, (B_H * C, B_KW), 0) >> 6
    ck = lax.broadcasted_iota(jnp.int32, (B_H * C, B_KW), 1) >> 5
    hm_k = rk == ck
    rv = lax.broadcasted_iota(jnp.int32, (B_H * C, B_VW), 0) >> 6
    cv = lax.broadcasted_iota(jnp.int32, (B_H * C, B_VW), 1) >> 6
    hm_v = rv == cv
    ra = lax.broadcasted_iota(jnp.int32, (C, B_H * C), 0)
    ca = lax.broadcasted_iota(jnp.int32, (C, B_H * C), 1) & (C - 1)
    tri = (ca <= ra, ca >= ra)
    zero16 = jnp.zeros((), BF16)

    def expand_state(st):
        return jnp.where(hm_k, jnp.concatenate([st] * B_H, axis=0), 0.0)

    def chunk(d, start, S):
        rows = pl.ds(start if isinstance(start, int) else pl.multiple_of(start, C), C)
        q = gq_ref[rows, :]
        k = gk_ref[rows, :]
        v = gv_ref[rows, :].astype(BF16)
        la = gla_ref[rows, d * B_KW:(d + 1) * B_KW]
        p = la
        for s in (1, 2, 4, 8, 16, 32):
            p = p + jnp.where(row_i >= s, pltpu.roll(p, s, axis=0), 0.0)
        tot = p[C - 1:C, :]
        b = p if d == 0 else tot - p + la
        e = b - b[C // 2:C // 2 + 1, :]
        qt = (q * jnp.exp(e)).astype(BF16)
        kt = (k * jnp.exp(-e)).astype(BF16)
        qd = (q * jnp.exp(b)).astype(BF16)
        kd = (k * jnp.exp(tot - b)).astype(BF16)
        kbd = jnp.where(hm_k, jnp.concatenate([kt] * B_H, axis=0), zero16)
        a = _dot_nt(qt, kbd)
        a = jnp.where(tri[d], a, 0.0).astype(BF16)
        vbd = jnp.where(hm_v, jnp.concatenate([v] * B_H, axis=0), zero16)
        o = _dot(a, vbd) + _dot_nt(qd, S.astype(BF16))
        kv = _dot_tn(v, kd)
        S_new = S * jnp.exp(tot) + jnp.where(hm_k, kv, 0.0)
        return rows, o, S_new

    def collapse_state(S):
        Sm = jnp.where(hm_k, S, 0.0)
        acc = Sm[0:C, :]
        for hh in range(1, B_H):
            acc = acc + Sm[hh * C:(hh + 1) * C, :]
        return acc

    return expand_state, chunk, collapse_state


def _gla_gate(o, seg, gg, gr):
    ms = _split_dot(o * o, seg) * (1.0 / B_DV)
    return (o * lax.rsqrt(ms + EPS) * gg * gr).astype(BF16)


def _gla_body(*refs, seq, nbb, has_s0, has_alias, zero_fill):
    it = iter(refs)
    gq_ref, gk_ref, gv_ref, gla_ref, gr_ref = (next(it) for _ in range(5))
    s0_ref = next(it) if has_s0 else None
    gg_ref, seg_ref = next(it), next(it)
    if has_alias:
        next(it)
    ob_ref, sfin_ref, of_scr, or_scr = (next(it) for _ in range(4))
    expand_state, chunk, collapse_state = _gla_tools(gq_ref, gk_ref, gv_ref, gla_ref)
    n = seq // B_CHUNK

    def step(i, carry):
        new = []
        for bb in range(nbb):
            rows, o, Sf = chunk(0, bb * seq + i * B_CHUNK, carry[2 * bb])
            of_scr[rows, :] = o
            rows, o, Sb = chunk(1, bb * seq + (n - 1 - i) * B_CHUNK, carry[2 * bb + 1])
            or_scr[rows, :] = o
            new += [Sf, Sb]
        return tuple(new)

    if has_s0:
        carry = tuple(expand_state(s0_ref[bb, d]) for bb in range(nbb) for d in range(2))
    else:
        carry = tuple(jnp.zeros((B_VW, B_KW), F32) for _ in range(2 * nbb))
    if n <= 4:
        for i in range(n):
            carry = step(i, carry)
    else:
        carry = lax.fori_loop(0, n, step, carry, unroll=2)
    for bb in range(nbb):
        if zero_fill:
            sfin_ref[bb, 0, 0] = collapse_state(carry[2 * bb])
            sfin_ref[bb, 0, 1] = collapse_state(carry[2 * bb + 1])
            sfin_ref[bb, 1:] = jnp.zeros((DEPTH - 1, 2, B_DV, B_KW), F32)
        else:
            sfin_ref[bb, 0] = collapse_state(carry[2 * bb])
            sfin_ref[bb, 1] = collapse_state(carry[2 * bb + 1])

    RT = 256

    def fin(t, _):
        rows = pl.ds(pl.multiple_of(t * RT, RT), RT)
        ob_ref[rows, :] = _gla_gate(of_scr[rows, :] + or_scr[rows, :], seg_ref[...], gg_ref[...], gr_ref[rows, :])
        return 0

    lax.fori_loop(0, nbb * seq // RT, fin, 0)


def _gla(gq, gk, gv, gla, gr, s0, prm, seg, states, *, l, ctx):
    nb, seq, nbb = (NB_CTX, T_CTX, GLA_NBB_CTX) if ctx else (NB_LAT, T_LAT, GLA_NBB_LAT)
    tok = lambda w: pl.BlockSpec((nbb * seq, w), lambda b: (b, 0))
    in_specs = [tok(B_KW), tok(B_KW), tok(B_VW), tok(2 * B_KW), tok(B_VW)]
    args = [gq, gk, gv, gla, gr]
    if s0 is not None:
        in_specs.append(pl.BlockSpec((nbb, 2, B_DV, B_KW), lambda b: (b, 0, 0, 0)))
        args.append(s0)
    in_specs += [pl.BlockSpec((None, 1, B_VW), lambda b: (l, 0, 0)), pl.BlockSpec((B_VW, B_VW), lambda b: (0, 0))]
    args += [prm["ggl"], seg]
    aliases = {}
    if ctx:
        st_shape = jax.ShapeDtypeStruct((NB_CTX, DEPTH, 2, B_DV, B_KW), F32)
        st_spec = pl.BlockSpec((nbb, None, 2, B_DV, B_KW), lambda b: (b, l, 0, 0, 0))
        if states is None:
            assert l == 0
            st_spec = pl.BlockSpec((nbb, DEPTH, 2, B_DV, B_KW), lambda b: (b, 0, 0, 0, 0))
        if states is not None:
            aliases[len(args)] = 1
            in_specs.append(pl.BlockSpec(memory_space=pl.ANY))
            args.append(states)
    else:
        st_shape = jax.ShapeDtypeStruct((nb, 2, B_DV, B_KW), F32)
        st_spec = pl.BlockSpec((nbb, 2, B_DV, B_KW), lambda b: (b, 0, 0, 0))
    return pl.pallas_call(
        functools.partial(_gla_body, seq=seq, nbb=nbb, has_s0=s0 is not None, has_alias=bool(aliases),
                          zero_fill=ctx and states is None),
        out_shape=[jax.ShapeDtypeStruct((nb * seq, B_VW), BF16), st_shape],
        grid=(nb // nbb,),
        in_specs=in_specs,
        out_specs=[tok(B_VW), st_spec],
        scratch_shapes=[pltpu.VMEM((nbb * seq, B_VW), F32), pltpu.VMEM((nbb * seq, B_VW), F32)],
        input_output_aliases=aliases,
        compiler_params=_cparams(("parallel",)),
        name="gla_ctx" if ctx else "gla_lat",
    )(*args)


def _attn_body(*refs, lam_init, ctx, nbb):
    it = iter(refs)
    x_ref, qa_ref, qc_ref, ob_ref = (next(it) for _ in range(4))
    kv_new = [next(it) for _ in range(4)]
    kv_old = None if ctx else [next(it) for _ in range(4)]
    wout_ref, gt_ref, lam_ref, gco_ref, seg_ref, o_ref, mix_scr = (next(it) for _ in range(7))

    def keys_values(bb):
        if ctx:
            return [r[bb] for r in kv_new]
        return [jnp.concatenate([rn[bb], ro[...]], axis=0) for rn, ro in zip(kv_new, kv_old)]

    b = pl.program_id(0)
    r = 0 if ctx else 1 + b
    tq = x_ref.shape[0] // nbb
    a_stack, c_stack = (A_STACK_CTX, C_STACK_CTX) if ctx else (A_STACK_LAT, C_STACK_LAT)
    lane = lax.broadcasted_iota(jnp.int32, (1, 128), 1)
    lane2 = lax.broadcasted_iota(jnp.int32, (1, 256), 1)
    hmask = [(lane >= hh * A_D) & (lane < (hh + 1) * A_D) for hh in range(A_KV)]
    lm = lam_ref[...]
    lam = (jnp.exp(jnp.sum(lm[0:1] * lm[1:2], axis=-1, keepdims=True))
           - jnp.exp(jnp.sum(lm[2:3] * lm[3:4], axis=-1, keepdims=True)) + lam_init)

    def softmax_terms(s):
        e = jnp.exp2(s - jnp.max(s, axis=-1, keepdims=True))
        return e, jnp.sum(e, axis=-1, keepdims=True)

    for bb in range(nbb):
        _attn_mix_rows(bb, tq, qa_ref, qc_ref, ob_ref, keys_values(bb), gco_ref, seg_ref, mix_scr,
                       a_stack, c_stack, hmask, lane2, lam, lam_init, softmax_terms)
    mixed = _dot(mix_scr[...], wout_ref[...])
    o_ref[...] = x_ref[...] + gt_ref[pl.ds(r, 1), :] * mixed


def _attn_mix_rows(bb, tq, qa_ref, qc_ref, ob_ref, kv, gco_ref, seg_ref, mix_scr,
                   a_stack, c_stack, hmask, lane2, lam, lam_init, softmax_terms):
    rows = slice(bb * tq, (bb + 1) * tq)
    zero16 = jnp.zeros((), BF16)
    ka, va, kc_all, vc_all = kv
    va_ones = jnp.concatenate([va, jnp.ones_like(va)], axis=1)

    maps = [(g, hh) for g in range(A_G) for hh in range(A_KV)]
    acc = [jnp.zeros((tq, 128), F32) for _ in range(A_G)]
    for g0 in range(0, len(maps), a_stack):
        grp = maps[g0:g0 + a_stack]
        qs = jnp.concatenate(
            [jnp.where(hmask[hh], qa_ref[rows, g * 128:(g + 1) * 128], zero16) for g, hh in grp], axis=0)
        s = _dot_nt(qs, ka)
        e = jnp.exp2(s - jnp.max(s, axis=-1, keepdims=True)).astype(BF16)
        oe = _dot(e, va_ones)
        o = oe[:, :128] * (1.0 / oe[:, 128:])
        for k, (g, hh) in enumerate(grp):
            acc[g] = acc[g] + jnp.where(hmask[hh], o[k * tq:(k + 1) * tq], 0.0)
    for g in range(A_G):
        mix_scr[rows, g * 128:(g + 1) * 128] = acc[g].astype(BF16)

    mix_scr[rows, A_W:A_W + B_VW] = ob_ref[rows, :]

    outs = []
    for win in range(2):
        base = win * 128
        kc = kc_all[:, base:base + 256]
        vc = vc_all[:, base:base + 256]
        qc = qc_ref[rows, base:base + 256]
        ow = jnp.zeros((tq, 256), F32)
        cmaps = [(hh, mm) for hh in (2 * win, 2 * win + 1) for mm in range(2)]
        for g0 in range(0, len(cmaps), c_stack):
            grp = cmaps[g0:g0 + c_stack]
            qparts = []
            for hh, mm in grp:
                lo = (hh * 2 + mm) * C_DQ - base
                qparts.append(jnp.where((lane2 >= lo) & (lane2 < lo + C_DQ), qc, zero16))
            e, l = softmax_terms(_dot_nt(jnp.concatenate(qparts, axis=0), kc))
            ws, invs = [], []
            for k in range(0, len(grp), 2):
                l0 = l[k * tq:(k + 1) * tq]
                l1 = l[(k + 1) * tq:(k + 2) * tq]
                ws.append((e[k * tq:(k + 1) * tq] - (lam * l0 / l1) * e[(k + 1) * tq:(k + 2) * tq]).astype(BF16))
                invs.append(1.0 / l0)
            rr = _dot(jnp.concatenate(ws, axis=0) if len(ws) > 1 else ws[0], vc)
            for k in range(len(ws)):
                hh = grp[2 * k][0]
                vlo = hh * C_DV - base
                vm = (lane2 >= vlo) & (lane2 < vlo + C_DV)
                ow = ow + jnp.where(vm, rr[k * tq:(k + 1) * tq] * invs[k], 0.0)
        outs.append(ow)
    oc = jnp.concatenate([outs[0][:, :128], outs[0][:, 128:] + outs[1][:, :128], outs[1][:, 128:]], axis=1)
    ms = _dot((oc * oc).astype(BF16), seg_ref[...]) * (1.0 / C_DV)
    oc = oc * lax.rsqrt(ms + EPS) * gco_ref[...] * (1.0 - lam_init)
    mix_scr[rows, A_W + B_VW:] = oc.astype(BF16)


def _attn(x, qa, qc, ob, kv_new, kv_old, prm, seg96, *, l, lam_init, ctx):
    nb, seq, nbb, tq = (NB_CTX, T_CTX, ATT_NBB_CTX, T_CTX) if ctx else (NB_LAT, T_LAT, 1, TQ_ATT_LAT)
    nq = seq // tq
    rb = nbb * tq
    off = 0 if ctx else N_CTX // rb
    qspec = lambda w: pl.BlockSpec((rb, w), lambda b, q: (b * nq + q, 0))
    one_buf = None if ctx else pl.Buffered(1)
    kspec = lambda a: pl.BlockSpec((nbb,) + a.shape[1:], lambda b, q: (b, 0, 0), pipeline_mode=one_buf)
    ospec = lambda a: pl.BlockSpec((None, None) + a.shape[2:], lambda b, q: (b, l, 0, 0), pipeline_mode=one_buf)
    lay = lambda shape: pl.BlockSpec((None,) + shape, lambda b, q: (l,) + (0,) * len(shape),
                                     pipeline_mode=pl.Buffered(1))
    xspec = pl.BlockSpec((rb, D_MODEL), lambda b, q: (off + b * nq + q, 0))
    in_specs = [xspec, qspec(384), qspec(384), qspec(B_VW)] + [kspec(a) for a in kv_new]
    args = [x, qa, qc, ob, *kv_new]
    if kv_old is not None:
        in_specs += [ospec(a) for a in kv_old]
        args += list(kv_old)
    in_specs += [lay((D_MODEL, D_MODEL)),
                 pl.BlockSpec((None, 8, D_MODEL), lambda b, q: (l, 0, 5)),
                 lay((4, C_DQ)), lay((1, 384)),
                 pl.BlockSpec((384, 384), lambda b, q: (0, 0))]
    args += [prm["w_o"], prm["mod"], prm["lam_c"], prm["gco"], seg96]
    return pl.pallas_call(
        functools.partial(_attn_body, lam_init=lam_init, ctx=ctx, nbb=nbb),
        out_shape=jax.ShapeDtypeStruct((N_TOK, D_MODEL), F32),
        grid=(nb // nbb, nq),
        in_specs=in_specs,
        out_specs=xspec,
        scratch_shapes=[pltpu.VMEM((rb, D_MODEL), BF16)],
        input_output_aliases={0: 0},
        compiler_params=_cparams(("parallel", "arbitrary")),
        name="attn_ctx" if ctx else "attn_lat",
    )(*args)


def _block_ones(width, seg):
    idx = np.arange(width) // seg
    return jnp.asarray((idx[:, None] == idx[None, :]).astype(np.float32), dtype=BF16)


def _rot_matrix(width, half):
    m = np.zeros((width, width), np.float32)
    for j in range(width):
        if (j % (2 * half)) < half:
            m[j + half, j] = -1.0
        else:
            m[j - half, j] = 1.0
    return jnp.asarray(m, dtype=BF16)


def _rope_tables():
    t = np.arange(T_LAT)
    row = (t // GRID_W).astype(np.float32)
    col = (t % GRID_W).astype(np.float32)

    def tab(head_dim, n_heads):
        m = head_dim // 4
        freqs = ROPE_BASE ** (-jnp.arange(m, dtype=F32) / m)
        ang_r = jnp.asarray(row)[:, None] * freqs[None, :]
        ang_c = jnp.asarray(col)[:, None] * freqs[None, :]
        cs = jnp.concatenate([jnp.cos(ang_r)] * 2 + [jnp.cos(ang_c)] * 2, axis=1)
        sn = jnp.concatenate([jnp.sin(ang_r)] * 2 + [jnp.sin(ang_c)] * 2, axis=1)
        return jnp.tile(cs, (1, n_heads)), jnp.tile(sn, (1, n_heads))

    cosa, sina = tab(A_D, 2)
    cosc, sinc = tab(C_DQ, 2 * C_H)
    return cosa, sina, cosc, sinc


def kernel(x_prompt, x_sample, c, cache_a_k, cache_a_v, cache_c_k, cache_c_v, state_gla, c_ctx, w_ada, b_ada,
           g_norm, w_ffn_gate, w_ffn_up, w_ffn_down, w_in, g_a_q, g_a_k, w_gla_up, b_gla, g_gla, g_c_q, g_c_k,
           lam_c, g_c_out, w_out):
    lam_inits = [0.8 - 0.6 * math.exp(-0.3 * l) for l in range(DEPTH)]

    consts = {
        "seg64_384": _block_ones(384, 64), "seg64_128": _block_ones(128, 64),
        "seg48_384": _block_ones(384, 48), "seg96_384": _block_ones(384, 96),
        "seg64_256": _block_ones(256, 64),
        "rot_a384": _rot_matrix(384, 16), "rot_a128": _rot_matrix(128, 16), "rot_c384": _rot_matrix(384, 12),
    }
    rope_tabs = _rope_tables()

    wq = w_in[:, :, 0:384].reshape(DEPTH, D_MODEL, A_KV, A_G, A_D).transpose(0, 1, 3, 2, 4).reshape(DEPTH, D_MODEL, 384)
    seg = lambda a, b: w_in[:, :, a:b]
    w_p = jnp.concatenate([
        wq, seg(384, 512), seg(512, 640),
        seg(640, 768), seg(768, 896), seg(896, 1152),
        seg(1184, 1440),
        seg(1440, 1824), seg(1824, 2208), seg(2208, 2592),
        seg(1152, 1184), jnp.zeros((DEPTH, D_MODEL, 96), F32),
    ], axis=2).astype(BF16)
    wo_a = w_out[:, 0:384].reshape(DEPTH, A_KV, A_G, A_D, D_MODEL).transpose(0, 2, 1, 3, 4).reshape(DEPTH, 384, D_MODEL)
    wup = jnp.zeros((DEPTH, 128, 2 * B_KW), F32)
    wup = wup.at[:, 0:B_RANK, 0:B_KW].set(w_gla_up[:, 0]).at[:, B_RANK:2 * B_RANK, B_KW:].set(w_gla_up[:, 1])
    cond8 = jnp.zeros((8, D_MODEL), F32).at[0].set(c_ctx).at[1:3].set(c)
    prm = {
        "mod": _adaln(cond8, w_ada, b_ada),
        "gn4": g_norm.reshape(DEPTH, 3, 1, D_MODEL),
        "w_p": w_p,
        "w_o": jnp.concatenate([wo_a, w_out[:, 384:]], axis=1).astype(BF16),
        "wup": wup.astype(BF16),
        "bgla": b_gla.reshape(DEPTH, 1, 2 * B_KW),
        "gaq": jnp.tile(g_a_q, (1, 6)).reshape(DEPTH, 1, 384),
        "gak": jnp.tile(g_a_k, (1, 2)).reshape(DEPTH, 1, 128),
        "gcq": jnp.tile(g_c_q.reshape(DEPTH, 96), (1, 4)).reshape(DEPTH, 1, 384),
        "gck": jnp.tile(g_c_k.reshape(DEPTH, 96), (1, 4)).reshape(DEPTH, 1, 384),
        "gco": jnp.tile(g_c_out, (1, 4)).reshape(DEPTH, 1, 384),
        "ggl": jnp.tile(g_gla, (1, 4)).reshape(DEPTH, 1, 256),
        "lam_c": lam_c,
    }

    caches, states = None, None
    s0_lat = jnp.swapaxes(state_gla.reshape(NB_LAT, DEPTH, 2, B_KW, B_DV), 3, 4)
    kv_old = [a.reshape(NB_LAT, DEPTH, PAST_LEN, a.shape[-2] * a.shape[-1]).astype(BF16)
              for a in (cache_a_k, cache_a_v, cache_c_k, cache_c_v)]
    ffn = functools.partial(_ffn, mod=prm["mod"], gn4=prm["gn4"], wg=w_ffn_gate, wu=w_ffn_up, wd=w_ffn_down)
    x = None
    for l in range(DEPTH):
        if l == 0:
            x = ffn((x_prompt.reshape(N_CTX, D_MODEL), x_sample.reshape(N_LAT, D_MODEL)), l=0, s=0, first=True)
        else:
            x = ffn((x,), l=l, s=0)

        outs = _proj(x, prm, consts, None, caches, l=l, ctx=True)
        qa, ka, va, qc, kc, vc, gq, gk, gv, gla, gr = outs[:11]
        caches = list(outs[11:])
        ob, states = _gla(gq, gk, gv, gla, gr, None, prm, consts["seg64_256"], states, l=l, ctx=True)
        r3 = lambda a, n, t: a.reshape(n, t, a.shape[-1])
        x = _attn(x, qa, qc, ob, [r3(a, NB_CTX, T_CTX) for a in (ka, va, kc, vc)], None, prm, consts["seg96_384"],
                  l=l, lam_init=lam_inits[l], ctx=True)

        qa, ka, va, qc, kc, vc, gq, gk, gv, gla, gr = _proj(x, prm, consts, rope_tabs, None, l=l, ctx=False)
        ob, _ = _gla(gq, gk, gv, gla, gr, s0_lat[:, l], prm, consts["seg64_256"], None, l=l, ctx=False)
        x = _attn(x, qa, qc, ob, [r3(a, NB_LAT, T_LAT) for a in (ka, va, kc, vc)], kv_old, prm, consts["seg96_384"],
                  l=l, lam_init=lam_inits[l], ctx=False)

        if l == DEPTH - 1:
            y_prompt, y_sample = ffn((x,), l=l, s=1, last=True)
        else:
            x = ffn((x,), l=l, s=1)

    return (y_prompt.reshape(NB_CTX, T_CTX, D_MODEL), y_sample.reshape(NB_LAT, T_LAT, D_MODEL),
            caches[0].reshape(NB_CTX, DEPTH, T_CTX, A_KV, A_D), caches[1].reshape(NB_CTX, DEPTH, T_CTX, A_KV, A_D),
            caches[2].reshape(NB_CTX, DEPTH, T_CTX, C_H, 2 * C_DQ), caches[3].reshape(NB_CTX, DEPTH, T_CTX, C_H, C_DV),
            jnp.swapaxes(states, 3, 4).reshape(NB_CTX, DEPTH, 2, B_H, B_DK, B_DV))
```

```python
import functools
import math

import numpy as np
import jax
import jax.numpy as jnp
from jax import lax
from jax.experimental import pallas as pl
from jax.experimental.pallas import tpu as pltpu

F32 = jnp.float32
BF16 = jnp.bfloat16

D_MODEL = 1024
D_FF = 2816
DEPTH = 4
N_MOD = 9
EPS = 1e-6
ROPE_BASE = 10000.0
GRID_W = 64
LOG2E = math.log2(math.e)

NB_CTX, T_CTX = 32, 256
NB_LAT, T_LAT = 2, 2048
PAST_LEN = 512
N_CTX = NB_CTX * T_CTX
N_LAT = NB_LAT * T_LAT
N_TOK = N_CTX + N_LAT

A_KV, A_G, A_D = 2, 3, 64
A_W = A_KV * A_G * A_D
B_H, B_DK, B_DV = 4, 32, 64
B_KW = B_H * B_DK
B_VW = B_H * B_DV
B_RANK = 16
B_TAU = 16.0
B_CHUNK = 64
C_H, C_DQ, C_DV = 4, 48, 96

P_AQ, P_AK, P_AV = 0, 384, 512
P_BQ, P_BK, P_BV, P_BR = 640, 768, 896, 1152
P_CQ, P_CK, P_CV = 1408, 1792, 2176
P_BG = 2560
P_W = 2688

TM_FFN = 1024
TF_FFN = 256
TM_PROJ_CTX = 256
TM_PROJ_LAT = 512
TQ_ATT_LAT = 512
ATT_NBB_CTX = 4
TN_ADA = 1152
GLA_NBB_CTX = 8
GLA_NBB_LAT = 2
A_STACK_CTX, A_STACK_LAT = 6, 2
C_STACK_CTX, C_STACK_LAT = 4, 2
VMEM_LIMIT = 56 * 1024 * 1024


def _cparams(sem):
    return pltpu.CompilerParams(dimension_semantics=sem, vmem_limit_bytes=VMEM_LIMIT)


def _dot(a, b):
    return jnp.dot(a, b, preferred_element_type=F32)


def _dot_nt(a, b):
    return lax.dot_general(a, b, (((1,), (1,)), ((), ())), preferred_element_type=F32)


def _dot_tn(a, b):
    return lax.dot_general(a, b, (((0,), (0,)), ((), ())), preferred_element_type=F32)


def _split_dot(x, m):
    hi = x.astype(BF16)
    lo = (x - hi.astype(F32)).astype(BF16)
    return _dot(hi, m) + _dot(lo, m)


def _seg_rms(x, seg, n):
    ms = _dot((x * x).astype(BF16), seg) * (1.0 / n)
    return x * lax.rsqrt(ms + EPS)


def _silu(x):
    return x * jax.nn.sigmoid(x)


def _norm_mod(x, gn, sc, sh):
    ms = jnp.mean(x * x, axis=-1, keepdims=True)
    return ((x * lax.rsqrt(ms + EPS)) * (gn * (1.0 + sc)) + sh).astype(BF16)


def _adaln_body(cond_ref, w_ref, b_ref, o_ref):
    c = cond_ref[...]
    sc = _silu(c).astype(BF16)
    o_ref[0] = _dot(sc, w_ref[0].astype(BF16)) + b_ref[0]


def _adaln(cond8, w_ada, b_ada):
    nj = (N_MOD * D_MODEL) // TN_ADA
    return pl.pallas_call(
        _adaln_body,
        out_shape=jax.ShapeDtypeStruct((DEPTH, 8, N_MOD * D_MODEL), F32),
        grid=(DEPTH, nj),
        in_specs=[
            pl.BlockSpec((8, D_MODEL), lambda l, j: (0, 0)),
            pl.BlockSpec((1, D_MODEL, TN_ADA), lambda l, j: (l, 0, j)),
            pl.BlockSpec((1, 1, TN_ADA), lambda l, j: (l, 0, j)),
        ],
        out_specs=pl.BlockSpec((1, 8, TN_ADA), lambda l, j: (l, 0, j)),
        compiler_params=_cparams(("parallel", "parallel")),
        name="adaln",
    )(cond8, w_ada, b_ada.reshape(DEPTH, 1, N_MOD * D_MODEL))


FFN_TILES_CTX = N_CTX // TM_FFN
FFN_TILES_PER_LAT = T_LAT // TM_FFN


def _ffn_body(*refs, first, last):
    it = iter(refs)
    x_refs = (next(it), next(it)) if first else (next(it),)
    sh_ref, sc_ref, gt_ref, gn_ref, wg_ref, wu_ref, wd_ref = (next(it) for _ in range(7))
    o_refs = (next(it), next(it)) if last else (next(it),)
    h_scr, acc_scr = next(it), next(it)

    i = pl.program_id(0)
    j = pl.program_id(1)
    is_ctx = i < FFN_TILES_CTX
    r = jnp.where(is_ctx, 0, 1 + jnp.maximum(i - FFN_TILES_CTX, 0) // FFN_TILES_PER_LAT)

    def on_tile(pred, n_variants, fn):
        if n_variants == 1:
            pl.when(pred)(lambda: fn(0))
        else:
            pl.when(pred & is_ctx)(lambda: fn(0))
            pl.when(pred & jnp.logical_not(is_ctx))(lambda: fn(1))

    def prologue(k):
        h_scr[...] = _norm_mod(x_refs[k][...], gn_ref[...], sc_ref[pl.ds(r, 1), :], sh_ref[pl.ds(r, 1), :])
        acc_scr[...] = jnp.zeros_like(acc_scr)

    on_tile(j == 0, len(x_refs), prologue)

    h = h_scr[...]
    g = _dot(h, wg_ref[...].astype(BF16))
    u = _dot(h, wu_ref[...].astype(BF16))
    a = (_silu(g) * u).astype(BF16)
    acc_scr[...] += _dot(a, wd_ref[...].astype(BF16))

    def epilogue(k):
        x_ref = x_refs[k if first else 0]
        o_ref = o_refs[k if last else 0]
        o_ref[...] = x_ref[...] + 0.5 * gt_ref[pl.ds(r, 1), :] * acc_scr[...]

    on_tile(j == pl.num_programs(1) - 1, max(len(x_refs), len(o_refs)), epilogue)


def _ffn(xs, mod, gn4, wg, wu, wd, *, l, s, first=False, last=False):
    ni = N_TOK // TM_FFN
    nj = D_FF // TF_FFN
    k0 = 6 * s
    gi = 2 * s
    tc = FFN_TILES_CTX
    split_specs = [pl.BlockSpec((TM_FFN, D_MODEL), lambda i, j: (jnp.minimum(i, tc - 1), 0)),
                   pl.BlockSpec((TM_FFN, D_MODEL), lambda i, j: (jnp.maximum(i - tc, 0), 0))]
    one_spec = [pl.BlockSpec((TM_FFN, D_MODEL), lambda i, j: (i, 0))]
    mspec = lambda k: pl.BlockSpec((None, 8, D_MODEL), lambda i, j: (l, 0, k))
    in_specs = (split_specs if first else one_spec) + [
        mspec(k0), mspec(k0 + 1), mspec(k0 + 2),
        pl.BlockSpec((None, None, 1, D_MODEL), lambda i, j: (l, gi, 0, 0)),
        pl.BlockSpec((None, None, D_MODEL, TF_FFN), lambda i, j: (l, s, 0, j)),
        pl.BlockSpec((None, None, D_MODEL, TF_FFN), lambda i, j: (l, s, 0, j)),
        pl.BlockSpec((None, None, TF_FFN, D_MODEL), lambda i, j: (l, s, j, 0)),
    ]
    if last:
        out_shape = [jax.ShapeDtypeStruct((N_CTX, D_MODEL), F32), jax.ShapeDtypeStruct((N_LAT, D_MODEL), F32)]
        out_specs = split_specs
    else:
        out_shape = jax.ShapeDtypeStruct((N_TOK, D_MODEL), F32)
        out_specs = one_spec[0]
    return pl.pallas_call(
        functools.partial(_ffn_body, first=first, last=last),
        out_shape=out_shape,
        grid=(ni, nj),
        in_specs=in_specs,
        out_specs=out_specs,
        scratch_shapes=[pltpu.VMEM((TM_FFN, D_MODEL), BF16), pltpu.VMEM((TM_FFN, D_MODEL), F32)],
        compiler_params=_cparams(("arbitrary", "arbitrary")),
        name="ffn",
    )(*xs, mod, mod, mod, gn4, wg, wu, wd)


def _proj_body(*refs, rope, ctx, tiles_per_batch, n_alias):
    it = iter(refs)
    x_ref, sh_ref, sc_ref, gn_ref, w_ref, wup_ref, bgla_ref = (next(it) for _ in range(7))
    gaq_ref, gak_ref, gcq_ref, gck_ref = (next(it) for _ in range(4))
    s64a_ref, s64k_ref, s48_ref = (next(it) for _ in range(3))
    if rope:
        cosa_ref, sina_ref, cosc_ref, sinc_ref, ra_ref, rak_ref, rc_ref = (next(it) for _ in range(7))
    for _ in range(n_alias):
        next(it)
    qa_o, ka_o, va_o, qc_o, kc_o, vc_o, gq_o, gk_o, gv_o, gla_o, gr_o = (next(it) for _ in range(11))
    if ctx:
        ka32_o, va32_o, kc32_o, vc32_o = (next(it) for _ in range(4))

    i = pl.program_id(0)
    r = 0 if ctx else 1 + i // tiles_per_batch

    h = _norm_mod(x_ref[...], gn_ref[...], sc_ref[pl.ds(r, 1), :], sh_ref[pl.ds(r, 1), :])
    p = _dot(h, w_ref[...])

    aq = _seg_rms(p[:, P_AQ:P_AQ + A_W], s64a_ref[...], A_D) * gaq_ref[...]
    ak = _seg_rms(p[:, P_AK:P_AK + 128], s64k_ref[...], A_D) * gak_ref[...]
    av = p[:, P_AV:P_AV + 128]
    cq = _seg_rms(p[:, P_CQ:P_CQ + 384], s48_ref[...], C_DQ) * gcq_ref[...]
    ck = _seg_rms(p[:, P_CK:P_CK + 384], s48_ref[...], C_DQ) * gck_ref[...]
    cv = p[:, P_CV:P_CV + 384]
    if ctx:
        for o, val in ((ka32_o, ak), (va32_o, av), (kc32_o, ck), (vc32_o, cv)):
            if n_alias:
                o[...] = val
            else:
                o[0] = val
                o[1:] = jnp.zeros((DEPTH - 1,) + val.shape, F32)
    if rope:
        cosa = cosa_ref[...]
        sina = sina_ref[...]
        cosa3 = jnp.concatenate([cosa, cosa, cosa], axis=1)
        sina3 = jnp.concatenate([sina, sina, sina], axis=1)
        aq = aq * cosa3 + _dot(aq.astype(BF16), ra_ref[...]) * sina3
        ak = ak * cosa + _dot(ak.astype(BF16), rak_ref[...]) * sina
        cosc = cosc_ref[...]
        sinc = sinc_ref[...]
        cq = cq * cosc + _dot(cq.astype(BF16), rc_ref[...]) * sinc
        ck = ck * cosc + _dot(ck.astype(BF16), rc_ref[...]) * sinc
    qa_o[...] = (aq * (A_D ** -0.5 * LOG2E)).astype(BF16)
    ka_o[...] = ak.astype(BF16)
    va_o[...] = av.astype(BF16)
    qc_o[...] = (cq * (C_DQ ** -0.5 * LOG2E)).astype(BF16)
    kc_o[...] = ck.astype(BF16)
    vc_o[...] = cv.astype(BF16)

    gq_o[...] = p[:, P_BQ:P_BQ + B_KW] * (B_DK ** -0.5)
    gk_o[...] = p[:, P_BK:P_BK + B_KW]
    gv_o[...] = p[:, P_BV:P_BV + B_VW]
    gr_o[...] = _silu(p[:, P_BR:P_BR + B_VW])
    z = _dot(p[:, P_BG:P_BG + 128].astype(BF16), wup_ref[...]) + bgla_ref[...]
    log_sig = jnp.minimum(z, 0.0) - jnp.log1p(jnp.exp(-jnp.abs(z)))
    gla_o[...] = log_sig * (1.0 / B_TAU)


def _proj(x, prm, consts, rope_tabs, caches, *, l, ctx):
    n = N_CTX if ctx else N_LAT
    tm = TM_PROJ_CTX if ctx else TM_PROJ_LAT
    off = 0 if ctx else N_CTX // tm
    tiles_per_batch = (T_CTX if ctx else T_LAT) // tm
    rope = not ctx
    full = lambda shape: pl.BlockSpec(shape, lambda i: (0,) * len(shape))
    lay = lambda shape: pl.BlockSpec((None,) + shape, lambda i: (l,) + (0,) * len(shape))
    in_specs = [
        pl.BlockSpec((tm, D_MODEL), lambda i: (i + off, 0)),
        pl.BlockSpec((None, 8, D_MODEL), lambda i: (l, 0, 3)),
        pl.BlockSpec((None, 8, D_MODEL), lambda i: (l, 0, 4)),
        pl.BlockSpec((None, None, 1, D_MODEL), lambda i: (l, 1, 0, 0)),
        lay((D_MODEL, P_W)),
        lay((128, 2 * B_KW)),
        lay((1, 2 * B_KW)),
        lay((1, 384)), lay((1, 128)), lay((1, 384)), lay((1, 384)),
        full((384, 384)), full((128, 128)), full((384, 384)),
    ]
    args = [x, prm["mod"], prm["mod"], prm["gn4"], prm["w_p"], prm["wup"], prm["bgla"],
            prm["gaq"], prm["gak"], prm["gcq"], prm["gck"],
            consts["seg64_384"], consts["seg64_128"], consts["seg48_384"]]
    if rope:
        tpb = tiles_per_batch
        in_specs += [
            pl.BlockSpec((tm, 128), lambda i: (i % tpb, 0)),
            pl.BlockSpec((tm, 128), lambda i: (i % tpb, 0)),
            pl.BlockSpec((tm, 384), lambda i: (i % tpb, 0)),
            pl.BlockSpec((tm, 384), lambda i: (i % tpb, 0)),
            full((384, 384)), full((128, 128)), full((384, 384)),
        ]
        args += [*rope_tabs, consts["rot_a384"], consts["rot_a128"], consts["rot_c384"]]
    widths = [(384, BF16), (128, BF16), (128, BF16), (384, BF16), (384, BF16), (384, BF16),
              (B_KW, F32), (B_KW, F32), (B_VW, F32), (2 * B_KW, F32), (B_VW, F32)]
    out_shape = [jax.ShapeDtypeStruct((n, w), dt) for w, dt in widths]
    out_specs = [pl.BlockSpec((tm, w), lambda i: (i, 0)) for w, _ in widths]
    aliases = {}
    if ctx:
        assert tm == T_CTX
        for k, w in enumerate((128, 128, 384, 384)):
            if caches is not None:
                aliases[len(args)] = len(out_shape)
                in_specs.append(pl.BlockSpec(memory_space=pl.ANY))
                args.append(caches[k])
            out_shape.append(jax.ShapeDtypeStruct((NB_CTX, DEPTH, T_CTX, w), F32))
            if caches is not None:
                out_specs.append(pl.BlockSpec((None, None, T_CTX, w), lambda i: (i, l, 0, 0)))
            else:
                assert l == 0
                out_specs.append(pl.BlockSpec((None, DEPTH, T_CTX, w), lambda i: (i, 0, 0, 0)))
    return pl.pallas_call(
        functools.partial(_proj_body, rope=rope, ctx=ctx, tiles_per_batch=tiles_per_batch, n_alias=len(aliases)),
        out_shape=out_shape,
        grid=(n // tm,),
        in_specs=in_specs,
        out_specs=out_specs,
        input_output_aliases=aliases,
        compiler_params=_cparams(("parallel",)),
        name="proj_ctx" if ctx else "proj_lat",
    )(*args)


def _gla_tools(gq_ref, gk_ref, gv_ref, gla_ref):
    C = B_CHUNK
    row_i = lax.broadcasted_iota(jnp.int32, (C, B_KW), 0)
    rk = lax.broadcasted_iota(jnp.int32, (B_H * C, B_KW), 0) >> 6
    ck = lax.broadcasted_iota(jnp.int32, (B_H * C, B_KW), 1) >> 5
    hm_k = rk == ck
    rv = lax.broadcasted_iota(jnp.int32, (B_H * C, B_VW), 0) >> 6
    cv = lax.broadcasted_iota(jnp.int32, (B_H * C, B_VW), 1) >> 6
    hm_v = rv == cv
    ra = lax.broadcasted_iota(jnp.int32, (C, B_H * C), 0)
    ca = lax.broadcasted_iota(jnp.int32, (C, B_H * C), 1) & (C - 1)
    tri = (ca <= ra, ca >= ra)
    zero16 = jnp.zeros((), BF16)

    def expand_state(st):
        return jnp.where(hm_k, jnp.concatenate([st] * B_H, axis=0), 0.0)

    def chunk(d, start, S):
        rows = pl.ds(start if isinstance(start, int) else pl.multiple_of(start, C), C)
        q = gq_ref[rows, :]
        k = gk_ref[rows, :]
        v = gv_ref[rows, :].astype(BF16)
        la = gla_ref[rows, d * B_KW:(d + 1) * B_KW]
        p = la
        for s in (1, 2, 4, 8, 16, 32):
            p = p + jnp.where(row_i >= s, pltpu.roll(p, s, axis=0), 0.0)
        tot = p[C - 1:C, :]
        b = p if d == 0 else tot - p + la
        e = b - b[C // 2:C // 2 + 1, :]
        qt = (q * jnp.exp(e)).astype(BF16)
        kt = (k * jnp.exp(-e)).astype(BF16)
        qd = (q * jnp.exp(b)).astype(BF16)
        kd = (k * jnp.exp(tot - b)).astype(BF16)
        kbd = jnp.where(hm_k, jnp.concatenate([kt] * B_H, axis=0), zero16)
        a = _dot_nt(qt, kbd)
        a = jnp.where(tri[d], a, 0.0).astype(BF16)
        vbd = jnp.where(hm_v, jnp.concatenate([v] * B_H, axis=0), zero16)
        o = _dot(a, vbd) + _dot_nt(qd, S.astype(BF16))
        kv = _dot_tn(v, kd)
        S_new = S * jnp.exp(tot) + jnp.where(hm_k, kv, 0.0)
        return rows, o, S_new

    def collapse_state(S):
        Sm = jnp.where(hm_k, S, 0.0)
        acc = Sm[0:C, :]
        for hh in range(1, B_H):
            acc = acc + Sm[hh * C:(hh + 1) * C, :]
        return acc

    return expand_state, chunk, collapse_state


def _gla_gate(o, seg, gg, gr):
    ms = _split_dot(o * o, seg) * (1.0 / B_DV)
    return (o * lax.rsqrt(ms + EPS) * gg * gr).astype(BF16)


def _gla_body(*refs, seq, nbb, has_s0, has_alias, zero_fill):
    it = iter(refs)
    gq_ref, gk_ref, gv_ref, gla_ref, gr_ref = (next(it) for _ in range(5))
    s0_ref = next(it) if has_s0 else None
    gg_ref, seg_ref = next(it), next(it)
    if has_alias:
        next(it)
    ob_ref, sfin_ref, of_scr, or_scr = (next(it) for _ in range(4))
    expand_state, chunk, collapse_state = _gla_tools(gq_ref, gk_ref, gv_ref, gla_ref)
    n = seq // B_CHUNK

    def step(i, carry):
        new = []
        for bb in range(nbb):
            rows, o, Sf = chunk(0, bb * seq + i * B_CHUNK, carry[2 * bb])
            of_scr[rows, :] = o
            rows, o, Sb = chunk(1, bb * seq + (n - 1 - i) * B_CHUNK, carry[2 * bb + 1])
            or_scr[rows, :] = o
            new += [Sf, Sb]
        return tuple(new)

    if has_s0:
        carry = tuple(expand_state(s0_ref[bb, d]) for bb in range(nbb) for d in range(2))
    else:
        carry = tuple(jnp.zeros((B_VW, B_KW), F32) for _ in range(2 * nbb))
    if n <= 4:
        for i in range(n):
            carry = step(i, carry)
    else:
        carry = lax.fori_loop(0, n, step, carry, unroll=2)
    for bb in range(nbb):
        if zero_fill:
            sfin_ref[bb, 0, 0] = collapse_state(carry[2 * bb])
            sfin_ref[bb, 0, 1] = collapse_state(carry[2 * bb + 1])
            sfin_ref[bb, 1:] = jnp.zeros((DEPTH - 1, 2, B_DV, B_KW), F32)
        else:
            sfin_ref[bb, 0] = collapse_state(carry[2 * bb])
            sfin_ref[bb, 1] = collapse_state(carry[2 * bb + 1])

    RT = 256

    def fin(t, _):
        rows = pl.ds(pl.multiple_of(t * RT, RT), RT)
        ob_ref[rows, :] = _gla_gate(of_scr[rows, :] + or_scr[rows, :], seg_ref[...], gg_ref[...], gr_ref[rows, :])
        return 0

    lax.fori_loop(0, nbb * seq // RT, fin, 0)


def _gla(gq, gk, gv, gla, gr, s0, prm, seg, states, *, l, ctx):
    nb, seq, nbb = (NB_CTX, T_CTX, GLA_NBB_CTX) if ctx else (NB_LAT, T_LAT, GLA_NBB_LAT)
    tok = lambda w: pl.BlockSpec((nbb * seq, w), lambda b: (b, 0))
    in_specs = [tok(B_KW), tok(B_KW), tok(B_VW), tok(2 * B_KW), tok(B_VW)]
    args = [gq, gk, gv, gla, gr]
    if s0 is not None:
        in_specs.append(pl.BlockSpec((nbb, 2, B_DV, B_KW), lambda b: (b, 0, 0, 0)))
        args.append(s0)
    in_specs += [pl.BlockSpec((None, 1, B_VW), lambda b: (l, 0, 0)), pl.BlockSpec((B_VW, B_VW), lambda b: (0, 0))]
    args += [prm["ggl"], seg]
    aliases = {}
    if ctx:
        st_shape = jax.ShapeDtypeStruct((NB_CTX, DEPTH, 2, B_DV, B_KW), F32)
        st_spec = pl.BlockSpec((nbb, None, 2, B_DV, B_KW), lambda b: (b, l, 0, 0, 0))
        if states is None:
            assert l == 0
            st_spec = pl.BlockSpec((nbb, DEPTH, 2, B_DV, B_KW), lambda b: (b, 0, 0, 0, 0))
        if states is not None:
            aliases[len(args)] = 1
            in_specs.append(pl.BlockSpec(memory_space=pl.ANY))
            args.append(states)
    else:
        st_shape = jax.ShapeDtypeStruct((nb, 2, B_DV, B_KW), F32)
        st_spec = pl.BlockSpec((nbb, 2, B_DV, B_KW), lambda b: (b, 0, 0, 0))
    return pl.pallas_call(
        functools.partial(_gla_body, seq=seq, nbb=nbb, has_s0=s0 is not None, has_alias=bool(aliases),
                          zero_fill=ctx and states is None),
        out_shape=[jax.ShapeDtypeStruct((nb * seq, B_VW), BF16), st_shape],
        grid=(nb // nbb,),
        in_specs=in_specs,
        out_specs=[tok(B_VW), st_spec],
        scratch_shapes=[pltpu.VMEM((nbb * seq, B_VW), F32), pltpu.VMEM((nbb * seq, B_VW), F32)],
        input_output_aliases=aliases,
        compiler_params=_cparams(("parallel",)),
        name="gla_ctx" if ctx else "gla_lat",
    )(*args)


def _attn_body(*refs, lam_init, ctx, nbb):
    it = iter(refs)
    x_ref, qa_ref, qc_ref, ob_ref = (next(it) for _ in range(4))
    kv_new = [next(it) for _ in range(4)]
    kv_old = None if ctx else [next(it) for _ in range(4)]
    wout_ref, gt_ref, lam_ref, gco_ref, seg_ref, o_ref, mix_scr = (next(it) for _ in range(7))

    def keys_values(bb):
        if ctx:
            return [r[bb] for r in kv_new]
        return [jnp.concatenate([rn[bb], ro[...]], axis=0) for rn, ro in zip(kv_new, kv_old)]

    b = pl.program_id(0)
    r = 0 if ctx else 1 + b
    tq = x_ref.shape[0] // nbb
    a_stack, c_stack = (A_STACK_CTX, C_STACK_CTX) if ctx else (A_STACK_LAT, C_STACK_LAT)
    lane = lax.broadcasted_iota(jnp.int32, (1, 128), 1)
    lane2 = lax.broadcasted_iota(jnp.int32, (1, 256), 1)
    hmask = [(lane >= hh * A_D) & (lane < (hh + 1) * A_D) for hh in range(A_KV)]
    lm = lam_ref[...]
    lam = (jnp.exp(jnp.sum(lm[0:1] * lm[1:2], axis=-1, keepdims=True))
           - jnp.exp(jnp.sum(lm[2:3] * lm[3:4], axis=-1, keepdims=True)) + lam_init)

    def softmax_terms(s):
        e = jnp.exp2(s - jnp.max(s, axis=-1, keepdims=True))
        return e, jnp.sum(e, axis=-1, keepdims=True)

    for bb in range(nbb):
        _attn_mix_rows(bb, tq, qa_ref, qc_ref, ob_ref, keys_values(bb), gco_ref, seg_ref, mix_scr,
                       a_stack, c_stack, hmask, lane2, lam, lam_init, softmax_terms)
    mixed = _dot(mix_scr[...], wout_ref[...])
    o_ref[...] = x_ref[...] + gt_ref[pl.ds(r, 1), :] * mixed


def _attn_mix_rows(bb, tq, qa_ref, qc_ref, ob_ref, kv, gco_ref, seg_ref, mix_scr,
                   a_stack, c_stack, hmask, lane2, lam, lam_init, softmax_terms):
    rows = slice(bb * tq, (bb + 1) * tq)
    zero16 = jnp.zeros((), BF16)
    ka, va, kc_all, vc_all = kv
    va_ones = jnp.concatenate([va, jnp.ones_like(va)], axis=1)

    maps = [(g, hh) for g in range(A_G) for hh in range(A_KV)]
    acc = [jnp.zeros((tq, 128), F32) for _ in range(A_G)]
    for g0 in range(0, len(maps), a_stack):
        grp = maps[g0:g0 + a_stack]
        qs = jnp.concatenate(
            [jnp.where(hmask[hh], qa_ref[rows, g * 128:(g + 1) * 128], zero16) for g, hh in grp], axis=0)
        s = _dot_nt(qs, ka)
        e = jnp.exp2(s - jnp.max(s, axis=-1, keepdims=True)).astype(BF16)
        oe = _dot(e, va_ones)
        o = oe[:, :128] * (1.0 / oe[:, 128:])
        for k, (g, hh) in enumerate(grp):
            acc[g] = acc[g] + jnp.where(hmask[hh], o[k * tq:(k + 1) * tq], 0.0)
    for g in range(A_G):
        mix_scr[rows, g * 128:(g + 1) * 128] = acc[g].astype(BF16)

    mix_scr[rows, A_W:A_W + B_VW] = ob_ref[rows, :]

    outs = []
    for win in range(2):
        base = win * 128
        kc = kc_all[:, base:base + 256]
        vc = vc_all[:, base:base + 256]
        qc = qc_ref[rows, base:base + 256]
        ow = jnp.zeros((tq, 256), F32)
        cmaps = [(hh, mm) for hh in (2 * win, 2 * win + 1) for mm in range(2)]
        for g0 in range(0, len(cmaps), c_stack):
            grp = cmaps[g0:g0 + c_stack]
            qparts = []
            for hh, mm in grp:
                lo = (hh * 2 + mm) * C_DQ - base
                qparts.append(jnp.where((lane2 >= lo) & (lane2 < lo + C_DQ), qc, zero16))
            e, l = softmax_terms(_dot_nt(jnp.concatenate(qparts, axis=0), kc))
            ws, invs = [], []
            for k in range(0, len(grp), 2):
                l0 = l[k * tq:(k + 1) * tq]
                l1 = l[(k + 1) * tq:(k + 2) * tq]
                ws.append((e[k * tq:(k + 1) * tq] - (lam * l0 / l1) * e[(k + 1) * tq:(k + 2) * tq]).astype(BF16))
                invs.append(1.0 / l0)
            rr = _dot(jnp.concatenate(ws, axis=0) if len(ws) > 1 else ws[0], vc)
            for k in range(len(ws)):
                hh = grp[2 * k][0]
                vlo = hh * C_DV - base
                vm = (lane2 >= vlo) & (lane2 < vlo + C_DV)
                ow = ow + jnp.where(vm, rr[k * tq:(k + 1) * tq] * invs[k], 0.0)
        outs.append(ow)
    oc = jnp.concatenate([outs[0][:, :128], outs[0][:, 128:] + outs[1][:, :128], outs[1][:, 128:]], axis=1)
    ms = _dot((oc * oc).astype(BF16), seg_ref[...]) * (1.0 / C_DV)
    oc = oc * lax.rsqrt(ms + EPS) * gco_ref[...] * (1.0 - lam_init)
    mix_scr[rows, A_W + B_VW:] = oc.astype(BF16)


def _attn(x, qa, qc, ob, kv_new, kv_old, prm, seg96, *, l, lam_init, ctx):
    nb, seq, nbb, tq = (NB_CTX, T_CTX, ATT_NBB_CTX, T_CTX) if ctx else (NB_LAT, T_LAT, 1, TQ_ATT_LAT)
    nq = seq // tq
    rb = nbb * tq
    off = 0 if ctx else N_CTX // rb
    qspec = lambda w: pl.BlockSpec((rb, w), lambda b, q: (b * nq + q, 0))
    one_buf = None if ctx else pl.Buffered(1)
    kspec = lambda a: pl.BlockSpec((nbb,) + a.shape[1:], lambda b, q: (b, 0, 0), pipeline_mode=one_buf)
    ospec = lambda a: pl.BlockSpec((None, None) + a.shape[2:], lambda b, q: (b, l, 0, 0), pipeline_mode=one_buf)
    lay = lambda shape: pl.BlockSpec((None,) + shape, lambda b, q: (l,) + (0,) * len(shape),
                                     pipeline_mode=pl.Buffered(1))
    xspec = pl.BlockSpec((rb, D_MODEL), lambda b, q: (off + b * nq + q, 0))
    in_specs = [xspec, qspec(384), qspec(384), qspec(B_VW)] + [kspec(a) for a in kv_new]
    args = [x, qa, qc, ob, *kv_new]
    if kv_old is not None:
        in_specs += [ospec(a) for a in kv_old]
        args += list(kv_old)
    in_specs += [lay((D_MODEL, D_MODEL)),
                 pl.BlockSpec((None, 8, D_MODEL), lambda b, q: (l, 0, 5)),
                 lay((4, C_DQ)), lay((1, 384)),
                 pl.BlockSpec((384, 384), lambda b, q: (0, 0))]
    args += [prm["w_o"], prm["mod"], prm["lam_c"], prm["gco"], seg96]
    return pl.pallas_call(
        functools.partial(_attn_body, lam_init=lam_init, ctx=ctx, nbb=nbb),
        out_shape=jax.ShapeDtypeStruct((N_TOK, D_MODEL), F32),
        grid=(nb // nbb, nq),
        in_specs=in_specs,
        out_specs=xspec,
        scratch_shapes=[pltpu.VMEM((rb, D_MODEL), BF16)],
        input_output_aliases={0: 0},
        compiler_params=_cparams(("parallel", "arbitrary")),
        name="attn_ctx" if ctx else "attn_lat",
    )(*args)


def _block_ones(width, seg):
    idx = np.arange(width) // seg
    return jnp.asarray((idx[:, None] == idx[None, :]).astype(np.float32), dtype=BF16)


def _rot_matrix(width, half):
    m = np.zeros((width, width), np.float32)
    for j in range(width):
        if (j % (2 * half)) < half:
            m[j + half, j] = -1.0
        else:
            m[j - half, j] = 1.0
    return jnp.asarray(m, dtype=BF16)


def _rope_tables():
    t = np.arange(T_LAT)
    row = (t // GRID_W).astype(np.float32)
    col = (t % GRID_W).astype(np.float32)

    def tab(head_dim, n_heads):
        m = head_dim // 4
        freqs = ROPE_BASE ** (-jnp.arange(m, dtype=F32) / m)
        ang_r = jnp.asarray(row)[:, None] * freqs[None, :]
        ang_c = jnp.asarray(col)[:, None] * freqs[None, :]
        cs = jnp.concatenate([jnp.cos(ang_r)] * 2 + [jnp.cos(ang_c)] * 2, axis=1)
        sn = jnp.concatenate([jnp.sin(ang_r)] * 2 + [jnp.sin(ang_c)] * 2, axis=1)
        return jnp.tile(cs, (1, n_heads)), jnp.tile(sn, (1, n_heads))

    cosa, sina = tab(A_D, 2)
    cosc, sinc = tab(C_DQ, 2 * C_H)
    return cosa, sina, cosc, sinc


def kernel(x_prompt, x_sample, c, cache_a_k, cache_a_v, cache_c_k, cache_c_v, state_gla, c_ctx, w_ada, b_ada,
           g_norm, w_ffn_gate, w_ffn_up, w_ffn_down, w_in, g_a_q, g_a_k, w_gla_up, b_gla, g_gla, g_c_q, g_c_k,
           lam_c, g_c_out, w_out):
    lam_inits = [0.8 - 0.6 * math.exp(-0.3 * l) for l in range(DEPTH)]

    consts = {
        "seg64_384": _block_ones(384, 64), "seg64_128": _block_ones(128, 64),
        "seg48_384": _block_ones(384, 48), "seg96_384": _block_ones(384, 96),
        "seg64_256": _block_ones(256, 64),
        "rot_a384": _rot_matrix(384, 16), "rot_a128": _rot_matrix(128, 16), "rot_c384": _rot_matrix(384, 12),
    }
    rope_tabs = _rope_tables()

    wq = w_in[:, :, 0:384].reshape(DEPTH, D_MODEL, A_KV, A_G, A_D).transpose(0, 1, 3, 2, 4).reshape(DEPTH, D_MODEL, 384)
    seg = lambda a, b: w_in[:, :, a:b]
    w_p = jnp.concatenate([
        wq, seg(384, 512), seg(512, 640),
        seg(640, 768), seg(768, 896), seg(896, 1152),
        seg(1184, 1440),
        seg(1440, 1824), seg(1824, 2208), seg(2208, 2592),
        seg(1152, 1184), jnp.zeros((DEPTH, D_MODEL, 96), F32),
    ], axis=2).astype(BF16)
    wo_a = w_out[:, 0:384].reshape(DEPTH, A_KV, A_G, A_D, D_MODEL).transpose(0, 2, 1, 3, 4).reshape(DEPTH, 384, D_MODEL)
    wup = jnp.zeros((DEPTH, 128, 2 * B_KW), F32)
    wup = wup.at[:, 0:B_RANK, 0:B_KW].set(w_gla_up[:, 0]).at[:, B_RANK:2 * B_RANK, B_KW:].set(w_gla_up[:, 1])
    cond8 = jnp.zeros((8, D_MODEL), F32).at[0].set(c_ctx).at[1:3].set(c)
    prm = {
        "mod": _adaln(cond8, w_ada, b_ada),
        "gn4": g_norm.reshape(DEPTH, 3, 1, D_MODEL),
        "w_p": w_p,
        "w_o": jnp.concatenate([wo_a, w_out[:, 384:]], axis=1).astype(BF16),
        "wup": wup.astype(BF16),
        "bgla": b_gla.reshape(DEPTH, 1, 2 * B_KW),
        "gaq": jnp.tile(g_a_q, (1, 6)).reshape(DEPTH, 1, 384),
        "gak": jnp.tile(g_a_k, (1, 2)).reshape(DEPTH, 1, 128),
        "gcq": jnp.tile(g_c_q.reshape(DEPTH, 96), (1, 4)).reshape(DEPTH, 1, 384),
        "gck": jnp.tile(g_c_k.reshape(DEPTH, 96), (1, 4)).reshape(DEPTH, 1, 384),
        "gco": jnp.tile(g_c_out, (1, 4)).reshape(DEPTH, 1, 384),
        "ggl": jnp.tile(g_gla, (1, 4)).reshape(DEPTH, 1, 256),
        "lam_c": lam_c,
    }

    caches, states = None, None
    s0_lat = jnp.swapaxes(state_gla.reshape(NB_LAT, DEPTH, 2, B_KW, B_DV), 3, 4)
    kv_old = [a.reshape(NB_LAT, DEPTH, PAST_LEN, a.shape[-2] * a.shape[-1]).astype(BF16)
              for a in (cache_a_k, cache_a_v, cache_c_k, cache_c_v)]
    ffn = functools.partial(_ffn, mod=prm["mod"], gn4=prm["gn4"], wg=w_ffn_gate, wu=w_ffn_up, wd=w_ffn_down)
    x = None
    for l in range(DEPTH):
        if l == 0:
            x = ffn((x_prompt.reshape(N_CTX, D_MODEL), x_sample.reshape(N_LAT, D_MODEL)), l=0, s=0, first=True)
        else:
            x = ffn((x,), l=l, s=0)

        outs = _proj(x, prm, consts, None, caches, l=l, ctx=True)
        qa, ka, va, qc, kc, vc, gq, gk, gv, gla, gr = outs[:11]
        caches = list(outs[11:])
        ob, states = _gla(gq, gk, gv, gla, gr, None, prm, consts["seg64_256"], states, l=l, ctx=True)
        r3 = lambda a, n, t: a.reshape(n, t, a.shape[-1])
        x = _attn(x, qa, qc, ob, [r3(a, NB_CTX, T_CTX) for a in (ka, va, kc, vc)], None, prm, consts["seg96_384"],
                  l=l, lam_init=lam_inits[l], ctx=True)

        qa, ka, va, qc, kc, vc, gq, gk, gv, gla, gr = _proj(x, prm, consts, rope_tabs, None, l=l, ctx=False)
        ob, _ = _gla(gq, gk, gv, gla, gr, s0_lat[:, l], prm, consts["seg64_256"], None, l=l, ctx=False)
        x = _attn(x, qa, qc, ob, [r3(a, NB_LAT, T_LAT) for a in (ka, va, kc, vc)], kv_old, prm, consts["seg96_384"],
                  l=l, lam_init=lam_inits[l], ctx=False)

        if l == DEPTH - 1:
            y_prompt, y_sample = ffn((x,), l=l, s=1, last=True)
        else:
            x = ffn((x,), l=l, s=1)

    return (y_prompt.reshape(NB_CTX, T_CTX, D_MODEL), y_sample.reshape(NB_LAT, T_LAT, D_MODEL),
            caches[0].reshape(NB_CTX, DEPTH, T_CTX, A_KV, A_D), caches[1].reshape(NB_CTX, DEPTH, T_CTX, A_KV, A_D),
            caches[2].reshape(NB_CTX, DEPTH, T_CTX, C_H, 2 * C_DQ), caches[3].reshape(NB_CTX, DEPTH, T_CTX, C_H, C_DV),
            jnp.swapaxes(states, 3, 4).reshape(NB_CTX, DEPTH, 2, B_H, B_DK, B_DV))
```

```python
import functools
import math

import numpy as np
import jax
import jax.numpy as jnp
from jax import lax
from jax.experimental import pallas as pl
from jax.experimental.pallas import tpu as pltpu

F32 = jnp.float32
BF16 = jnp.bfloat16

D_MODEL = 1024
D_FF = 2816
DEPTH = 4
N_MOD = 9
EPS = 1e-6
ROPE_BASE = 10000.0
GRID_W = 64
LOG2E = math.log2(math.e)

NB_CTX, T_CTX = 32, 256
NB_LAT, T_LAT = 2, 2048
PAST_LEN = 512
N_CTX = NB_CTX * T_CTX
N_LAT = NB_LAT * T_LAT
N_TOK = N_CTX + N_LAT

A_KV, A_G, A_D = 2, 3, 64
A_W = A_KV * A_G * A_D
B_H, B_DK, B_DV = 4, 32, 64
B_KW = B_H * B_DK
B_VW = B_H * B_DV
B_RANK = 16
B_TAU = 16.0
B_CHUNK = 64
C_H, C_DQ, C_DV = 4, 48, 96

P_AQ, P_AK, P_AV = 0, 384, 512
P_BQ, P_BK, P_BV, P_BR = 640, 768, 896, 1152
P_CQ, P_CK, P_CV = 1408, 1792, 2176
P_BG = 2560
P_W = 2688

TM_FFN = 1024
TF_FFN = 256
TM_PROJ_CTX = 256
TM_PROJ_LAT = 512
TQ_ATT_LAT = 512
ATT_NBB_CTX = 4
TN_ADA = 1152
GLA_NBB_CTX = 8
GLA_NBB_LAT = 2
A_STACK_CTX, A_STACK_LAT = 6, 2
C_STACK_CTX, C_STACK_LAT = 4, 2
VMEM_LIMIT = 56 * 1024 * 1024


def _cparams(sem):
    return pltpu.CompilerParams(dimension_semantics=sem, vmem_limit_bytes=VMEM_LIMIT)


def _dot(a, b):
    return jnp.dot(a, b, preferred_element_type=F32)


def _dot_nt(a, b):
    return lax.dot_general(a, b, (((1,), (1,)), ((), ())), preferred_element_type=F32)


def _dot_tn(a, b):
    return lax.dot_general(a, b, (((0,), (0,)), ((), ())), preferred_element_type=F32)


def _split_dot(x, m):
    hi = x.astype(BF16)
    lo = (x - hi.astype(F32)).astype(BF16)
    return _dot(hi, m) + _dot(lo, m)


def _seg_rms(x, seg, n):
    ms = _dot((x * x).astype(BF16), seg) * (1.0 / n)
    return x * lax.rsqrt(ms + EPS)


def _silu(x):
    return x * jax.nn.sigmoid(x)


def _norm_mod(x, gn, sc, sh):
    ms = jnp.mean(x * x, axis=-1, keepdims=True)
    return ((x * lax.rsqrt(ms + EPS)) * (gn * (1.0 + sc)) + sh).astype(BF16)


def _adaln_body(cond_ref, w_ref, b_ref, o_ref):
    c = cond_ref[...]
    sc = _silu(c).astype(BF16)
    o_ref[0] = _dot(sc, w_ref[0].astype(BF16)) + b_ref[0]


def _adaln(cond8, w_ada, b_ada):
    nj = (N_MOD * D_MODEL) // TN_ADA
    return pl.pallas_call(
        _adaln_body,
        out_shape=jax.ShapeDtypeStruct((DEPTH, 8, N_MOD * D_MODEL), F32),
        grid=(DEPTH, nj),
        in_specs=[
            pl.BlockSpec((8, D_MODEL), lambda l, j: (0, 0)),
            pl.BlockSpec((1, D_MODEL, TN_ADA), lambda l, j: (l, 0, j)),
            pl.BlockSpec((1, 1, TN_ADA), lambda l, j: (l, 0, j)),
        ],
        out_specs=pl.BlockSpec((1, 8, TN_ADA), lambda l, j: (l, 0, j)),
        compiler_params=_cparams(("parallel", "parallel")),
        name="adaln",
    )(cond8, w_ada, b_ada.reshape(DEPTH, 1, N_MOD * D_MODEL))


FFN_TILES_CTX = N_CTX // TM_FFN
FFN_TILES_PER_LAT = T_LAT // TM_FFN


def _ffn_body(*refs, first, last):
    it = iter(refs)
    x_refs = (next(it), next(it)) if first else (next(it),)
    sh_ref, sc_ref, gt_ref, gn_ref, wg_ref, wu_ref, wd_ref = (next(it) for _ in range(7))
    o_refs = (next(it), next(it)) if last else (next(it),)
    h_scr, acc_scr = next(it), next(it)

    i = pl.program_id(0)
    j = pl.program_id(1)
    is_ctx = i < FFN_TILES_CTX
    r = jnp.where(is_ctx, 0, 1 + jnp.maximum(i - FFN_TILES_CTX, 0) // FFN_TILES_PER_LAT)

    def on_tile(pred, n_variants, fn):
        if n_variants == 1:
            pl.when(pred)(lambda: fn(0))
        else:
            pl.when(pred & is_ctx)(lambda: fn(0))
            pl.when(pred & jnp.logical_not(is_ctx))(lambda: fn(1))

    def prologue(k):
        h_scr[...] = _norm_mod(x_refs[k][...], gn_ref[...], sc_ref[pl.ds(r, 1), :], sh_ref[pl.ds(r, 1), :])
        acc_scr[...] = jnp.zeros_like(acc_scr)

    on_tile(j == 0, len(x_refs), prologue)

    def hidden_tile():
        h = h_scr[...]
        g = _dot(h, wg_ref[...].astype(BF16))
        u = _dot(h, wu_ref[...].astype(BF16))
        a = (_silu(g) * u).astype(BF16)
        return _dot(a, wd_ref[...].astype(BF16))

    is_last = j == pl.num_programs(1) - 1

    @pl.when(jnp.logical_not(is_last))
    def _():
        acc_scr[...] += hidden_tile()

    def epilogue(k):
        x_ref = x_refs[k if first else 0]
        o_ref = o_refs[k if last else 0]
        o_ref[...] = x_ref[...] + 0.5 * gt_ref[pl.ds(r, 1), :] * (acc_scr[...] + hidden_tile())

    on_tile(is_last, max(len(x_refs), len(o_refs)), epilogue)


def _ffn(xs, mod, gn4, wg, wu, wd, *, l, s, first=False, last=False):
    ni = N_TOK // TM_FFN
    nj = D_FF // TF_FFN
    k0 = 6 * s
    gi = 2 * s
    tc = FFN_TILES_CTX
    split_specs = [pl.BlockSpec((TM_FFN, D_MODEL), lambda i, j: (jnp.minimum(i, tc - 1), 0)),
                   pl.BlockSpec((TM_FFN, D_MODEL), lambda i, j: (jnp.maximum(i - tc, 0), 0))]
    one_spec = [pl.BlockSpec((TM_FFN, D_MODEL), lambda i, j: (i, 0))]
    mspec = lambda k: pl.BlockSpec((None, 8, D_MODEL), lambda i, j: (l, 0, k))
    in_specs = (split_specs if first else one_spec) + [
        mspec(k0), mspec(k0 + 1), mspec(k0 + 2),
        pl.BlockSpec((None, None, 1, D_MODEL), lambda i, j: (l, gi, 0, 0)),
        pl.BlockSpec((None, None, D_MODEL, TF_FFN), lambda i, j: (l, s, 0, j)),
        pl.BlockSpec((None, None, D_MODEL, TF_FFN), lambda i, j: (l, s, 0, j)),
        pl.BlockSpec((None, None, TF_FFN, D_MODEL), lambda i, j: (l, s, j, 0)),
    ]
    if last:
        out_shape = [jax.ShapeDtypeStruct((N_CTX, D_MODEL), F32), jax.ShapeDtypeStruct((N_LAT, D_MODEL), F32)]
        out_specs = split_specs
    else:
        out_shape = jax.ShapeDtypeStruct((N_TOK, D_MODEL), F32)
        out_specs = one_spec[0]
    return pl.pallas_call(
        functools.partial(_ffn_body, first=first, last=last),
        out_shape=out_shape,
        grid=(ni, nj),
        in_specs=in_specs,
        out_specs=out_specs,
        scratch_shapes=[pltpu.VMEM((TM_FFN, D_MODEL), BF16), pltpu.VMEM((TM_FFN, D_MODEL), F32)],
        compiler_params=_cparams(("arbitrary", "arbitrary")),
        name="ffn",
    )(*xs, mod, mod, mod, gn4, wg, wu, wd)


def _proj_body(*refs, rope, ctx, tiles_per_batch, n_alias):
    it = iter(refs)
    x_ref, sh_ref, sc_ref, gn_ref, w_ref, wup_ref, bgla_ref = (next(it) for _ in range(7))
    gaq_ref, gak_ref, gcq_ref, gck_ref = (next(it) for _ in range(4))
    s64a_ref, s64k_ref, s48_ref = (next(it) for _ in range(3))
    if rope:
        cosa_ref, sina_ref, cosc_ref, sinc_ref, ra_ref, rak_ref, rc_ref = (next(it) for _ in range(7))
    for _ in range(n_alias):
        next(it)
    qa_o, ka_o, va_o, qc_o, kc_o, vc_o, gq_o, gk_o, gv_o, gla_o, gr_o = (next(it) for _ in range(11))
    if ctx:
        ka32_o, va32_o, kc32_o, vc32_o = (next(it) for _ in range(4))

    i = pl.program_id(0)
    r = 0 if ctx else 1 + i // tiles_per_batch

    h = _norm_mod(x_ref[...], gn_ref[...], sc_ref[pl.ds(r, 1), :], sh_ref[pl.ds(r, 1), :])
    p = _dot(h, w_ref[...])

    aq = _seg_rms(p[:, P_AQ:P_AQ + A_W], s64a_ref[...], A_D) * gaq_ref[...]
    ak = _seg_rms(p[:, P_AK:P_AK + 128], s64k_ref[...], A_D) * gak_ref[...]
    av = p[:, P_AV:P_AV + 128]
    cq = _seg_rms(p[:, P_CQ:P_CQ + 384], s48_ref[...], C_DQ) * gcq_ref[...]
    ck = _seg_rms(p[:, P_CK:P_CK + 384], s48_ref[...], C_DQ) * gck_ref[...]
    cv = p[:, P_CV:P_CV + 384]
    if ctx:
        for o, val in ((ka32_o, ak), (va32_o, av), (kc32_o, ck), (vc32_o, cv)):
            if n_alias:
                o[...] = val
            else:
                o[0] = val
                o[1:] = jnp.zeros((DEPTH - 1,) + val.shape, F32)
    if rope:
        cosa = cosa_ref[...]
        sina = sina_ref[...]
        cosa3 = jnp.concatenate([cosa, cosa, cosa], axis=1)
        sina3 = jnp.concatenate([sina, sina, sina], axis=1)
        aq = aq * cosa3 + _dot(aq.astype(BF16), ra_ref[...]) * sina3
        ak = ak * cosa + _dot(ak.astype(BF16), rak_ref[...]) * sina
        cosc = cosc_ref[...]
        sinc = sinc_ref[...]
        cq = cq * cosc + _dot(cq.astype(BF16), rc_ref[...]) * sinc
        ck = ck * cosc + _dot(ck.astype(BF16), rc_ref[...]) * sinc
    qa_o[...] = (aq * (A_D ** -0.5 * LOG2E)).astype(BF16)
    ka_o[...] = ak.astype(BF16)
    va_o[...] = av.astype(BF16)
    qc_o[...] = (cq * (C_DQ ** -0.5 * LOG2E)).astype(BF16)
    kc_o[...] = ck.astype(BF16)
    vc_o[...] = cv.astype(BF16)

    gq_o[...] = p[:, P_BQ:P_BQ + B_KW] * (B_DK ** -0.5)
    gk_o[...] = p[:, P_BK:P_BK + B_KW]
    gv_o[...] = p[:, P_BV:P_BV + B_VW]
    gr_o[...] = _silu(p[:, P_BR:P_BR + B_VW])
    z = _dot(p[:, P_BG:P_BG + 128].astype(BF16), wup_ref[...]) + bgla_ref[...]
    log_sig = jnp.minimum(z, 0.0) - jnp.log1p(jnp.exp(-jnp.abs(z)))
    gla_o[...] = log_sig * (1.0 / B_TAU)


def _proj(x, prm, consts, rope_tabs, caches, *, l, ctx):
    n = N_CTX if ctx else N_LAT
    tm = TM_PROJ_CTX if ctx else TM_PROJ_LAT
    off = 0 if ctx else N_CTX // tm
    tiles_per_batch = (T_CTX if ctx else T_LAT) // tm
    rope = not ctx
    full = lambda shape: pl.BlockSpec(shape, lambda i: (0,) * len(shape))
    lay = lambda shape: pl.BlockSpec((None,) + shape, lambda i: (l,) + (0,) * len(shape))
    in_specs = [
        pl.BlockSpec((tm, D_MODEL), lambda i: (i + off, 0)),
        pl.BlockSpec((None, 8, D_MODEL), lambda i: (l, 0, 3)),
        pl.BlockSpec((None, 8, D_MODEL), lambda i: (l, 0, 4)),
        pl.BlockSpec((None, None, 1, D_MODEL), lambda i: (l, 1, 0, 0)),
        lay((D_MODEL, P_W)),
        lay((128, 2 * B_KW)),
        lay((1, 2 * B_KW)),
        lay((1, 384)), lay((1, 128)), lay((1, 384)), lay((1, 384)),
        full((384, 384)), full((128, 128)), full((384, 384)),
    ]
    args = [x, prm["mod"], prm["mod"], prm["gn4"], prm["w_p"], prm["wup"], prm["bgla"],
            prm["gaq"], prm["gak"], prm["gcq"], prm["gck"],
            consts["seg64_384"], consts["seg64_128"], consts["seg48_384"]]
    if rope:
        tpb = tiles_per_batch
        in_specs += [
            pl.BlockSpec((tm, 128), lambda i: (i % tpb, 0)),
            pl.BlockSpec((tm, 128), lambda i: (i % tpb, 0)),
            pl.BlockSpec((tm, 384), lambda i: (i % tpb, 0)),
            pl.BlockSpec((tm, 384), lambda i: (i % tpb, 0)),
            full((384, 384)), full((128, 128)), full((384, 384)),
        ]
        args += [*rope_tabs, consts["rot_a384"], consts["rot_a128"], consts["rot_c384"]]
    widths = [(384, BF16), (128, BF16), (128, BF16), (384, BF16), (384, BF16), (384, BF16),
              (B_KW, F32), (B_KW, F32), (B_VW, F32), (2 * B_KW, F32), (B_VW, F32)]
    out_shape = [jax.ShapeDtypeStruct((n, w), dt) for w, dt in widths]
    out_specs = [pl.BlockSpec((tm, w), lambda i: (i, 0)) for w, _ in widths]
    aliases = {}
    if ctx:
        assert tm == T_CTX
        for k, w in enumerate((128, 128, 384, 384)):
            if caches is not None:
                aliases[len(args)] = len(out_shape)
                in_specs.append(pl.BlockSpec(memory_space=pl.ANY))
                args.append(caches[k])
            out_shape.append(jax.ShapeDtypeStruct((NB_CTX, DEPTH, T_CTX, w), F32))
            if caches is not None:
                out_specs.append(pl.BlockSpec((None, None, T_CTX, w), lambda i: (i, l, 0, 0)))
            else:
                assert l == 0
                out_specs.append(pl.BlockSpec((None, DEPTH, T_CTX, w), lambda i: (i, 0, 0, 0)))
    return pl.pallas_call(
        functools.partial(_proj_body, rope=rope, ctx=ctx, tiles_per_batch=tiles_per_batch, n_alias=len(aliases)),
        out_shape=out_shape,
        grid=(n // tm,),
        in_specs=in_specs,
        out_specs=out_specs,
        input_output_aliases=aliases,
        compiler_params=_cparams(("parallel",)),
        name="proj_ctx" if ctx else "proj_lat",
    )(*args)


def _gla_tools(gq_ref, gk_ref, gv_ref, gla_ref):
    C = B_CHUNK
    row_i = lax.broadcasted_iota(jnp.int32, (C, B_KW), 0)
    rk = lax.broadcasted_iota(jnp.int32, (B_H * C, B_KW), 0) >> 6
    ck = lax.broadcasted_iota(jnp.int32, (B_H * C, B_KW), 1) >> 5
    hm_k = rk == ck
    rv = lax.broadcasted_iota(jnp.int32, (B_H * C, B_VW), 0) >> 6
    cv = lax.broadcasted_iota(jnp.int32, (B_H * C, B_VW), 1) >> 6
    hm_v = rv == cv
    ra = lax.broadcasted_iota(jnp.int32, (C, B_H * C), 0)
    ca = lax.broadcasted_iota(jnp.int32, (C, B_H * C), 1) & (C - 1)
    tri = (ca <= ra, ca >= ra)
    zero16 = jnp.zeros((), BF16)

    def expand_state(st):
        return jnp.where(hm_k, jnp.concatenate([st] * B_H, axis=0), 0.0)

    def chunk(d, start, S):
        rows = pl.ds(start if isinstance(start, int) else pl.multiple_of(start, C), C)
        q = gq_ref[rows, :]
        k = gk_ref[rows, :]
        v = gv_ref[rows, :].astype(BF16)
        la = gla_ref[rows, d * B_KW:(d + 1) * B_KW]
        p = la
        for s in (1, 2, 4, 8, 16, 32):
            p = p + jnp.where(row_i >= s, pltpu.roll(p, s, axis=0), 0.0)
        tot = p[C - 1:C, :]
        b = p if d == 0 else tot - p + la
        e = b - b[C // 2:C // 2 + 1, :]
        qt = (q * jnp.exp(e)).astype(BF16)
        kt = (k * jnp.exp(-e)).astype(BF16)
        qd = (q * jnp.exp(b)).astype(BF16)
        kd = (k * jnp.exp(tot - b)).astype(BF16)
        kbd = jnp.where(hm_k, jnp.concatenate([kt] * B_H, axis=0), zero16)
        a = _dot_nt(qt, kbd)
        a = jnp.where(tri[d], a, 0.0).astype(BF16)
        vbd = jnp.where(hm_v, jnp.concatenate([v] * B_H, axis=0), zero16)
        o = _dot(a, vbd) + _dot_nt(qd, S.astype(BF16))
        kv = _dot_tn(v, kd)
        S_new = S * jnp.exp(tot) + jnp.where(hm_k, kv, 0.0)
        return rows, o, S_new

    def collapse_state(S):
        Sm = jnp.where(hm_k, S, 0.0)
        acc = Sm[0:C, :]
        for hh in range(1, B_H):
            acc = acc + Sm[hh * C:(hh + 1) * C, :]
        return acc

    return expand_state, chunk, collapse_state


def _gla_gate(o, seg, gg, gr):
    ms = _split_dot(o * o, seg) * (1.0 / B_DV)
    return (o * lax.rsqrt(ms + EPS) * gg * gr).astype(BF16)


def _gla_body(*refs, seq, nbb, has_s0, has_alias, zero_fill):
    it = iter(refs)
    gq_ref, gk_ref, gv_ref, gla_ref, gr_ref = (next(it) for _ in range(5))
    s0_ref = next(it) if has_s0 else None
    gg_ref, seg_ref = next(it), next(it)
    if has_alias:
        next(it)
    ob_ref, sfin_ref, of_scr, or_scr = (next(it) for _ in range(4))
    expand_state, chunk, collapse_state = _gla_tools(gq_ref, gk_ref, gv_ref, gla_ref)
    n = seq // B_CHUNK

    def step(i, carry):
        new = []
        for bb in range(nbb):
            rows, o, Sf = chunk(0, bb * seq + i * B_CHUNK, carry[2 * bb])
            of_scr[rows, :] = o
            rows, o, Sb = chunk(1, bb * seq + (n - 1 - i) * B_CHUNK, carry[2 * bb + 1])
            or_scr[rows, :] = o
            new += [Sf, Sb]
        return tuple(new)

    if has_s0:
        carry = tuple(expand_state(s0_ref[bb, d]) for bb in range(nbb) for d in range(2))
    else:
        carry = tuple(jnp.zeros((B_VW, B_KW), F32) for _ in range(2 * nbb))
    if n <= 4:
        for i in range(n):
            carry = step(i, carry)
    else:
        carry = lax.fori_loop(0, n, step, carry, unroll=2)
    for bb in range(nbb):
        if zero_fill:
            sfin_ref[bb, 0, 0] = collapse_state(carry[2 * bb])
            sfin_ref[bb, 0, 1] = collapse_state(carry[2 * bb + 1])
            sfin_ref[bb, 1:] = jnp.zeros((DEPTH - 1, 2, B_DV, B_KW), F32)
        else:
            sfin_ref[bb, 0] = collapse_state(carry[2 * bb])
            sfin_ref[bb, 1] = collapse_state(carry[2 * bb + 1])

    RT = 256

    def fin(t, _):
        rows = pl.ds(pl.multiple_of(t * RT, RT), RT)
        ob_ref[rows, :] = _gla_gate(of_scr[rows, :] + or_scr[rows, :], seg_ref[...], gg_ref[...], gr_ref[rows, :])
        return 0

    lax.fori_loop(0, nbb * seq // RT, fin, 0)


def _gla(gq, gk, gv, gla, gr, s0, prm, seg, states, *, l, ctx):
    nb, seq, nbb = (NB_CTX, T_CTX, GLA_NBB_CTX) if ctx else (NB_LAT, T_LAT, GLA_NBB_LAT)
    tok = lambda w: pl.BlockSpec((nbb * seq, w), lambda b: (b, 0))
    in_specs = [tok(B_KW), tok(B_KW), tok(B_VW), tok(2 * B_KW), tok(B_VW)]
    args = [gq, gk, gv, gla, gr]
    if s0 is not None:
        in_specs.append(pl.BlockSpec((nbb, 2, B_DV, B_KW), lambda b: (b, 0, 0, 0)))
        args.append(s0)
    in_specs += [pl.BlockSpec((None, 1, B_VW), lambda b: (l, 0, 0)), pl.BlockSpec((B_VW, B_VW), lambda b: (0, 0))]
    args += [prm["ggl"], seg]
    aliases = {}
    if ctx:
        st_shape = jax.ShapeDtypeStruct((NB_CTX, DEPTH, 2, B_DV, B_KW), F32)
        st_spec = pl.BlockSpec((nbb, None, 2, B_DV, B_KW), lambda b: (b, l, 0, 0, 0))
        if states is None:
            assert l == 0
            st_spec = pl.BlockSpec((nbb, DEPTH, 2, B_DV, B_KW), lambda b: (b, 0, 0, 0, 0))
        if states is not None:
            aliases[len(args)] = 1
            in_specs.append(pl.BlockSpec(memory_space=pl.ANY))
            args.append(states)
    else:
        st_shape = jax.ShapeDtypeStruct((nb, 2, B_DV, B_KW), F32)
        st_spec = pl.BlockSpec((nbb, 2, B_DV, B_KW), lambda b: (b, 0, 0, 0))
    return pl.pallas_call(
        functools.partial(_gla_body, seq=seq, nbb=nbb, has_s0=s0 is not None, has_alias=bool(aliases),
                          zero_fill=ctx and states is None),
        out_shape=[jax.ShapeDtypeStruct((nb * seq, B_VW), BF16), st_shape],
        grid=(nb // nbb,),
        in_specs=in_specs,
        out_specs=[tok(B_VW), st_spec],
        scratch_shapes=[pltpu.VMEM((nbb * seq, B_VW), F32), pltpu.VMEM((nbb * seq, B_VW), F32)],
        input_output_aliases=aliases,
        compiler_params=_cparams(("parallel",)),
        name="gla_ctx" if ctx else "gla_lat",
    )(*args)


def _attn_body(*refs, lam_init, ctx, nbb):
    it = iter(refs)
    x_ref, qa_ref, qc_ref, ob_ref = (next(it) for _ in range(4))
    kv_new = [next(it) for _ in range(4)]
    kv_old = None if ctx else [next(it) for _ in range(4)]
    wout_ref, gt_ref, lam_ref, gco_ref, seg_ref, o_ref, mix_scr = (next(it) for _ in range(7))

    def keys_values(bb):
        if ctx:
            return [r[bb] for r in kv_new]
        return [jnp.concatenate([rn[bb], ro[...]], axis=0) for rn, ro in zip(kv_new, kv_old)]

    b = pl.program_id(0)
    r = 0 if ctx else 1 + b
    tq = x_ref.shape[0] // nbb
    a_stack, c_stack = (A_STACK_CTX, C_STACK_CTX) if ctx else (A_STACK_LAT, C_STACK_LAT)
    lane = lax.broadcasted_iota(jnp.int32, (1, 128), 1)
    lane2 = lax.broadcasted_iota(jnp.int32, (1, 256), 1)
    hmask = [(lane >= hh * A_D) & (lane < (hh + 1) * A_D) for hh in range(A_KV)]
    lm = lam_ref[...]
    lam = (jnp.exp(jnp.sum(lm[0:1] * lm[1:2], axis=-1, keepdims=True))
           - jnp.exp(jnp.sum(lm[2:3] * lm[3:4], axis=-1, keepdims=True)) + lam_init)

    def softmax_terms(s):
        e = jnp.exp2(s - jnp.max(s, axis=-1, keepdims=True))
        return e, jnp.sum(e, axis=-1, keepdims=True)

    for bb in range(nbb):
        _attn_mix_rows(bb, tq, qa_ref, qc_ref, ob_ref, keys_values(bb), gco_ref, seg_ref, mix_scr,
                       a_stack, c_stack, hmask, lane2, lam, lam_init, softmax_terms)
    mixed = _dot(mix_scr[...], wout_ref[...])
    o_ref[...] = x_ref[...] + gt_ref[pl.ds(r, 1), :] * mixed


def _attn_mix_rows(bb, tq, qa_ref, qc_ref, ob_ref, kv, gco_ref, seg_ref, mix_scr,
                   a_stack, c_stack, hmask, lane2, lam, lam_init, softmax_terms):
    rows = slice(bb * tq, (bb + 1) * tq)
    zero16 = jnp.zeros((), BF16)
    ka, va, kc_all, vc_all = kv
    va_ones = jnp.concatenate([va, jnp.ones_like(va)], axis=1)

    maps = [(g, hh) for g in range(A_G) for hh in range(A_KV)]
    acc = [jnp.zeros((tq, 128), F32) for _ in range(A_G)]
    for g0 in range(0, len(maps), a_stack):
        grp = maps[g0:g0 + a_stack]
        qs = jnp.concatenate(
            [jnp.where(hmask[hh], qa_ref[rows, g * 128:(g + 1) * 128], zero16) for g, hh in grp], axis=0)
        s = _dot_nt(qs, ka)
        e = jnp.exp2(s - jnp.max(s, axis=-1, keepdims=True)).astype(BF16)
        oe = _dot(e, va_ones)
        o = oe[:, :128] * (1.0 / oe[:, 128:])
        for k, (g, hh) in enumerate(grp):
            acc[g] = acc[g] + jnp.where(hmask[hh], o[k * tq:(k + 1) * tq], 0.0)
    for g in range(A_G):
        mix_scr[rows, g * 128:(g + 1) * 128] = acc[g].astype(BF16)

    mix_scr[rows, A_W:A_W + B_VW] = ob_ref[rows, :]

    outs = []
    for win in range(2):
        base = win * 128
        kc = kc_all[:, base:base + 256]
        vc = vc_all[:, base:base + 256]
        qc = qc_ref[rows, base:base + 256]
        ow = jnp.zeros((tq, 256), F32)
        cmaps = [(hh, mm) for hh in (2 * win, 2 * win + 1) for mm in range(2)]
        for g0 in range(0, len(cmaps), c_stack):
            grp = cmaps[g0:g0 + c_stack]
            qparts = []
            for hh, mm in grp:
                lo = (hh * 2 + mm) * C_DQ - base
                qparts.append(jnp.where((lane2 >= lo) & (lane2 < lo + C_DQ), qc, zero16))
            e, l = softmax_terms(_dot_nt(jnp.concatenate(qparts, axis=0), kc))
            ws, invs = [], []
            for k in range(0, len(grp), 2):
                l0 = l[k * tq:(k + 1) * tq]
                l1 = l[(k + 1) * tq:(k + 2) * tq]
                ws.append((e[k * tq:(k + 1) * tq] - (lam * l0 / l1) * e[(k + 1) * tq:(k + 2) * tq]).astype(BF16))
                invs.append(1.0 / l0)
            rr = _dot(jnp.concatenate(ws, axis=0) if len(ws) > 1 else ws[0], vc)
            for k in range(len(ws)):
                hh = grp[2 * k][0]
                vlo = hh * C_DV - base
                vm = (lane2 >= vlo) & (lane2 < vlo + C_DV)
                ow = ow + jnp.where(vm, rr[k * tq:(k + 1) * tq] * invs[k], 0.0)
        outs.append(ow)
    oc = jnp.concatenate([outs[0][:, :128], outs[0][:, 128:] + outs[1][:, :128], outs[1][:, 128:]], axis=1)
    ms = _dot((oc * oc).astype(BF16), seg_ref[...]) * (1.0 / C_DV)
    oc = oc * lax.rsqrt(ms + EPS) * gco_ref[...] * (1.0 - lam_init)
    mix_scr[rows, A_W + B_VW:] = oc.astype(BF16)


def _attn(x, qa, qc, ob, kv_new, kv_old, prm, seg96, *, l, lam_init, ctx):
    nb, seq, nbb, tq = (NB_CTX, T_CTX, ATT_NBB_CTX, T_CTX) if ctx else (NB_LAT, T_LAT, 1, TQ_ATT_LAT)
    nq = seq // tq
    rb = nbb * tq
    off = 0 if ctx else N_CTX // rb
    qspec = lambda w: pl.BlockSpec((rb, w), lambda b, q: (b * nq + q, 0))
    one_buf = None if ctx else pl.Buffered(1)
    kspec = lambda a: pl.BlockSpec((nbb,) + a.shape[1:], lambda b, q: (b, 0, 0), pipeline_mode=one_buf)
    ospec = lambda a: pl.BlockSpec((None, None) + a.shape[2:], lambda b, q: (b, l, 0, 0), pipeline_mode=one_buf)
    lay = lambda shape: pl.BlockSpec((None,) + shape, lambda b, q: (l,) + (0,) * len(shape),
                                     pipeline_mode=pl.Buffered(1))
    xspec = pl.BlockSpec((rb, D_MODEL), lambda b, q: (off + b * nq + q, 0))
    in_specs = [xspec, qspec(384), qspec(384), qspec(B_VW)] + [kspec(a) for a in kv_new]
    args = [x, qa, qc, ob, *kv_new]
    if kv_old is not None:
        in_specs += [ospec(a) for a in kv_old]
        args += list(kv_old)
    in_specs += [lay((D_MODEL, D_MODEL)),
                 pl.BlockSpec((None, 8, D_MODEL), lambda b, q: (l, 0, 5)),
                 lay((4, C_DQ)), lay((1, 384)),
                 pl.BlockSpec((384, 384), lambda b, q: (0, 0))]
    args += [prm["w_o"], prm["mod"], prm["lam_c"], prm["gco"], seg96]
    return pl.pallas_call(
        functools.partial(_attn_body, lam_init=lam_init, ctx=ctx, nbb=nbb),
        out_shape=jax.ShapeDtypeStruct((N_TOK, D_MODEL), F32),
        grid=(nb // nbb, nq),
        in_specs=in_specs,
        out_specs=xspec,
        scratch_shapes=[pltpu.VMEM((rb, D_MODEL), BF16)],
        input_output_aliases={0: 0},
        compiler_params=_cparams(("parallel", "arbitrary")),
        name="attn_ctx" if ctx else "attn_lat",
    )(*args)


def _block_ones(width, seg):
    idx = np.arange(width) // seg
    return jnp.asarray((idx[:, None] == idx[None, :]).astype(np.float32), dtype=BF16)


def _rot_matrix(width, half):
    m = np.zeros((width, width), np.float32)
    for j in range(width):
        if (j % (2 * half)) < half:
            m[j + half, j] = -1.0
        else:
            m[j - half, j] = 1.0
    return jnp.asarray(m, dtype=BF16)


def _rope_tables():
    t = np.arange(T_LAT)
    row = (t // GRID_W).astype(np.float32)
    col = (t % GRID_W).astype(np.float32)

    def tab(head_dim, n_heads):
        m = head_dim // 4
        freqs = ROPE_BASE ** (-jnp.arange(m, dtype=F32) / m)
        ang_r = jnp.asarray(row)[:, None] * freqs[None, :]
        ang_c = jnp.asarray(col)[:, None] * freqs[None, :]
        cs = jnp.concatenate([jnp.cos(ang_r)] * 2 + [jnp.cos(ang_c)] * 2, axis=1)
        sn = jnp.concatenate([jnp.sin(ang_r)] * 2 + [jnp.sin(ang_c)] * 2, axis=1)
        return jnp.tile(cs, (1, n_heads)), jnp.tile(sn, (1, n_heads))

    cosa, sina = tab(A_D, 2)
    cosc, sinc = tab(C_DQ, 2 * C_H)
    return cosa, sina, cosc, sinc


def kernel(x_prompt, x_sample, c, cache_a_k, cache_a_v, cache_c_k, cache_c_v, state_gla, c_ctx, w_ada, b_ada,
           g_norm, w_ffn_gate, w_ffn_up, w_ffn_down, w_in, g_a_q, g_a_k, w_gla_up, b_gla, g_gla, g_c_q, g_c_k,
           lam_c, g_c_out, w_out):
    lam_inits = [0.8 - 0.6 * math.exp(-0.3 * l) for l in range(DEPTH)]

    consts = {
        "seg64_384": _block_ones(384, 64), "seg64_128": _block_ones(128, 64),
        "seg48_384": _block_ones(384, 48), "seg96_384": _block_ones(384, 96),
        "seg64_256": _block_ones(256, 64),
        "rot_a384": _rot_matrix(384, 16), "rot_a128": _rot_matrix(128, 16), "rot_c384": _rot_matrix(384, 12),
    }
    rope_tabs = _rope_tables()

    wq = w_in[:, :, 0:384].reshape(DEPTH, D_MODEL, A_KV, A_G, A_D).transpose(0, 1, 3, 2, 4).reshape(DEPTH, D_MODEL, 384)
    seg = lambda a, b: w_in[:, :, a:b]
    w_p = jnp.concatenate([
        wq, seg(384, 512), seg(512, 640),
        seg(640, 768), seg(768, 896), seg(896, 1152),
        seg(1184, 1440),
        seg(1440, 1824), seg(1824, 2208), seg(2208, 2592),
        seg(1152, 1184), jnp.zeros((DEPTH, D_MODEL, 96), F32),
    ], axis=2).astype(BF16)
    wo_a = w_out[:, 0:384].reshape(DEPTH, A_KV, A_G, A_D, D_MODEL).transpose(0, 2, 1, 3, 4).reshape(DEPTH, 384, D_MODEL)
    wup = jnp.zeros((DEPTH, 128, 2 * B_KW), F32)
    wup = wup.at[:, 0:B_RANK, 0:B_KW].set(w_gla_up[:, 0]).at[:, B_RANK:2 * B_RANK, B_KW:].set(w_gla_up[:, 1])
    cond8 = jnp.zeros((8, D_MODEL), F32).at[0].set(c_ctx).at[1:3].set(c)
    prm = {
        "mod": _adaln(cond8, w_ada, b_ada),
        "gn4": g_norm.reshape(DEPTH, 3, 1, D_MODEL),
        "w_p": w_p,
        "w_o": jnp.concatenate([wo_a, w_out[:, 384:]], axis=1).astype(BF16),
        "wup": wup.astype(BF16),
        "bgla": b_gla.reshape(DEPTH, 1, 2 * B_KW),
        "gaq": jnp.tile(g_a_q, (1, 6)).reshape(DEPTH, 1, 384),
        "gak": jnp.tile(g_a_k, (1, 2)).reshape(DEPTH, 1, 128),
        "gcq": jnp.tile(g_c_q.reshape(DEPTH, 96), (1, 4)).reshape(DEPTH, 1, 384),
        "gck": jnp.tile(g_c_k.reshape(DEPTH, 96), (1, 4)).reshape(DEPTH, 1, 384),
        "gco": jnp.tile(g_c_out, (1, 4)).reshape(DEPTH, 1, 384),
        "ggl": jnp.tile(g_gla, (1, 4)).reshape(DEPTH, 1, 256),
        "lam_c": lam_c,
    }

    caches, states = None, None
    s0_lat = jnp.swapaxes(state_gla.reshape(NB_LAT, DEPTH, 2, B_KW, B_DV), 3, 4)
    kv_old = [a.reshape(NB_LAT, DEPTH, PAST_LEN, a.shape[-2] * a.shape[-1]).astype(BF16)
              for a in (cache_a_k, cache_a_v, cache_c_k, cache_c_v)]
    ffn = functools.partial(_ffn, mod=prm["mod"], gn4=prm["gn4"], wg=w_ffn_gate, wu=w_ffn_up, wd=w_ffn_down)
    x = None
    for l in range(DEPTH):
        if l == 0:
            x = ffn((x_prompt.reshape(N_CTX, D_MODEL), x_sample.reshape(N_LAT, D_MODEL)), l=0, s=0, first=True)
        else:
            x = ffn((x,), l=l, s=0)

        outs = _proj(x, prm, consts, None, caches, l=l, ctx=True)
        qa, ka, va, qc, kc, vc, gq, gk, gv, gla, gr = outs[:11]
        caches = list(outs[11:])
        ob, states = _gla(gq, gk, gv, gla, gr, None, prm, consts["seg64_256"], states, l=l, ctx=True)
        r3 = lambda a, n, t: a.reshape(n, t, a.shape[-1])
        x = _attn(x, qa, qc, ob, [r3(a, NB_CTX, T_CTX) for a in (ka, va, kc, vc)], None, prm, consts["seg96_384"],
                  l=l, lam_init=lam_inits[l], ctx=True)

        qa, ka, va, qc, kc, vc, gq, gk, gv, gla, gr = _proj(x, prm, consts, rope_tabs, None, l=l, ctx=False)
        ob, _ = _gla(gq, gk, gv, gla, gr, s0_lat[:, l], prm, consts["seg64_256"], None, l=l, ctx=False)
        x = _attn(x, qa, qc, ob, [r3(a, NB_LAT, T_LAT) for a in (ka, va, kc, vc)], kv_old, prm, consts["seg96_384"],
                  l=l, lam_init=lam_inits[l], ctx=False)

        if l == DEPTH - 1:
            y_prompt, y_sample = ffn((x,), l=l, s=1, last=True)
        else:
            x = ffn((x,), l=l, s=1)

    return (y_prompt.reshape(NB_CTX, T_CTX, D_MODEL), y_sample.reshape(NB_LAT, T_LAT, D_MODEL),
            caches[0].reshape(NB_CTX, DEPTH, T_CTX, A_KV, A_D), caches[1].reshape(NB_CTX, DEPTH, T_CTX, A_KV, A_D),
            caches[2].reshape(NB_CTX, DEPTH, T_CTX, C_H, 2 * C_DQ), caches[3].reshape(NB_CTX, DEPTH, T_CTX, C_H, C_DV),
            jnp.swapaxes(states, 3, 4).reshape(NB_CTX, DEPTH, 2, B_H, B_DK, B_DV))
```
